```python
import jax, jax.numpy as jnp
from jax import lax
import numpy as np

D_MODEL = 1024
BATCH = 4
SEQ = 4096
DEPTH = 2
DEC_BATCH = 128
DEC_SEQ = 8
PAST_LEN = 2048
PAGE_SIZE = 128

HEAD_DIM = 64
N_MIXERS = 4
GROUP_W = D_MODEL // N_MIXERS
GROUP_HEADS = GROUP_W // HEAD_DIM
CHUNK = 128
Q_BLOCK = 128
CMP_LEN = 32
CMP_STRIDE = 16
CMP_HID = 128
SEL_LEN = 64
SEL_TOPK = 16
WINDOW = 512
CONV_W = 3
N_EXPERTS = 16
N_GROUPS = 4
EXP_PER_GROUP = N_EXPERTS // N_GROUPS
TOP_K = 2
D_EXPERT = D_MODEL // 4
FORGET_BIAS_INIT = 2.0
LN_EPS = 1e-5
SCALE = HEAD_DIM ** -0.5
NEG = -1e30
BIG = 1e30
DEEPNORM_ALPHA = (2 * DEPTH) ** 0.25
DEEPNORM_BETA = (8 * DEPTH) ** -0.25
SPLIT_SIZES = (GROUP_W, GROUP_W, GROUP_W, GROUP_W, GROUP_W, GROUP_HEADS, GROUP_W, 6 * HEAD_DIM, 3 * GROUP_HEADS, GROUP_W, GROUP_W, GROUP_W)
P_IN = sum(SPLIT_SIZES)

kernel_name = 'hybrid_gmlp_fox_nsa_conv_moe_step'

f32 = jnp.float32


def layer_norm(x, g, b):
    xf = x.astype(f32)
    mu = xf.mean(-1, keepdims=True)
    var = jnp.square(xf - mu).mean(-1, keepdims=True)
    return ((xf - mu) * lax.rsqrt(var + LN_EPS) * g.astype(f32) + b.astype(f32)).astype(x.dtype)


def in_proj(x, w):
    h = x @ w
    return jnp.split(h, [int(i) for i in np.cumsum(SPLIT_SIZES)[:-1]], axis=-1)


def heads(a):
    return a.reshape(*a.shape[:-1], GROUP_HEADS, HEAD_DIM)


def to_blocks(a, n):
    return a.reshape(a.shape[0], n, Q_BLOCK, *a.shape[2:]).swapaxes(0, 1)


def from_blocks(a):
    return a.swapaxes(0, 1).reshape(a.shape[1], -1, *a.shape[3:])


def gmlp_uv(a_u, a_v):
    u = heads(jax.nn.gelu(a_u))
    v = heads(jax.nn.gelu(a_v)).astype(f32)
    mu = v.mean(-1, keepdims=True)
    var = jnp.square(v - mu).mean(-1, keepdims=True)
    return u, ((v - mu) * lax.rsqrt(var + LN_EPS)).astype(a_v.dtype)


def gmlp_gate(u, v, w_s, b_s):
    n = v.shape[2]
    w = w_s[:, :n, :n] * jnp.tril(jnp.ones((n, n), w_s.dtype))
    z = jnp.einsum('hts,bcshd->bcthd', w, v) + b_s[:, :n].T[None, None, :, :, None]
    return u * z


def fox_attend(q, cq, qpos, k, v, ck, kpos):
    s = jnp.einsum('bqhd,bkhd->bhqk', q, k, preferred_element_type=f32) * SCALE
    s = s + jnp.swapaxes(cq, 1, 2)[..., :, None] - jnp.swapaxes(ck, 1, 2)[..., None, :]
    s = jnp.where((kpos[None, :] <= qpos[:, None])[None, None], s, NEG)
    p = jax.nn.softmax(s, axis=-1)
    return jnp.einsum('bhqk,bkhd->bqhd', p.astype(v.dtype), v)


def nsa_compress(rows, pe, w1, w2):
    bsz, t, d = rows.shape
    n_cmp = (t - CMP_LEN) // CMP_STRIDE + 1
    idx = np.arange(n_cmp)[:, None] * CMP_STRIDE + np.arange(CMP_LEN)[None, :]
    blocks = (rows[:, idx] + pe).reshape(bsz, n_cmp, CMP_LEN * d)
    return jax.nn.gelu(blocks @ w1) @ w2


def nsa_static(t, n_sel):
    n_cmp = (t - CMP_LEN) // CMP_STRIDE + 1
    start = np.arange(n_cmp) * CMP_STRIDE
    sstart = np.arange(n_sel) * SEL_LEN
    ov = np.minimum(start[:, None] + CMP_LEN, sstart[None, :] + SEL_LEN) - np.maximum(start[:, None], sstart[None, :])
    ov = np.clip(ov, 0, None) / CMP_STRIDE
    return jnp.asarray(start + CMP_LEN - 1, jnp.int32), jnp.asarray(ov, f32)


def nsa_attend(q, qpos, gates, kc, vc, cmp_end, sel_w, ks_blk, vs_blk, kw, vw, wpos):
    bsz, tq = q.shape[0], q.shape[1]
    vis_c = cmp_end[None, :] <= qpos[:, None]
    s_c = jnp.einsum('bqhd,bnd->bqhn', q, kc, preferred_element_type=f32) * SCALE
    p_c = jax.nn.softmax(jnp.where(vis_c[None, :, None, :], s_c, NEG), axis=-1)
    p_c = p_c * jnp.any(vis_c, axis=-1)[None, :, None, None].astype(f32)
    o_c = jnp.einsum('bqhn,bnd->bqhd', p_c.astype(vc.dtype), vc)
    n_sel = ks_blk.shape[1]
    imp = jnp.einsum('bqhn,nm->bqm', p_c, sel_w)
    blk = jnp.arange(n_sel)
    cur = qpos // SEL_LEN
    valid = blk[None, :] * SEL_LEN <= qpos[:, None]
    forced = (blk[None, :] == 0) | (blk[None, :] == cur[:, None]) | (blk[None, :] == cur[:, None] - 1)
    imp = jnp.where(valid[None], jnp.where(forced[None], BIG, imp), -BIG)
    _, top_i = lax.top_k(imp, min(SEL_TOPK, n_sel))
    ok = valid[jnp.arange(tq)[None, :, None], top_i]
    bidx = jnp.arange(bsz)[:, None, None]
    kg = ks_blk[bidx, top_i]
    vg = vs_blk[bidx, top_i]
    pos_g = top_i[..., None] * SEL_LEN + jnp.arange(SEL_LEN)
    m_g = ok[..., None] & (pos_g <= qpos[None, :, None, None])
    s_s = jnp.einsum('bqhd,bqkld->bqhkl', q, kg, preferred_element_type=f32) * SCALE
    p_s = jax.nn.softmax(jnp.where(m_g[:, :, None], s_s, NEG), axis=(-2, -1))
    o_s = jnp.einsum('bqhkl,bqkld->bqhd', p_s.astype(vg.dtype), vg)
    m_w = (wpos[None, :] <= qpos[:, None]) & (wpos[None, :] > qpos[:, None] - WINDOW) & (wpos[None, :] >= 0)
    s_w = jnp.einsum('bqhd,bkd->bqhk', q, kw, preferred_element_type=f32) * SCALE
    p_w = jax.nn.softmax(jnp.where(m_w[None, :, None, :], s_w, NEG), axis=-1)
    o_w = jnp.einsum('bqhk,bkd->bqhd', p_w.astype(vw.dtype), vw)
    g = gates.astype(o_c.dtype)
    o = g[:, :, 0, :, None] * o_c + g[:, :, 1, :, None] * o_s + g[:, :, 2, :, None] * o_w
    return o.reshape(bsz, tq, GROUP_W)


def short_conv(z, past, w):
    zp = jnp.concatenate([past.astype(z.dtype), z], axis=1)
    t = z.shape[1]
    y = sum(w[j] * zp[:, j:j + t] for j in range(CONV_W))
    return y, zp[:, -(CONV_W - 1):]


def moe(x, w_router, w_gate, w_up, w_down):
    shp = x.shape
    xt = x.reshape(-1, shp[-1])
    probs = jax.nn.softmax(jnp.matmul(xt, w_router, preferred_element_type=f32), axis=-1)
    grp_top, _ = lax.top_k(probs.reshape(-1, N_GROUPS, EXP_PER_GROUP), TOP_K)
    g_sel = jnp.argmax(grp_top.sum(-1), axis=-1)
    in_grp = (jnp.arange(N_EXPERTS) // EXP_PER_GROUP)[None, :] == g_sel[:, None]
    top_w, top_i = lax.top_k(jnp.where(in_grp, probs, -1.0), TOP_K)
    top_w = top_w / top_w.sum(-1, keepdims=True)
    gate = jnp.einsum('nk,nke->ne', top_w, jax.nn.one_hot(top_i, N_EXPERTS, dtype=f32))
    h = jax.nn.silu(jnp.einsum('nd,edf->nef', xt, w_gate)) * jnp.einsum('nd,edf->nef', xt, w_up)
    y = jnp.einsum('nef,efd->nd', h * gate.astype(h.dtype)[..., None], w_down)
    return y.reshape(shp)


def post_block(x, mixed, w_out, ln_g, ln_b, w_router, w_gate, w_up, w_down):
    x = layer_norm(DEEPNORM_ALPHA * x + mixed @ w_out, ln_g[0], ln_b[0])
    return layer_norm(DEEPNORM_ALPHA * x + moe(x, w_router, w_gate, w_up, w_down), ln_g[1], ln_b[1])


def prompt_mixers(x, w_in, fox_bias, w_s, b_s, cmp_pe, cmp_w1, cmp_w2, conv_w):
    bsz, seq, _ = x.shape
    n_chunk = seq // CHUNK
    n_blk = seq // Q_BLOCK
    a_u, a_v, b_q, b_k, b_v, b_f, c_q, c_kv, c_g, d_h, d_b, d_c = in_proj(x, w_in)
    u, v = gmlp_uv(a_u, a_v)
    cshape = (bsz, n_chunk, CHUNK, GROUP_HEADS, HEAD_DIM)
    y_a = gmlp_gate(u.reshape(cshape), v.reshape(cshape), w_s, b_s).reshape(bsz, seq, GROUP_W)
    q, k, vv = heads(b_q), heads(b_k), heads(b_v)
    logf = jax.nn.log_sigmoid(b_f.astype(f32) + fox_bias.astype(f32))
    c = jnp.cumsum(logf, axis=1)
    kpos = jnp.arange(seq)
    starts = jnp.arange(n_blk) * Q_BLOCK

    def fox_block(args):
        qi, ci, s0 = args
        return fox_attend(qi, ci, s0 + jnp.arange(Q_BLOCK), k, vv, c, kpos)

    y_b = from_blocks(lax.map(fox_block, (to_blocks(q, n_blk), to_blocks(c, n_blk), starts))).reshape(bsz, seq, GROUP_W)
    qc = heads(c_q)
    gates = jax.nn.sigmoid(c_g.astype(f32)).reshape(bsz, seq, 3, GROUP_HEADS)
    kc, vc, ks, vs, kw, vw = jnp.split(c_kv, 6, axis=-1)
    k_cmp = nsa_compress(kc, cmp_pe[0], cmp_w1[0], cmp_w2[0])
    v_cmp = nsa_compress(vc, cmp_pe[1], cmp_w1[1], cmp_w2[1])
    n_sel = seq // SEL_LEN
    cmp_end, sel_w = nsa_static(seq, n_sel)
    ks_blk = ks.reshape(bsz, n_sel, SEL_LEN, HEAD_DIM)
    vs_blk = vs.reshape(bsz, n_sel, SEL_LEN, HEAD_DIM)
    pad = ((0, 0), (WINDOW, 0), (0, 0))
    kw_pad = jnp.pad(kw, pad)
    vw_pad = jnp.pad(vw, pad)

    def nsa_block(args):
        qi, gi, s0 = args
        kwi = lax.dynamic_slice_in_dim(kw_pad, s0, WINDOW + Q_BLOCK, axis=1)
        vwi = lax.dynamic_slice_in_dim(vw_pad, s0, WINDOW + Q_BLOCK, axis=1)
        wpos = s0 - WINDOW + jnp.arange(WINDOW + Q_BLOCK)
        return nsa_attend(qi, s0 + jnp.arange(Q_BLOCK), gi, k_cmp, v_cmp, cmp_end, sel_w, ks_blk, vs_blk, kwi, vwi, wpos)

    y_c = from_blocks(lax.map(nsa_block, (to_blocks(qc, n_blk), to_blocks(gates, n_blk), starts)))
    y_conv, conv_state = short_conv(d_c * d_h, jnp.zeros((bsz, CONV_W - 1, GROUP_W), x.dtype), conv_w)
    y_d = d_b * y_conv
    mixed = jnp.concatenate([y_a, y_b, y_c, y_d], axis=-1)
    win_keep = min(WINDOW, seq)
    return mixed, (jnp.stack([k, vv], axis=2), logf, jnp.stack([kc, vc, ks, vs], axis=2),
                   jnp.stack([kw, vw], axis=2)[:, seq - win_keep:], conv_state)


def sample_mixers(x, fox_kv_c, fox_lf_c, nsa_kv_c, win_c, conv_c, page_table,
                  w_in, fox_bias, w_s, b_s, cmp_pe, cmp_w1, cmp_w2, conv_w):
    bsz, tn, _ = x.shape
    a_u, a_v, b_q, b_k, b_v, b_f, c_q, c_kv, c_g, d_h, d_b, d_c = in_proj(x, w_in)
    u, v = gmlp_uv(a_u, a_v)
    y_a = gmlp_gate(u[:, None], v[:, None], w_s, b_s).reshape(bsz, tn, GROUP_W)
    q, k, vv = heads(b_q), heads(b_k), heads(b_v)
    logf = jax.nn.log_sigmoid(b_f.astype(f32) + fox_bias.astype(f32))
    kv_past = fox_kv_c[page_table].reshape(bsz, -1, 2, GROUP_HEADS, HEAD_DIM).astype(k.dtype)
    lf_past = fox_lf_c[page_table].reshape(bsz, -1, GROUP_HEADS).astype(f32)
    past = kv_past.shape[1]
    kv_new = jnp.stack([k, vv], axis=2)
    kv_all = jnp.concatenate([kv_past, kv_new], axis=1)
    c_all = jnp.cumsum(jnp.concatenate([lf_past, logf], axis=1), axis=1)
    qpos = past + jnp.arange(tn)
    y_b = fox_attend(q, c_all[:, past:], qpos, kv_all[:, :, 0], kv_all[:, :, 1], c_all,
                     jnp.arange(past + tn)).reshape(bsz, tn, GROUP_W)
    qc = heads(c_q)
    gates = jax.nn.sigmoid(c_g.astype(f32)).reshape(bsz, tn, 3, GROUP_HEADS)
    kc, vc, ks, vs, kw, vw = jnp.split(c_kv, 6, axis=-1)
    rows_new = jnp.stack([kc, vc, ks, vs], axis=2)
    rows_past = nsa_kv_c[page_table].reshape(bsz, -1, 4, HEAD_DIM).astype(rows_new.dtype)
    rows = jnp.concatenate([rows_past, rows_new], axis=1)
    t_all = rows.shape[1]
    k_cmp = nsa_compress(rows[:, :, 0], cmp_pe[0], cmp_w1[0], cmp_w2[0])
    v_cmp = nsa_compress(rows[:, :, 1], cmp_pe[1], cmp_w1[1], cmp_w2[1])
    n_sel = -(-t_all // SEL_LEN)
    sel = jnp.pad(rows[:, :, 2:4], ((0, 0), (0, n_sel * SEL_LEN - t_all), (0, 0), (0, 0)))
    sel = sel.reshape(bsz, n_sel, SEL_LEN, 2, HEAD_DIM)
    cmp_end, sel_w = nsa_static(t_all, n_sel)
    win_keep = win_c.shape[1]
    win = jnp.concatenate([win_c.astype(kw.dtype), jnp.stack([kw, vw], axis=2)], axis=1)
    wpos = past - win_keep + jnp.arange(win_keep + tn)
    y_c = nsa_attend(qc, qpos, gates, k_cmp, v_cmp, cmp_end, sel_w, sel[:, :, :, 0], sel[:, :, :, 1],
                     win[:, :, 0], win[:, :, 1], wpos)
    y_conv, conv_state = short_conv(d_c * d_h, conv_c, conv_w)
    y_d = d_b * y_conv
    mixed = jnp.concatenate([y_a, y_b, y_c, y_d], axis=-1)
    return mixed, (kv_new, logf, rows_new, win[:, -win_keep:], conv_state, v)


def setup_inputs(seed: int = 0) -> dict:
    key = jax.random.key(seed)
    ks = jax.random.split(key, 24)
    n_pages = PAST_LEN // PAGE_SIZE
    n_pool = (DEC_BATCH * n_pages * 5) // 4
    win_cache = min(WINDOW, PAST_LEN)

    def nrm(k, shape, s):
        return jax.random.normal(k, shape, f32) * s

    offs = np.cumsum((0,) + SPLIT_SIZES)
    col = np.ones(P_IN, np.float32)
    col[offs[4]:offs[4] + GROUP_W] = DEEPNORM_BETA
    for j in (1, 3, 5):
        o = offs[7] + j * HEAD_DIM
        col[o:o + HEAD_DIM] = DEEPNORM_BETA
    page_table = jax.random.permutation(ks[7], n_pool)[:DEC_BATCH * n_pages]
    return {
        'x_prompt': nrm(ks[0], (BATCH, SEQ, D_MODEL), 1.0),
        'x_sample': nrm(ks[1], (DEC_BATCH, DEC_SEQ, D_MODEL), 1.0),
        'cache_fox_kv': nrm(ks[2], (DEPTH, n_pool, PAGE_SIZE, 2, GROUP_HEADS, HEAD_DIM), 1.0),
        'cache_fox_logf': jax.nn.log_sigmoid(nrm(ks[3], (DEPTH, n_pool, PAGE_SIZE, GROUP_HEADS), 1.0) + FORGET_BIAS_INIT),
        'cache_nsa_kv': nrm(ks[4], (DEPTH, n_pool, PAGE_SIZE, 4, HEAD_DIM), 1.0),
        'state_nsa_win': nrm(ks[5], (DEPTH, DEC_BATCH, win_cache, 2, HEAD_DIM), 1.0),
        'state_conv': nrm(ks[6], (DEPTH, DEC_BATCH, CONV_W - 1, GROUP_W), 1.0),
        'page_table': page_table.reshape(DEC_BATCH, n_pages).astype(jnp.int32),
        'w_in': nrm(ks[8], (DEPTH, D_MODEL, P_IN), D_MODEL ** -0.5) * jnp.asarray(col),
        'fox_b_f': FORGET_BIAS_INIT + nrm(ks[9], (DEPTH, GROUP_HEADS), 0.1),
        'gmlp_w_s': nrm(ks[10], (DEPTH, GROUP_HEADS, CHUNK, CHUNK), CHUNK ** -0.5),
        'gmlp_b_s': 1.0 + nrm(ks[11], (DEPTH, GROUP_HEADS, CHUNK), 0.1),
        'nsa_cmp_pe': nrm(ks[12], (DEPTH, 2, CMP_LEN, HEAD_DIM), 0.1),
        'nsa_cmp_w1': nrm(ks[13], (DEPTH, 2, CMP_LEN * HEAD_DIM, CMP_HID), (CMP_LEN * HEAD_DIM) ** -0.5),
        'nsa_cmp_w2': nrm(ks[14], (DEPTH, 2, CMP_HID, HEAD_DIM), CMP_HID ** -0.5),
        'conv_w': nrm(ks[15], (DEPTH, CONV_W, GROUP_W), CONV_W ** -0.5),
        'w_out': nrm(ks[16], (DEPTH, D_MODEL, D_MODEL), D_MODEL ** -0.5 * DEEPNORM_BETA),
        'ln_g': 1.0 + nrm(ks[17], (DEPTH, 2, D_MODEL), 0.05),
        'ln_b': nrm(ks[18], (DEPTH, 2, D_MODEL), 0.02),
        'w_router': nrm(ks[19], (D_MODEL, N_EXPERTS), D_MODEL ** -0.5),
        'w_exp_gate': nrm(ks[20], (DEPTH, N_EXPERTS, D_MODEL, D_EXPERT), D_MODEL ** -0.5),
        'w_exp_up': nrm(ks[21], (DEPTH, N_EXPERTS, D_MODEL, D_EXPERT), D_MODEL ** -0.5 * DEEPNORM_BETA),
        'w_exp_down': nrm(ks[22], (DEPTH, N_EXPERTS, D_EXPERT, D_MODEL), D_EXPERT ** -0.5 * DEEPNORM_BETA),
    }


def reference(x_prompt, x_sample, cache_fox_kv, cache_fox_logf, cache_nsa_kv, state_nsa_win, state_conv, page_table,
              w_in, fox_b_f, gmlp_w_s, gmlp_b_s, nsa_cmp_pe, nsa_cmp_w1, nsa_cmp_w2, conv_w, w_out, ln_g, ln_b,
              w_router, w_exp_gate, w_exp_up, w_exp_down):
    xp, xs = x_prompt, x_sample
    st_p = [[] for _ in range(5)]
    st_s = [[] for _ in range(6)]
    for l in range(DEPTH):
        mp, sp = prompt_mixers(xp, w_in[l], fox_b_f[l], gmlp_w_s[l], gmlp_b_s[l], nsa_cmp_pe[l], nsa_cmp_w1[l],
                               nsa_cmp_w2[l], conv_w[l])
        ms, ss = sample_mixers(xs, cache_fox_kv[l], cache_fox_logf[l], cache_nsa_kv[l], state_nsa_win[l], state_conv[l],
                               page_table, w_in[l], fox_b_f[l], gmlp_w_s[l], gmlp_b_s[l], nsa_cmp_pe[l], nsa_cmp_w1[l],
                               nsa_cmp_w2[l], conv_w[l])
        xp = post_block(xp, mp, w_out[l], ln_g[l], ln_b[l], w_router, w_exp_gate[l], w_exp_up[l], w_exp_down[l])
        xs = post_block(xs, ms, w_out[l], ln_g[l], ln_b[l], w_router, w_exp_gate[l], w_exp_up[l], w_exp_down[l])
        for lst, a in zip(st_p, sp):
            lst.append(a)
        for lst, a in zip(st_s, ss):
            lst.append(a)
    fox_kv_p, fox_lf_p, nsa_kv_p, nsa_win_p, conv_p = [jnp.stack(a, axis=0) for a in st_p]
    fox_kv_s, fox_lf_s, nsa_kv_s, nsa_win_s, conv_s, gmlp_v_s = [jnp.stack(a, axis=0) for a in st_s]
    return (xp, xs, fox_kv_p, fox_kv_s, fox_lf_p, fox_lf_s, nsa_kv_p, nsa_kv_s, nsa_win_p, nsa_win_s, conv_p, conv_s, gmlp_v_s)
```

```python
import functools

import numpy as np
import jax
import jax.numpy as jnp
from jax import lax
from jax.experimental import pallas as pl
from jax.experimental.pallas import tpu as pltpu

f32 = jnp.float32
bf16 = jnp.bfloat16

HEAD_DIM = 64
GROUP_W = 256
GROUP_HEADS = 4
CHUNK = 128
Q_BLOCK = 128
CMP_LEN = 32
CMP_STRIDE = 16
CMP_HID = 128
SEL_LEN = 64
SEL_TOPK = 16
WINDOW = 512
CONV_W = 3
N_EXPERTS = 16
N_GROUPS = 4
EXP_PER_GROUP = 4
TOP_K = 2
LN_EPS = 1e-5
SCALE = HEAD_DIM ** -0.5
NEG = -1e30
BIG = 1e30
LANES = 128

AU, AV, BQ, BK, BV, CQ, CKV, DH, DB, DC, MISC = 0, 256, 512, 768, 1024, 1280, 1536, 1920, 2176, 2432, 2688
P_PACK = 2816
N_FOX_F = GROUP_HEADS
N_GATES = 3 * GROUP_HEADS

VMEM_LIMIT = 56 * 1024 * 1024


def _cparams(sem):
    return pltpu.CompilerParams(dimension_semantics=sem, vmem_limit_bytes=VMEM_LIMIT)


def _dot(a, b):
    return jnp.dot(a, b, preferred_element_type=f32)


def _dot_nt(a, b):
    return lax.dot_general(a, b, (((1,), (1,)), ((), ())), preferred_element_type=f32)


def _split3(x):
    hi = x.astype(bf16)
    r = x - hi.astype(f32)
    mid = r.astype(bf16)
    lo = (r - mid.astype(f32)).astype(bf16)
    return hi, mid, lo


def _dot_exact_rhs(a, b3):
    return _dot(a, b3[0]) + _dot(a, b3[1]) + _dot(a, b3[2])


def _dot_exact_lhs(a3, b):
    return _dot(a3[0], b) + _dot(a3[1], b) + _dot(a3[2], b)


def _log_sigmoid(x):
    return jnp.minimum(x, 0.0) - jnp.log1p(jnp.exp(-jnp.abs(x)))


def _gelu(x):
    return jax.nn.gelu(x, approximate=True)


def _layer_norm_rows(x, g, b):
    mu = jnp.mean(x, axis=-1, keepdims=True)
    d = x - mu
    var = jnp.mean(d * d, axis=-1, keepdims=True)
    return d * lax.rsqrt(var + LN_EPS) * g + b


def _head_layer_norm(g, avg):
    h2 = g.astype(bf16)
    mu = _dot(h2, avg) + _dot((g - h2.astype(f32)).astype(bf16), avg)
    d = g - mu
    dd = d * d
    d2 = dd.astype(bf16)
    var = _dot(d2, avg) + _dot((dd - d2.astype(f32)).astype(bf16), avg)
    return d * lax.rsqrt(var + LN_EPS)


def _head_diag(z_all, rows):
    lane_head = lax.broadcasted_iota(jnp.int32, (rows, GROUP_W), 1) // HEAD_DIM
    out = jnp.where(lane_head == 0, z_all[0:rows], 0.0)
    for h in range(1, GROUP_HEADS):
        out = out + jnp.where(lane_head == h, z_all[h * rows:(h + 1) * rows], 0.0)
    return out


def _head_blockdiag_q(q, rows):
    lane_head = lax.broadcasted_iota(jnp.int32, (rows, GROUP_W), 1) // HEAD_DIM
    return jnp.concatenate([jnp.where(lane_head == h, q, jnp.zeros_like(q)) for h in range(GROUP_HEADS)], axis=0)


def _prompt_inproj_kernel(x_ref, w_ref, fb_ref, wst_ref, bs_ref, cw_ref, avg_ref, tri_ref,
                          ya_ref, fq_ref, fkv_ref, fkvb_ref, misc_ref, ccol_ref, crow_ref,
                          nq_ref, nkv_ref, nb_ref, win_ref, yd_ref, cst_ref,
                          carry_c, carry_z, *, tm):
    ti = pl.program_id(1)

    @pl.when(ti == 0)
    def _():
        carry_c[...] = jnp.zeros_like(carry_c)
        carry_z[...] = jnp.zeros_like(carry_z)

    xb = x_ref[0].astype(bf16)

    u = _gelu(_dot(xb, w_ref[:, AU:AU + GROUP_W]))
    vn = _head_layer_norm(_gelu(_dot(xb, w_ref[:, AV:AV + GROUP_W])), avg_ref[...])
    wst = wst_ref[...]
    bsf = bs_ref[...]
    for c in range(tm // CHUNK):
        r0 = c * CHUNK
        z_all = _dot(wst, vn[r0:r0 + CHUNK].astype(bf16))
        ya_ref[0, r0:r0 + CHUNK, :] = (u[r0:r0 + CHUNK] * (_head_diag(z_all, CHUNK) + bsf)).astype(bf16)

    fq_ref[0] = (_dot(xb, w_ref[:, BQ:BQ + GROUP_W]) * SCALE).astype(bf16)
    kv = _dot(xb, w_ref[:, BK:BK + 2 * GROUP_W])
    fkv_ref[0] = kv
    fkvb_ref[0] = kv.astype(bf16)
    pre = _dot(xb, w_ref[:, MISC:MISC + LANES]) + fb_ref[0:1, :]
    lane = lax.broadcasted_iota(jnp.int32, (tm, LANES), 1)
    lf = jnp.where(lane < N_FOX_F, _log_sigmoid(pre), 0.0)
    misc_ref[0] = jnp.where(lane < N_FOX_F, lf, jnp.where(lane < N_FOX_F + N_GATES, jax.nn.sigmoid(pre), 0.0))
    c = _dot_exact_rhs(tri_ref[...], _split3(lf)) + carry_c[0:1, :]
    carry_c[0:1, :] = c[tm - 1:tm, :]
    ccol_ref[0] = c
    crow_ref[0] = jnp.transpose(c)[0:8, :]

    nq_ref[0] = (_dot(xb, w_ref[:, CQ:CQ + GROUP_W]) * SCALE).astype(bf16)
    ckv = _dot(xb, w_ref[:, CKV:CKV + 6 * HEAD_DIM])
    nkv_ref[0] = ckv[:, 0:4 * HEAD_DIM]
    win_ref[0] = ckv[:, 4 * HEAD_DIM:6 * HEAD_DIM]
    nb_ref[0] = ckv.astype(bf16)

    z = _dot(xb, w_ref[:, DC:DC + GROUP_W]) * _dot(xb, w_ref[:, DH:DH + GROUP_W])
    row = lax.broadcasted_iota(jnp.int32, (tm, GROUP_W), 0)
    zp = carry_z[...]
    z1 = jnp.where(row == 0, zp[7:8, :], pltpu.roll(z, 1, 0))
    z2 = jnp.where(row == 0, zp[6:7, :], jnp.where(row == 1, zp[7:8, :], pltpu.roll(z, 2, 0)))
    cw = cw_ref[...]
    y_conv = cw[0:1, :] * z2 + cw[1:2, :] * z1 + cw[2:3, :] * z
    yd_ref[0] = (_dot(xb, w_ref[:, DB:DB + GROUP_W]) * y_conv).astype(bf16)
    carry_z[...] = z[tm - 8:tm, :]
    cst_ref[0] = z[tm - 8:tm, :]


def _prompt_inproj(x, w_pack, fb_row, wst, bs_full, cw8, avg, tri, tm):
    bsz, t, d = x.shape
    nt = t // tm
    row_spec = lambda w: pl.BlockSpec((1, tm, w), lambda b, i: (b, i, 0))
    const = lambda a: pl.BlockSpec(a.shape, lambda b, i: (0,) * a.ndim)
    outs = [
        (GROUP_W, bf16), (GROUP_W, bf16), (2 * GROUP_W, f32), (2 * GROUP_W, bf16), (LANES, f32), (LANES, f32),
    ]
    out_shape = [jax.ShapeDtypeStruct((bsz, t, w), dt) for w, dt in outs]
    out_specs = [row_spec(w) for w, _ in outs]
    out_shape.append(jax.ShapeDtypeStruct((bsz, 8, t), f32))
    out_specs.append(pl.BlockSpec((1, 8, tm), lambda b, i: (b, 0, i)))
    outs2 = [(GROUP_W, bf16), (4 * HEAD_DIM, f32), (6 * HEAD_DIM, bf16), (2 * HEAD_DIM, f32), (GROUP_W, bf16)]
    out_shape += [jax.ShapeDtypeStruct((bsz, t, w), dt) for w, dt in outs2]
    out_specs += [row_spec(w) for w, _ in outs2]
    out_shape.append(jax.ShapeDtypeStruct((bsz, 8, GROUP_W), f32))
    out_specs.append(pl.BlockSpec((1, 8, GROUP_W), lambda b, i: (b, 0, 0)))
    consts = [w_pack, fb_row, wst, bs_full, cw8, avg, tri]
    return pl.pallas_call(
        functools.partial(_prompt_inproj_kernel, tm=tm),
        grid=(bsz, nt),
        in_specs=[row_spec(d)] + [const(a) for a in consts],
        out_specs=out_specs,
        out_shape=out_shape,
        scratch_shapes=[pltpu.VMEM((8, LANES), f32), pltpu.VMEM((8, GROUP_W), f32)],
        compiler_params=_cparams(("arbitrary", "arbitrary")),
    )(x, *consts)


def _fox_prompt_kernel(q_ref, kv_ref, ccol_ref, crow_ref, o_ref, m_sc, l_sc, acc_sc, *, tk):
    qi = pl.program_id(1)
    rows = GROUP_HEADS * Q_BLOCK
    qbd = _head_blockdiag_q(q_ref[0], Q_BLOCK)
    cc = ccol_ref[0]
    cq = jnp.concatenate([cc[:, h:h + 1] for h in range(GROUP_HEADS)], axis=0)
    qpos = qi * Q_BLOCK + lax.broadcasted_iota(jnp.int32, (Q_BLOCK, 1), 0)
    qpos = jnp.concatenate([qpos] * GROUP_HEADS, axis=0)
    m_sc[...] = jnp.full_like(m_sc, NEG)
    l_sc[...] = jnp.zeros_like(l_sc)
    acc_sc[...] = jnp.zeros_like(acc_sc)
    n_kv = (qi * Q_BLOCK + Q_BLOCK + tk - 1) // tk

    def body(j, carry):
        k0 = pl.multiple_of(j * tk, tk)
        k = kv_ref[0, pl.ds(k0, tk), 0:GROUP_W]
        v = kv_ref[0, pl.ds(k0, tk), GROUP_W:2 * GROUP_W]
        cr = crow_ref[0, :, pl.ds(k0, tk)]
        ck = jnp.concatenate([jnp.broadcast_to(cr[h:h + 1, :], (Q_BLOCK, tk)) for h in range(GROUP_HEADS)], axis=0)
        s = _dot_nt(qbd, k) + cq - ck
        kpos = k0 + lax.broadcasted_iota(jnp.int32, (1, tk), 1)
        s = jnp.where(kpos <= qpos, s, NEG)
        m_old = m_sc[...]
        m_new = jnp.maximum(m_old, jnp.max(s, axis=-1, keepdims=True))
        alpha = jnp.exp(m_old - m_new)
        p = jnp.exp(s - m_new)
        l_sc[...] = alpha * l_sc[...] + jnp.sum(p, axis=-1, keepdims=True)
        acc_sc[...] = alpha * acc_sc[...] + _dot(p.astype(bf16), v)
        m_sc[...] = m_new
        return carry

    lax.fori_loop(0, n_kv, body, 0)
    o = acc_sc[...] / l_sc[...]
    o_ref[0] = _head_diag(o, Q_BLOCK).astype(bf16)


def _fox_prompt(fq, fkvb, ccol, crow, tk):
    bsz, t, _ = fq.shape
    rows = GROUP_HEADS * Q_BLOCK
    return pl.pallas_call(
        functools.partial(_fox_prompt_kernel, tk=tk),
        grid=(bsz, t // Q_BLOCK),
        in_specs=[
            pl.BlockSpec((1, Q_BLOCK, GROUP_W), lambda b, i: (b, i, 0)),
            pl.BlockSpec((1, t, 2 * GROUP_W), lambda b, i: (b, 0, 0)),
            pl.BlockSpec((1, Q_BLOCK, LANES), lambda b, i: (b, i, 0)),
            pl.BlockSpec((1, 8, t), lambda b, i: (b, 0, 0)),
        ],
        out_specs=pl.BlockSpec((1, Q_BLOCK, GROUP_W), lambda b, i: (b, i, 0)),
        out_shape=jax.ShapeDtypeStruct((bsz, t, GROUP_W), bf16),
        scratch_shapes=[pltpu.VMEM((rows, 1), f32), pltpu.VMEM((rows, 1), f32), pltpu.VMEM((rows, GROUP_W), f32)],
        compiler_params=_cparams(("arbitrary", "arbitrary")),
    )(fq, fkvb, ccol, crow)


def _compress_kernel(x_ref, pe_ref, w1_ref, w2_ref, o_ref):
    x = x_ref[0]
    n_grp = x.shape[0]
    top = _dot((x + pe_ref[0:1, :]).astype(bf16), w1_ref[:, 0:2 * CMP_HID])
    bot = _dot((x + pe_ref[1:2, :]).astype(bf16), w1_ref[:, 2 * CMP_HID:4 * CMP_HID])
    g = _gelu(top + pltpu.roll(bot, n_grp - 1, 0)).astype(bf16)
    w2 = w2_ref[...]
    kc = _dot(g[:, 0:CMP_HID], w2[:, 0:HEAD_DIM])
    vc = _dot(g[:, CMP_HID:2 * CMP_HID], w2[:, HEAD_DIM:2 * HEAD_DIM])
    o_ref[0] = jnp.concatenate([kc, vc], axis=-1).astype(bf16)


def _compress(x_groups, pe2, w1e, w2c):
    bsz, n_grp, w = x_groups.shape
    const = lambda a: pl.BlockSpec(a.shape, lambda b: (0,) * a.ndim)
    return pl.pallas_call(
        _compress_kernel,
        grid=(bsz,),
        in_specs=[pl.BlockSpec((1, n_grp, w), lambda b: (b, 0, 0)), const(pe2), const(w1e), const(w2c)],
        out_specs=pl.BlockSpec((1, n_grp, 2 * HEAD_DIM), lambda b: (b, 0, 0)),
        out_shape=jax.ShapeDtypeStruct((bsz, n_grp, 2 * HEAD_DIM), bf16),
        compiler_params=_cparams(("arbitrary",)),
    )(x_groups, pe2, w1e, w2c)


def _stack_heads(q, rows):
    return jnp.concatenate([q[:, h * HEAD_DIM:(h + 1) * HEAD_DIM] for h in range(GROUP_HEADS)], axis=0)


def _unstack_heads(o, rows):
    return jnp.concatenate([o[h * rows:(h + 1) * rows] for h in range(GROUP_HEADS)], axis=-1)


def _gate_col(misc, branch):
    base = N_FOX_F + branch * GROUP_HEADS
    return jnp.concatenate([misc[:, base + h:base + h + 1] for h in range(GROUP_HEADS)], axis=0)


def _nsa_compressed(qs, cmp, qpos_s, selw, nq):
    n_c = cmp.shape[0]
    kc = cmp[:, 0:HEAD_DIM]
    vc = cmp[:, HEAD_DIM:2 * HEAD_DIM]
    s = _dot_nt(qs, kc)
    cend = lax.broadcasted_iota(jnp.int32, (1, n_c), 1) * CMP_STRIDE + (CMP_LEN - 1)
    vis = cend <= qpos_s
    s = jnp.where(vis, s, NEG)
    e = jnp.exp(s - jnp.max(s, axis=-1, keepdims=True))
    p = e / jnp.sum(e, axis=-1, keepdims=True)
    p = p * jnp.where(qpos_s >= CMP_LEN - 1, 1.0, 0.0)
    pb = p.astype(bf16)
    o_c = _dot(pb, vc)
    imp = _dot(pb[0:nq], selw)
    for h in range(1, GROUP_HEADS):
        imp = imp + _dot(pb[h * nq:(h + 1) * nq], selw)
    return o_c, imp


def _nsa_select(imp, qpos_row, n_sel_pad):
    nq = imp.shape[0]
    if nq < LANES:
        imp = jnp.concatenate([imp, jnp.zeros((LANES - nq, LANES), f32)], axis=0)
    imp_t = jnp.transpose(imp)[0:n_sel_pad]
    blk = lax.broadcasted_iota(jnp.int32, (n_sel_pad, LANES), 0)
    cur = qpos_row // SEL_LEN
    valid = blk * SEL_LEN <= qpos_row
    forced = (blk == 0) | (blk == cur) | (blk == cur - 1)
    val = jnp.where(valid, jnp.where(forced, BIG, imp_t), -BIG)
    rank = jnp.zeros((n_sel_pad, LANES), f32)
    for mp in range(n_sel_pad):
        vb = val[mp:mp + 1, :]
        rank = rank + jnp.where(vb > val, 1.0, jnp.where(vb == val, jnp.where(blk > mp, 1.0, 0.0), 0.0))
    sel_t = jnp.where(valid, jnp.where(rank < SEL_TOPK, 1.0, 0.0), 0.0)
    if n_sel_pad < LANES:
        sel_t = jnp.concatenate([sel_t, jnp.zeros((LANES - n_sel_pad, LANES), f32)], axis=0)
    return jnp.transpose(sel_t)[0:nq]


def _nsa_prompt_kernel(q_ref, nb_ref, cmp_ref, misc_ref, selw_ref, exp_ref, o_ref, m_sc, l_sc, acc_sc, *, tk, t, n_sel_pad):
    qi = pl.program_id(1)
    nq = Q_BLOCK
    rows = GROUP_HEADS * nq
    qs = _stack_heads(q_ref[0], nq)
    qpos = qi * nq + lax.broadcasted_iota(jnp.int32, (nq, 1), 0)
    qpos_s = jnp.concatenate([qpos] * GROUP_HEADS, axis=0)
    qpos_row = qi * nq + lax.broadcasted_iota(jnp.int32, (1, nq), 1)

    o_c, imp = _nsa_compressed(qs, cmp_ref[0], qpos_s, selw_ref[...], nq)
    sel = _nsa_select(imp, qpos_row, n_sel_pad).astype(bf16)

    m_sc[...] = jnp.full_like(m_sc, NEG)
    l_sc[...] = jnp.zeros_like(l_sc)
    acc_sc[...] = jnp.zeros_like(acc_sc)
    n_kv = (qi * nq + nq + tk - 1) // tk

    def body(j, carry):
        k0 = pl.multiple_of(j * tk, tk)
        k = nb_ref[0, pl.ds(k0, tk), 2 * HEAD_DIM:3 * HEAD_DIM]
        v = nb_ref[0, pl.ds(k0, tk), 3 * HEAD_DIM:4 * HEAD_DIM]
        picked = _dot(sel, exp_ref[:, pl.ds(k0, tk)])
        kpos = k0 + lax.broadcasted_iota(jnp.int32, (1, tk), 1)
        keep = jnp.where(kpos <= qpos, picked, 0.0)
        keep = jnp.concatenate([keep] * GROUP_HEADS, axis=0)
        s = jnp.where(keep > 0.5, _dot_nt(qs, k), NEG)
        m_old = m_sc[...]
        m_new = jnp.maximum(m_old, jnp.max(s, axis=-1, keepdims=True))
        alpha = jnp.exp(m_old - m_new)
        p = jnp.exp(s - m_new)
        l_sc[...] = alpha * l_sc[...] + jnp.sum(p, axis=-1, keepdims=True)
        acc_sc[...] = alpha * acc_sc[...] + _dot(p.astype(bf16), v)
        m_sc[...] = m_new
        return carry

    lax.fori_loop(0, n_kv, body, 0)
    o_s = acc_sc[...] / l_sc[...]

    span = min(WINDOW + nq, t)
    w0 = pl.multiple_of(jnp.maximum(qi * nq - WINDOW, 0), nq)
    kw = nb_ref[0, pl.ds(w0, span), 4 * HEAD_DIM:5 * HEAD_DIM]
    vw = nb_ref[0, pl.ds(w0, span), 5 * HEAD_DIM:6 * HEAD_DIM]
    wpos = w0 + lax.broadcasted_iota(jnp.int32, (1, span), 1)
    s = jnp.where((wpos <= qpos_s) & (wpos > qpos_s - WINDOW), _dot_nt(qs, kw), NEG)
    e = jnp.exp(s - jnp.max(s, axis=-1, keepdims=True))
    p = e / jnp.sum(e, axis=-1, keepdims=True)
    o_w = _dot(p.astype(bf16), vw)

    misc = misc_ref[0]
    o = _gate_col(misc, 0) * o_c + _gate_col(misc, 1) * o_s + _gate_col(misc, 2) * o_w
    o_ref[0] = _unstack_heads(o, nq).astype(bf16)


def _nsa_prompt(nq_b, nb, cmp, misc, selw, expand, tk):
    bsz, t, _ = nq_b.shape
    rows = GROUP_HEADS * Q_BLOCK
    n_c = cmp.shape[1]
    n_sel_pad = -(-(t // SEL_LEN) // 8) * 8
    const = lambda a: pl.BlockSpec(a.shape, lambda b, i: (0,) * a.ndim)
    return pl.pallas_call(
        functools.partial(_nsa_prompt_kernel, tk=tk, t=t, n_sel_pad=n_sel_pad),
        grid=(bsz, t // Q_BLOCK),
        in_specs=[
            pl.BlockSpec((1, Q_BLOCK, GROUP_W), lambda b, i: (b, i, 0)),
            pl.BlockSpec((1, t, 6 * HEAD_DIM), lambda b, i: (b, 0, 0)),
            pl.BlockSpec((1, n_c, 2 * HEAD_DIM), lambda b, i: (b, 0, 0)),
            pl.BlockSpec((1, Q_BLOCK, LANES), lambda b, i: (b, i, 0)),
            const(selw), const(expand),
        ],
        out_specs=pl.BlockSpec((1, Q_BLOCK, GROUP_W), lambda b, i: (b, i, 0)),
        out_shape=jax.ShapeDtypeStruct((bsz, t, GROUP_W), bf16),
        scratch_shapes=[pltpu.VMEM((rows, 1), f32), pltpu.VMEM((rows, 1), f32), pltpu.VMEM((rows, HEAD_DIM), f32)],
        compiler_params=_cparams(("arbitrary", "arbitrary")),
    )(nq_b, nb, cmp, misc, selw, expand)


def _matmul_kernel(x_ref, w_ref, o_ref):
    o_ref[...] = _dot(x_ref[...].astype(bf16), w_ref[...])


def _sample_inproj(x2, w_pack, tm):
    n, d = x2.shape
    pw = w_pack.shape[1]
    return pl.pallas_call(
        _matmul_kernel,
        grid=(n // tm,),
        in_specs=[pl.BlockSpec((tm, d), lambda i: (i, 0)), pl.BlockSpec((d, pw), lambda i: (0, 0))],
        out_specs=pl.BlockSpec((tm, pw), lambda i: (i, 0)),
        out_shape=jax.ShapeDtypeStruct((n, pw), f32),
        compiler_params=_cparams(("arbitrary",)),
    )(x2, w_pack)


def _pad_to(a, rows):
    return jnp.concatenate([a, jnp.zeros((rows - a.shape[0], a.shape[1]), a.dtype)], axis=0)


def _softmax_tiles(tiles):
    m = tiles[0].max(axis=-1, keepdims=True)
    for s in tiles[1:]:
        m = jnp.maximum(m, s.max(axis=-1, keepdims=True))
    ps = [jnp.exp(s - m) for s in tiles]
    tot = ps[0].sum(axis=-1, keepdims=True)
    for p in ps[1:]:
        tot = tot + p.sum(axis=-1, keepdims=True)
    return ps, tot


def _sample_mixer_kernel(pt_ref, h_ref, *refs, n_pages, page, past, n_sel_pad):
    del pt_ref
    fox_pages = refs[0:n_pages]
    lf_pages = refs[n_pages:2 * n_pages]
    nsa_rows = refs[2 * n_pages:3 * n_pages]
    nsa_grps = refs[3 * n_pages:4 * n_pages]
    (win_ref, conv_ref, fb_ref, wst_ref, bs_ref, cw_ref, avg_ref, cs_ref, cst_ref, csm_ref, slt_ref, tri8_ref,
     pe_ref, w1_ref, w2_ref, selw_ref, exp_ref,
     y_ref, fkv_ref, misc_ref, nrow_ref, wout_ref, cout_ref, gv_ref) = refs[4 * n_pages:]
    tn = h_ref.shape[1]
    rows = GROUP_HEADS * tn
    h = h_ref[0]

    u = _gelu(h[:, AU:AU + GROUP_W])
    vn = _head_layer_norm(_gelu(h[:, AV:AV + GROUP_W]), avg_ref[...])
    gv_ref[0] = vn
    w8 = jnp.concatenate([wst_ref[hh * CHUNK:hh * CHUNK + tn, :] for hh in range(GROUP_HEADS)], axis=0)
    z_all = _dot(w8, _pad_to(vn, CHUNK).astype(bf16))
    y_a = u * (_head_diag(z_all, tn) + bs_ref[0:tn, :])

    z = h[:, DC:DC + GROUP_W] * h[:, DH:DH + GROUP_W]
    zp = conv_ref[0, 0]
    row = lax.broadcasted_iota(jnp.int32, (tn, GROUP_W), 0)
    z1 = jnp.where(row == 0, zp[1:2, :], pltpu.roll(z, 1, 0))
    z2 = jnp.where(row == 0, zp[0:1, :], jnp.where(row == 1, zp[1:2, :], pltpu.roll(z, 2, 0)))
    cw = cw_ref[...]
    y_d = h[:, DB:DB + GROUP_W] * (cw[0:1, :] * z2 + cw[1:2, :] * z1 + cw[2:3, :] * z)
    cout_ref[0] = z

    pre = h[:, MISC:MISC + LANES] + fb_ref[0:1, :]
    lane = lax.broadcasted_iota(jnp.int32, (tn, LANES), 1)
    lf_new = jnp.where(lane < N_FOX_F, _log_sigmoid(pre), 0.0)
    misc = jnp.where(lane < N_FOX_F, lf_new, jnp.where(lane < N_FOX_F + N_GATES, jax.nn.sigmoid(pre), 0.0))
    misc_ref[0] = misc

    lf3 = _split3(jnp.concatenate([r[0, 0] for r in lf_pages], axis=0))
    c_loc = _dot_exact_lhs(lf3, cs_ref[...])
    tot = _dot_exact_lhs(lf3, cst_ref[...])
    off = _dot_exact_rhs(slt_ref[...], _split3(_pad_to(tot, LANES)))
    c_past = c_loc + off
    tot_small = _dot_exact_lhs(lf3, csm_ref[...])
    c_new = jnp.sum(tot_small, axis=0, keepdims=True) + _dot_exact_rhs(tri8_ref[...], _split3(_pad_to(lf_new, LANES)))
    cq = jnp.concatenate([c_new[:, hh:hh + 1] for hh in range(GROUP_HEADS)], axis=0)
    c_new_t = jnp.transpose(_pad_to(c_new, LANES))
    qpos = past + lax.broadcasted_iota(jnp.int32, (tn, 1), 0)
    qpos_s = jnp.concatenate([qpos] * GROUP_HEADS, axis=0)
    new_pos = past + lax.broadcasted_iota(jnp.int32, (1, LANES), 1)
    new_ok = new_pos <= qpos_s

    qbd = _head_blockdiag_q((h[:, BQ:BQ + GROUP_W] * SCALE).astype(bf16), tn)
    kv_new = h[:, BK:BK + 2 * GROUP_W]
    fkv_ref[0] = kv_new
    kvb_new = _pad_to(kv_new, LANES).astype(bf16)
    tiles = []
    for pg in range(n_pages):
        kb = fox_pages[pg][0, 0, :, 0:GROUP_W].astype(bf16)
        ck = jnp.concatenate([jnp.broadcast_to(c_past[pg:pg + 1, hh * page:(hh + 1) * page], (tn, page))
                              for hh in range(GROUP_HEADS)], axis=0)
        tiles.append(_dot_nt(qbd, kb) + cq - ck)
    ck = jnp.concatenate([jnp.broadcast_to(c_new_t[hh:hh + 1, :], (tn, LANES)) for hh in range(GROUP_HEADS)], axis=0)
    tiles.append(jnp.where(new_ok, _dot_nt(qbd, kvb_new[:, 0:GROUP_W]) + cq - ck, NEG))
    ps, tot_p = _softmax_tiles(tiles)
    acc = _dot(ps[n_pages].astype(bf16), kvb_new[:, GROUP_W:2 * GROUP_W])
    for pg in range(n_pages):
        acc = acc + _dot(ps[pg].astype(bf16), fox_pages[pg][0, 0, :, GROUP_W:2 * GROUP_W].astype(bf16))
    y_b = _head_diag(acc / tot_p, tn)

    ckv = h[:, CKV:CKV + 6 * HEAD_DIM]
    nrow_ref[0] = ckv[:, 0:4 * HEAD_DIM]
    xg = jnp.concatenate([r[0, 0] for r in nsa_grps], axis=0)
    n_grp = xg.shape[0]
    top = _dot((xg + pe_ref[0:1, :]).astype(bf16), w1_ref[:, 0:2 * CMP_HID])
    bot = _dot((xg + pe_ref[1:2, :]).astype(bf16), w1_ref[:, 2 * CMP_HID:4 * CMP_HID])
    g = _gelu(top + pltpu.roll(bot, n_grp - 1, 0)).astype(bf16)
    w2 = w2_ref[...]
    cmp = jnp.concatenate([_dot(g[:, 0:CMP_HID], w2[:, 0:HEAD_DIM]),
                           _dot(g[:, CMP_HID:2 * CMP_HID], w2[:, HEAD_DIM:2 * HEAD_DIM])], axis=-1).astype(bf16)
    qs = _stack_heads((h[:, CQ:CQ + GROUP_W] * SCALE).astype(bf16), tn)
    o_c, imp = _nsa_compressed(qs, cmp, qpos_s, selw_ref[...], tn)
    sel = _nsa_select(imp, new_pos, n_sel_pad).astype(bf16)
    sel_new = _pad_to(ckv[:, 2 * HEAD_DIM:4 * HEAD_DIM], LANES).astype(bf16)
    tiles = []
    for pg in range(n_pages + 1):
        picked = _dot(sel, exp_ref[:, pg * page:(pg + 1) * page])
        if pg == n_pages:
            picked = jnp.where(new_pos <= qpos, picked, 0.0)
            kb = sel_new[:, 0:HEAD_DIM]
        else:
            kb = nsa_rows[pg][0, 0, :, 2 * HEAD_DIM:3 * HEAD_DIM].astype(bf16)
        keep = jnp.concatenate([picked] * GROUP_HEADS, axis=0) > 0.5
        tiles.append(jnp.where(keep, _dot_nt(qs, kb), NEG))
    ps, tot_p = _softmax_tiles(tiles)
    acc = _dot(ps[n_pages].astype(bf16), sel_new[:, HEAD_DIM:2 * HEAD_DIM])
    for pg in range(n_pages):
        acc = acc + _dot(ps[pg].astype(bf16), nsa_rows[pg][0, 0, :, 3 * HEAD_DIM:4 * HEAD_DIM].astype(bf16))
    o_s = acc / tot_p

    win_old = win_ref[0, 0]
    win_new = ckv[:, 4 * HEAD_DIM:6 * HEAD_DIM]
    wout_ref[0, 0:WINDOW - tn, :] = win_old[tn:WINDOW, :]
    wout_ref[0, WINDOW - tn:WINDOW, :] = win_new
    wob = win_old.astype(bf16)
    wnb = _pad_to(win_new, LANES).astype(bf16)
    w_idx = lax.broadcasted_iota(jnp.int32, (1, WINDOW), 1)
    s_old = jnp.where(w_idx > qpos_s - past, _dot_nt(qs, wob[:, 0:HEAD_DIM]), NEG)
    s_new = jnp.where(new_ok, _dot_nt(qs, wnb[:, 0:HEAD_DIM]), NEG)
    ps, tot_p = _softmax_tiles([s_old, s_new])
    o_w = (_dot(ps[0].astype(bf16), wob[:, HEAD_DIM:2 * HEAD_DIM])
           + _dot(ps[1].astype(bf16), wnb[:, HEAD_DIM:2 * HEAD_DIM])) / tot_p

    o = _gate_col(misc, 0) * o_c + _gate_col(misc, 1) * o_s + _gate_col(misc, 2) * o_w
    y_c = _unstack_heads(o, tn)
    y_ref[0] = jnp.concatenate([y_a, y_b, y_c, y_d], axis=-1)


def _sample_mixer(h3, fox5, lf5, nsa_rows5, nsa_grps5, win4, conv4, page_table, layer, consts):
    bsz, tn, pw = h3.shape
    n_pages = page_table.shape[1]
    page = fox5.shape[2]
    past = n_pages * page
    n_sel_pad = -(-(-(-(past + tn) // SEL_LEN)) // 8) * 8
    grp_per_page = nsa_grps5.shape[2]

    def page_spec(arr, pg):
        blk = (1, 1) + arr.shape[2:]
        return pl.BlockSpec(blk, lambda b, pt: (layer, pt[b, pg], 0, 0))

    in_specs = [pl.BlockSpec((1, tn, pw), lambda b, pt: (b, 0, 0))]
    operands = [h3]
    for arr in (fox5, lf5, nsa_rows5, nsa_grps5):
        for pg in range(n_pages):
            in_specs.append(page_spec(arr, pg))
            operands.append(arr)
    in_specs.append(pl.BlockSpec((1, 1) + win4.shape[2:], lambda b, pt: (layer, b, 0, 0)))
    operands.append(win4)
    in_specs.append(pl.BlockSpec((1, 1) + conv4.shape[2:], lambda b, pt: (layer, b, 0, 0)))
    operands.append(conv4)
    for a in consts:
        in_specs.append(pl.BlockSpec(a.shape, lambda b, pt, nd=a.ndim: (0,) * nd))
        operands.append(a)
    out_w = [(4 * GROUP_W, tn), (2 * GROUP_W, tn), (LANES, tn), (4 * HEAD_DIM, tn), (2 * HEAD_DIM, WINDOW),
             (GROUP_W, tn), (GROUP_W, tn)]
    out_shape = [jax.ShapeDtypeStruct((bsz, r, w), f32) for w, r in out_w]
    out_specs = [pl.BlockSpec((1, r, w), lambda b, pt: (b, 0, 0)) for w, r in out_w]
    return pl.pallas_call(
        functools.partial(_sample_mixer_kernel, n_pages=n_pages, page=page, past=past, n_sel_pad=n_sel_pad),
        grid_spec=pltpu.PrefetchScalarGridSpec(num_scalar_prefetch=1, grid=(bsz,), in_specs=in_specs,
                                               out_specs=out_specs),
        out_shape=out_shape,
        compiler_params=_cparams(("arbitrary",)),
    )(page_table, *operands)


def _outproj_kernel(x_ref, *refs, alpha):
    y_refs = refs[:-4]
    w_ref, g_ref, b_ref, o_ref = refs[-4:]
    acc = None
    c0 = 0
    for y_ref in y_refs:
        wy = y_ref.shape[1]
        part = _dot(y_ref[...].astype(bf16), w_ref[c0:c0 + wy, :])
        acc = part if acc is None else acc + part
        c0 += wy
    o_ref[...] = _layer_norm_rows(alpha * x_ref[...] + acc, g_ref[0:1, :], b_ref[0:1, :])


def _outproj(x2, ys, w_out_b, g8, b8, alpha, tm):
    n, d = x2.shape
    assert sum(y.shape[1] for y in ys) == d
    const = lambda a: pl.BlockSpec(a.shape, lambda i: (0,) * a.ndim)
    return pl.pallas_call(
        functools.partial(_outproj_kernel, alpha=alpha),
        grid=(n // tm,),
        in_specs=[pl.BlockSpec((tm, d), lambda i: (i, 0))] + [pl.BlockSpec((tm, y.shape[1]), lambda i: (i, 0)) for y in ys]
        + [const(w_out_b), const(g8), const(b8)],
        out_specs=pl.BlockSpec((tm, d), lambda i: (i, 0)),
        out_shape=jax.ShapeDtypeStruct((n, d), f32),
        compiler_params=_cparams(("arbitrary",)),
    )(x2, *ys, w_out_b, g8, b8)


def _route(xb, wr_t, tm):
    logits = _dot_nt(wr_t, xb)
    e = jnp.exp(logits - jnp.max(logits, axis=0, keepdims=True))
    probs = e / jnp.sum(e, axis=0, keepdims=True)
    p = [probs[i:i + 1, :] for i in range(N_EXPERTS)]
    one = jnp.ones((1, tm), f32)
    zero = jnp.zeros((1, tm), f32)
    in_top = []
    gscore = []
    for g in range(N_GROUPS):
        members = range(g * EXP_PER_GROUP, (g + 1) * EXP_PER_GROUP)
        score = zero
        for i in members:
            rank = zero
            for j in members:
                if j < i:
                    rank = rank + jnp.where(p[j] >= p[i], one, zero)
                elif j > i:
                    rank = rank + jnp.where(p[j] > p[i], one, zero)
            keep = jnp.where(rank < TOP_K, one, zero)
            in_top.append(keep)
            score = score + keep * p[i]
        gscore.append(score)
    gate_rows = []
    wsum = zero
    chosen = []
    for g in range(N_GROUPS):
        win = one
        for g2 in range(N_GROUPS):
            if g2 < g:
                win = win * jnp.where(gscore[g] > gscore[g2], one, zero)
            elif g2 > g:
                win = win * jnp.where(gscore[g] >= gscore[g2], one, zero)
        for i in range(g * EXP_PER_GROUP, (g + 1) * EXP_PER_GROUP):
            c = in_top[i] * win
            chosen.append(c)
            wsum = wsum + c * p[i]
    for i in range(N_EXPERTS):
        gate_rows.append(jnp.where(chosen[i] > 0.5, p[i] / wsum, zero))
    gate_t = jnp.concatenate(gate_rows + [jnp.zeros((LANES - N_EXPERTS, tm), f32)], axis=0)
    return jnp.concatenate([jnp.transpose(gate_t[:, c * LANES:(c + 1) * LANES]) for c in range(tm // LANES)], axis=0)


def _moe_kernel(x_ref, wr_ref, wg_ref, wu_ref, wd_ref, g_ref, b_ref, o_ref, gate_sc, acc_sc, xb_sc, *, alpha, tm):
    e = pl.program_id(1)

    @pl.when(e == 0)
    def _():
        xb = x_ref[...].astype(bf16)
        xb_sc[...] = xb
        gate_sc[...] = _route(xb, wr_ref[...], tm)
        acc_sc[...] = jnp.zeros_like(acc_sc)

    xb = xb_sc[...]
    lane = lax.broadcasted_iota(jnp.int32, (tm, LANES), 1)
    gcol = jnp.sum(jnp.where(lane == e, gate_sc[...], 0.0), axis=-1, keepdims=True)
    h = jax.nn.silu(_dot(xb, wg_ref[0])) * _dot(xb, wu_ref[0])
    acc_sc[...] += _dot((h * gcol).astype(bf16), wd_ref[0])

    @pl.when(e == N_EXPERTS - 1)
    def _():
        o_ref[...] = _layer_norm_rows(alpha * x_ref[...] + acc_sc[...], g_ref[0:1, :], b_ref[0:1, :])


def _moe(x2, wr_t, wg, wu, wd, g8, b8, alpha, tm):
    n, d = x2.shape
    de = wg.shape[-1]
    const = lambda a: pl.BlockSpec(a.shape, lambda i, e: (0,) * a.ndim)
    return pl.pallas_call(
        functools.partial(_moe_kernel, alpha=alpha, tm=tm),
        grid=(n // tm, N_EXPERTS),
        in_specs=[
            pl.BlockSpec((tm, d), lambda i, e: (i, 0)), const(wr_t),
            pl.BlockSpec((1, d, de), lambda i, e: (e, 0, 0)), pl.BlockSpec((1, d, de), lambda i, e: (e, 0, 0)),
            pl.BlockSpec((1, de, d), lambda i, e: (e, 0, 0)), const(g8), const(b8),
        ],
        out_specs=pl.BlockSpec((tm, d), lambda i, e: (i, 0)),
        out_shape=jax.ShapeDtypeStruct((n, d), f32),
        scratch_shapes=[pltpu.VMEM((tm, LANES), f32), pltpu.VMEM((tm, d), f32), pltpu.VMEM((tm, d), bf16)],
        compiler_params=_cparams(("arbitrary", "arbitrary")),
    )(x2, wr_t, wg, wu, wd, g8, b8)


def _pad_rows(a, rows=8):
    return jnp.concatenate([a, jnp.zeros((rows - a.shape[0],) + a.shape[1:], a.dtype)], axis=0)


def _prep_layer(l, w_in, fox_b_f, gmlp_w_s, gmlp_b_s, nsa_cmp_pe, nsa_cmp_w1, nsa_cmp_w2, conv_w, w_out, ln_g, ln_b,
                w_exp_gate, w_exp_up, w_exp_down):
    w = w_in[l]
    d = w.shape[0]
    w_pack = jnp.concatenate([w[:, 0:1280], w[:, 1284:1924], w[:, 1936:2704], w[:, 1280:1284], w[:, 1924:1936],
                              jnp.zeros((d, LANES - N_FOX_F - N_GATES), w.dtype)], axis=1).astype(bf16)
    fb_row = _pad_rows(jnp.concatenate([fox_b_f[l], jnp.zeros((LANES - N_FOX_F,), f32)])[None, :])
    tril = jnp.asarray(np.tril(np.ones((CHUNK, CHUNK), np.float32)))
    wst = (gmlp_w_s[l] * tril).reshape(GROUP_HEADS * CHUNK, CHUNK).astype(bf16)
    bs_full = jnp.repeat(gmlp_b_s[l].T, HEAD_DIM, axis=1)
    cw8 = _pad_rows(conv_w[l])
    half = CMP_LEN // 2
    w1r = nsa_cmp_w1[l].reshape(2, CMP_LEN, HEAD_DIM, CMP_HID)
    per = nsa_cmp_pe[l]

    def w1_expand(kv, hf):
        m = jnp.zeros((half, 4 * HEAD_DIM, CMP_HID), f32)
        m = m.at[:, kv * HEAD_DIM:(kv + 1) * HEAD_DIM, :].set(w1r[kv, hf * half:(hf + 1) * half])
        return m.reshape(half * 4 * HEAD_DIM, CMP_HID)

    def pe_expand(hf):
        m = jnp.zeros((half, 4 * HEAD_DIM), f32)
        for kv in range(2):
            m = m.at[:, kv * HEAD_DIM:(kv + 1) * HEAD_DIM].set(per[kv, hf * half:(hf + 1) * half])
        return m.reshape(1, half * 4 * HEAD_DIM)

    w1e = jnp.concatenate([w1_expand(0, 0), w1_expand(1, 0), w1_expand(0, 1), w1_expand(1, 1)], axis=1).astype(bf16)
    pe2 = _pad_rows(jnp.concatenate([pe_expand(0), pe_expand(1)], axis=0))
    w2c = jnp.concatenate([nsa_cmp_w2[l, 0], nsa_cmp_w2[l, 1]], axis=1).astype(bf16)
    return dict(
        w_pack=w_pack, fb_row=fb_row, wst=wst, bs_full=bs_full, cw8=cw8, w1e=w1e, pe2=pe2, w2c=w2c,
        w_out=w_out[l].astype(bf16), g1=_pad_rows(ln_g[l, 0:1]), b1=_pad_rows(ln_b[l, 0:1]),
        g2=_pad_rows(ln_g[l, 1:2]), b2=_pad_rows(ln_b[l, 1:2]),
        wg=w_exp_gate[l].astype(bf16), wu=w_exp_up[l].astype(bf16), wd=w_exp_down[l].astype(bf16),
    )


def _sel_overlap(t, n_sel):
    n_cmp = (t - CMP_LEN) // CMP_STRIDE + 1
    start = np.arange(n_cmp) * CMP_STRIDE
    sstart = np.arange(n_sel) * SEL_LEN
    ov = np.minimum(start[:, None] + CMP_LEN, sstart[None, :] + SEL_LEN) - np.maximum(start[:, None], sstart[None, :])
    return np.clip(ov, 0, None) / CMP_STRIDE


def _head_avg():
    a = np.kron(np.eye(GROUP_HEADS, dtype=np.float32), np.full((HEAD_DIM, HEAD_DIM), 1.0 / HEAD_DIM, np.float32))
    return jnp.asarray(a, bf16)


def _post_block(x2, ys, p, wr_t, alpha, tm):
    x1 = _outproj(x2, ys, p['w_out'], p['g1'], p['b1'], alpha, tm)
    return _moe(x1, wr_t, p['wg'], p['wu'], p['wd'], p['g2'], p['b2'], alpha, tm)


def _prompt_layer(x, p, wr_t, alpha, tm=512, tk=512):
    bsz, t, d = x.shape
    n = bsz * t
    n_sel = t // SEL_LEN
    n_grp = t // CMP_STRIDE
    assert t % tm == 0 and t % tk == 0 and n_sel <= LANES and t >= WINDOW + Q_BLOCK
    tri = jnp.asarray(np.tril(np.ones((tm, tm), np.float32)), bf16)
    (ya, fq, fkv, fkvb, misc, ccol, crow, nq, nkv, nb, win, yd, cst) = _prompt_inproj(
        x, p['w_pack'], p['fb_row'], p['wst'], p['bs_full'], p['cw8'], _head_avg(), tri, tm)
    yb = _fox_prompt(fq, fkvb, ccol, crow, tk)
    cmp = _compress(nkv.reshape(bsz, n_grp, CMP_STRIDE * 4 * HEAD_DIM), p['pe2'], p['w1e'], p['w2c'])
    ov = _sel_overlap(t, n_sel)
    selw = np.zeros((n_grp, LANES), np.float32)
    selw[:ov.shape[0], :n_sel] = ov
    expand = np.zeros((LANES, t), np.float32)
    expand[np.arange(t) // SEL_LEN, np.arange(t)] = 1.0
    yc = _nsa_prompt(nq, nb, cmp, misc, jnp.asarray(selw, bf16), jnp.asarray(expand, bf16), tk)
    ys = [a.reshape(n, GROUP_W) for a in (ya, yb, yc, yd)]
    x_out = _post_block(x.reshape(n, d), ys, p, wr_t, alpha, tm).reshape(bsz, t, d)
    win_keep = min(WINDOW, t)
    states = (fkv.reshape(bsz, t, 2, GROUP_HEADS, HEAD_DIM), misc[:, :, 0:N_FOX_F],
              nkv.reshape(bsz, t, 4, HEAD_DIM), win[:, t - win_keep:].reshape(bsz, win_keep, 2, HEAD_DIM),
              cst[:, 8 - (CONV_W - 1):])
    return x_out, states, (ya, yb, yc, yd)


def _sample_consts(p, n_pages, page, tn):
    flat = page * GROUP_HEADS
    i = np.arange(flat)
    col = np.arange(flat)
    same_head = (i[:, None] % GROUP_HEADS) == (col[None, :] // page)
    cs = same_head & ((i[:, None] // GROUP_HEADS) <= (col[None, :] % page))
    csm = np.zeros((flat, LANES), np.float32)
    csm[i, i % GROUP_HEADS] = 1.0
    slt = np.zeros((n_pages, LANES), np.float32)
    slt[:, :n_pages] = np.tril(np.ones((n_pages, n_pages), np.float32), -1)
    tri8 = np.zeros((tn, LANES), np.float32)
    tri8[:, :tn] = np.tril(np.ones((tn, tn), np.float32))
    t_all = n_pages * page + tn
    n_sel = -(-t_all // SEL_LEN)
    ov = _sel_overlap(t_all, n_sel)
    n_grp = n_pages * page // CMP_STRIDE
    assert ov.shape[0] <= n_grp and n_sel <= LANES
    selw = np.zeros((n_grp, LANES), np.float32)
    selw[:ov.shape[0], :n_sel] = ov
    n_keys = (n_pages + 1) * page
    expand = np.zeros((LANES, n_keys), np.float32)
    expand[np.arange(n_keys) // SEL_LEN, np.arange(n_keys)] = 1.0
    as_b = lambda a: jnp.asarray(np.asarray(a, np.float32), bf16)
    return [p['fb_row'], p['wst'], p['bs_full'], p['cw8'], _head_avg(), as_b(cs), as_b(same_head), as_b(csm), as_b(slt),
            as_b(tri8), p['pe2'], p['w1e'], p['w2c'], as_b(selw), as_b(expand)]


def _sample_layer(x, views, page_table, layer, p, wr_t, alpha, tm=512):
    bsz, tn, d = x.shape
    n = bsz * tn
    fox5, lf5, nsa_rows5, nsa_grps5, win4, conv4 = views
    n_pages = page_table.shape[1]
    page = fox5.shape[2]
    assert tn == 8 and page == LANES and win4.shape[2] == WINDOW and n_pages * page >= WINDOW
    h = _sample_inproj(x.reshape(n, d), p['w_pack'], tm).reshape(bsz, tn, P_PACK)
    y, fkv, misc, nrow, wout, cout, gv = _sample_mixer(h, fox5, lf5, nsa_rows5, nsa_grps5, win4, conv4, page_table,
                                                       layer, _sample_consts(p, n_pages, page, tn))
    x_out = _post_block(x.reshape(n, d), [y.reshape(n, d)], p, wr_t, alpha, tm).reshape(bsz, tn, d)
    states = (fkv.reshape(bsz, tn, 2, GROUP_HEADS, HEAD_DIM), misc[:, :, 0:N_FOX_F], nrow.reshape(bsz, tn, 4, HEAD_DIM),
              wout.reshape(bsz, WINDOW, 2, HEAD_DIM), cout[:, tn - (CONV_W - 1):], gv.reshape(bsz, tn, GROUP_HEADS, HEAD_DIM))
    return x_out, states


def kernel(x_prompt, x_sample, cache_fox_kv, cache_fox_logf, cache_nsa_kv, state_nsa_win, state_conv, page_table,
           w_in, fox_b_f, gmlp_w_s, gmlp_b_s, nsa_cmp_pe, nsa_cmp_w1, nsa_cmp_w2, conv_w, w_out, ln_g, ln_b,
           w_router, w_exp_gate, w_exp_up, w_exp_down):
    depth, n_pool, page = cache_fox_kv.shape[0:3]
    dec_b = x_sample.shape[0]
    alpha = (2 * depth) ** 0.25
    wr_t = w_router.T.astype(bf16)
    views = (cache_fox_kv.reshape(depth, n_pool, page, 2 * GROUP_W),
             cache_fox_logf.reshape(depth, n_pool, 1, page * GROUP_HEADS),
             cache_nsa_kv.reshape(depth, n_pool, page, 4 * HEAD_DIM),
             cache_nsa_kv.reshape(depth, n_pool, page // CMP_STRIDE, CMP_STRIDE * 4 * HEAD_DIM),
             state_nsa_win.reshape(depth, dec_b, state_nsa_win.shape[2], 2 * HEAD_DIM),
             state_conv)
    xp, xs = x_prompt, x_sample
    st_p = [[] for _ in range(5)]
    st_s = [[] for _ in range(6)]
    for l in range(depth):
        p = _prep_layer(l, w_in, fox_b_f, gmlp_w_s, gmlp_b_s, nsa_cmp_pe, nsa_cmp_w1, nsa_cmp_w2, conv_w, w_out,
                        ln_g, ln_b, w_exp_gate, w_exp_up, w_exp_down)
        xp, sp, _ = _prompt_layer(xp, p, wr_t, alpha)
        xs, ss = _sample_layer(xs, views, page_table, l, p, wr_t, alpha)
        for lst, a in zip(st_p, sp):
            lst.append(a)
        for lst, a in zip(st_s, ss):
            lst.append(a)
    fox_kv_p, fox_lf_p, nsa_kv_p, nsa_win_p, conv_p = [jnp.stack(a, axis=0) for a in st_p]
    fox_kv_s, fox_lf_s, nsa_kv_s, nsa_win_s, conv_s, gmlp_v_s = [jnp.stack(a, axis=0) for a in st_s]
    return (xp, xs, fox_kv_p, fox_kv_s, fox_lf_p, fox_lf_s, nsa_kv_p, nsa_kv_s, nsa_win_p, nsa_win_s, conv_p, conv_s,
            gmlp_v_s)
```

```python
import functools

import numpy as np
import jax
import jax.numpy as jnp
from jax import lax
from jax.experimental import pallas as pl
from jax.experimental.pallas import tpu as pltpu

f32 = jnp.float32
bf16 = jnp.bfloat16

HEAD_DIM = 64
GROUP_W = 256
GROUP_HEADS = 4
CHUNK = 128
Q_BLOCK = 128
CMP_LEN = 32
CMP_STRIDE = 16
CMP_HID = 128
SEL_LEN = 64
SEL_TOPK = 16
WINDOW = 512
CONV_W = 3
N_EXPERTS = 16
N_GROUPS = 4
EXP_PER_GROUP = 4
TOP_K = 2
LN_EPS = 1e-5
SCALE = HEAD_DIM ** -0.5
NEG = -1e30
BIG = 1e30
LANES = 128

AU, AV, BQ, BK, BV, CQ, CKV, DH, DB, DC, MISC = 0, 256, 512, 768, 1024, 1280, 1536, 1920, 2176, 2432, 2688
P_PACK = 2816
N_FOX_F = GROUP_HEADS
N_GATES = 3 * GROUP_HEADS

VMEM_LIMIT = 56 * 1024 * 1024


def _cparams(sem):
    return pltpu.CompilerParams(dimension_semantics=sem, vmem_limit_bytes=VMEM_LIMIT)


def _dot(a, b):
    return jnp.dot(a, b, preferred_element_type=f32)


def _dot_nt(a, b):
    return lax.dot_general(a, b, (((1,), (1,)), ((), ())), preferred_element_type=f32)


def _split3(x):
    hi = x.astype(bf16)
    r = x - hi.astype(f32)
    mid = r.astype(bf16)
    lo = (r - mid.astype(f32)).astype(bf16)
    return hi, mid, lo


def _dot_exact_rhs(a, b3):
    return _dot(a, b3[0]) + _dot(a, b3[1]) + _dot(a, b3[2])


def _dot_exact_lhs(a3, b):
    return _dot(a3[0], b) + _dot(a3[1], b) + _dot(a3[2], b)


def _log_sigmoid(x):
    return jnp.minimum(x, 0.0) - jnp.log1p(jnp.exp(-jnp.abs(x)))


def _gelu(x):
    return jax.nn.gelu(x, approximate=True)


def _layer_norm_rows(x, g, b):
    mu = jnp.mean(x, axis=-1, keepdims=True)
    d = x - mu
    var = jnp.mean(d * d, axis=-1, keepdims=True)
    return d * lax.rsqrt(var + LN_EPS) * g + b


def _head_layer_norm(g, avg):
    h2 = g.astype(bf16)
    mu = _dot(h2, avg) + _dot((g - h2.astype(f32)).astype(bf16), avg)
    d = g - mu
    dd = d * d
    d2 = dd.astype(bf16)
    var = _dot(d2, avg) + _dot((dd - d2.astype(f32)).astype(bf16), avg)
    return d * lax.rsqrt(var + LN_EPS)


def _head_diag(z_all, rows):
    lane_head = lax.broadcasted_iota(jnp.int32, (rows, GROUP_W), 1) // HEAD_DIM
    out = jnp.where(lane_head == 0, z_all[0:rows], 0.0)
    for h in range(1, GROUP_HEADS):
        out = out + jnp.where(lane_head == h, z_all[h * rows:(h + 1) * rows], 0.0)
    return out


def _head_blockdiag_q(q, rows):
    lane_head = lax.broadcasted_iota(jnp.int32, (rows, GROUP_W), 1) // HEAD_DIM
    return jnp.concatenate([jnp.where(lane_head == h, q, jnp.zeros_like(q)) for h in range(GROUP_HEADS)], axis=0)


def _prompt_inproj_kernel(x_ref, w_ref, fb_ref, wst_ref, bs_ref, cw_ref, avg_ref, tri_ref,
                          ya_ref, fq_ref, fkv_ref, fkvb_ref, misc_ref, ccol_ref, crow_ref,
                          nq_ref, nkv_ref, nb_ref, win_ref, yd_ref, cst_ref,
                          carry_c, carry_z, *, tm):
    ti = pl.program_id(1)

    @pl.when(ti == 0)
    def _():
        carry_c[...] = jnp.zeros_like(carry_c)
        carry_z[...] = jnp.zeros_like(carry_z)

    xb = x_ref[0].astype(bf16)

    u = _gelu(_dot(xb, w_ref[:, AU:AU + GROUP_W]))
    vn = _head_layer_norm(_gelu(_dot(xb, w_ref[:, AV:AV + GROUP_W])), avg_ref[...])
    wst = wst_ref[...]
    bsf = bs_ref[...]
    for c in range(tm // CHUNK):
        r0 = c * CHUNK
        z_all = _dot(wst, vn[r0:r0 + CHUNK].astype(bf16))
        ya_ref[0, r0:r0 + CHUNK, :] = (u[r0:r0 + CHUNK] * (_head_diag(z_all, CHUNK) + bsf)).astype(bf16)

    fq_ref[0] = (_dot(xb, w_ref[:, BQ:BQ + GROUP_W]) * SCALE).astype(bf16)
    kv = _dot(xb, w_ref[:, BK:BK + 2 * GROUP_W])
    fkv_ref[0] = kv
    fkvb_ref[0] = kv.astype(bf16)
    pre = _dot(xb, w_ref[:, MISC:MISC + LANES]) + fb_ref[0:1, :]
    lane = lax.broadcasted_iota(jnp.int32, (tm, LANES), 1)
    lf = jnp.where(lane < N_FOX_F, _log_sigmoid(pre), 0.0)
    misc_ref[0] = jnp.where(lane < N_FOX_F, lf, jnp.where(lane < N_FOX_F + N_GATES, jax.nn.sigmoid(pre), 0.0))
    c = _dot_exact_rhs(tri_ref[...], _split3(lf)) + carry_c[0:1, :]
    carry_c[0:1, :] = c[tm - 1:tm, :]
    ccol_ref[0] = c
    crow_ref[0] = jnp.transpose(c)[0:8, :]

    nq_ref[0] = (_dot(xb, w_ref[:, CQ:CQ + GROUP_W]) * SCALE).astype(bf16)
    ckv = _dot(xb, w_ref[:, CKV:CKV + 6 * HEAD_DIM])
    nkv_ref[0] = ckv[:, 0:4 * HEAD_DIM]
    win_ref[0] = ckv[:, 4 * HEAD_DIM:6 * HEAD_DIM]
    nb_ref[0] = ckv.astype(bf16)

    z = _dot(xb, w_ref[:, DC:DC + GROUP_W]) * _dot(xb, w_ref[:, DH:DH + GROUP_W])
    row = lax.broadcasted_iota(jnp.int32, (tm, GROUP_W), 0)
    zp = carry_z[...]
    z1 = jnp.where(row == 0, zp[7:8, :], pltpu.roll(z, 1, 0))
    z2 = jnp.where(row == 0, zp[6:7, :], jnp.where(row == 1, zp[7:8, :], pltpu.roll(z, 2, 0)))
    cw = cw_ref[...]
    y_conv = cw[0:1, :] * z2 + cw[1:2, :] * z1 + cw[2:3, :] * z
    yd_ref[0] = (_dot(xb, w_ref[:, DB:DB + GROUP_W]) * y_conv).astype(bf16)
    carry_z[...] = z[tm - 8:tm, :]
    cst_ref[0] = z[tm - 8:tm, :]


def _prompt_inproj(x, w_pack, fb_row, wst, bs_full, cw8, avg, tri, tm):
    bsz, t, d = x.shape
    nt = t // tm
    row_spec = lambda w: pl.BlockSpec((1, tm, w), lambda b, i: (b, i, 0))
    const = lambda a: pl.BlockSpec(a.shape, lambda b, i: (0,) * a.ndim)
    outs = [
        (GROUP_W, bf16), (GROUP_W, bf16), (2 * GROUP_W, f32), (2 * GROUP_W, bf16), (LANES, f32), (LANES, f32),
    ]
    out_shape = [jax.ShapeDtypeStruct((bsz, t, w), dt) for w, dt in outs]
    out_specs = [row_spec(w) for w, _ in outs]
    out_shape.append(jax.ShapeDtypeStruct((bsz, 8, t), f32))
    out_specs.append(pl.BlockSpec((1, 8, tm), lambda b, i: (b, 0, i)))
    outs2 = [(GROUP_W, bf16), (4 * HEAD_DIM, f32), (6 * HEAD_DIM, bf16), (2 * HEAD_DIM, f32), (GROUP_W, bf16)]
    out_shape += [jax.ShapeDtypeStruct((bsz, t, w), dt) for w, dt in outs2]
    out_specs += [row_spec(w) for w, _ in outs2]
    out_shape.append(jax.ShapeDtypeStruct((bsz, 8, GROUP_W), f32))
    out_specs.append(pl.BlockSpec((1, 8, GROUP_W), lambda b, i: (b, 0, 0)))
    consts = [w_pack, fb_row, wst, bs_full, cw8, avg, tri]
    return pl.pallas_call(
        functools.partial(_prompt_inproj_kernel, tm=tm),
        grid=(bsz, nt),
        in_specs=[row_spec(d)] + [const(a) for a in consts],
        out_specs=out_specs,
        out_shape=out_shape,
        scratch_shapes=[pltpu.VMEM((8, LANES), f32), pltpu.VMEM((8, GROUP_W), f32)],
        compiler_params=_cparams(("arbitrary", "arbitrary")),
    )(x, *consts)


def _fox_prompt_kernel(q_ref, kv_ref, ccol_ref, crow_ref, o_ref, m_sc, l_sc, acc_sc, *, tk):
    qi = pl.program_id(1)
    rows = GROUP_HEADS * Q_BLOCK
    qbd = _head_blockdiag_q(q_ref[0], Q_BLOCK)
    cc = ccol_ref[0]
    cq = jnp.concatenate([cc[:, h:h + 1] for h in range(GROUP_HEADS)], axis=0)
    qpos = qi * Q_BLOCK + lax.broadcasted_iota(jnp.int32, (Q_BLOCK, 1), 0)
    qpos = jnp.concatenate([qpos] * GROUP_HEADS, axis=0)
    m_sc[...] = jnp.full_like(m_sc, NEG)
    l_sc[...] = jnp.zeros_like(l_sc)
    acc_sc[...] = jnp.zeros_like(acc_sc)
    n_kv = (qi * Q_BLOCK + Q_BLOCK + tk - 1) // tk

    def body(j, carry):
        k0 = pl.multiple_of(j * tk, tk)
        k = kv_ref[0, pl.ds(k0, tk), 0:GROUP_W]
        v = kv_ref[0, pl.ds(k0, tk), GROUP_W:2 * GROUP_W]
        cr = crow_ref[0, :, pl.ds(k0, tk)]
        ck = jnp.concatenate([jnp.broadcast_to(cr[h:h + 1, :], (Q_BLOCK, tk)) for h in range(GROUP_HEADS)], axis=0)
        s = _dot_nt(qbd, k) + cq - ck
        kpos = k0 + lax.broadcasted_iota(jnp.int32, (1, tk), 1)
        s = jnp.where(kpos <= qpos, s, NEG)
        m_old = m_sc[...]
        m_new = jnp.maximum(m_old, jnp.max(s, axis=-1, keepdims=True))
        alpha = jnp.exp(m_old - m_new)
        p = jnp.exp(s - m_new)
        l_sc[...] = alpha * l_sc[...] + jnp.sum(p, axis=-1, keepdims=True)
        acc_sc[...] = alpha * acc_sc[...] + _dot(p.astype(bf16), v)
        m_sc[...] = m_new
        return carry

    lax.fori_loop(0, n_kv, body, 0)
    o = acc_sc[...] / l_sc[...]
    o_ref[0] = _head_diag(o, Q_BLOCK).astype(bf16)


def _fox_prompt(fq, fkvb, ccol, crow, tk):
    bsz, t, _ = fq.shape
    rows = GROUP_HEADS * Q_BLOCK
    return pl.pallas_call(
        functools.partial(_fox_prompt_kernel, tk=tk),
        grid=(bsz, t // Q_BLOCK),
        in_specs=[
            pl.BlockSpec((1, Q_BLOCK, GROUP_W), lambda b, i: (b, i, 0)),
            pl.BlockSpec((1, t, 2 * GROUP_W), lambda b, i: (b, 0, 0)),
            pl.BlockSpec((1, Q_BLOCK, LANES), lambda b, i: (b, i, 0)),
            pl.BlockSpec((1, 8, t), lambda b, i: (b, 0, 0)),
        ],
        out_specs=pl.BlockSpec((1, Q_BLOCK, GROUP_W), lambda b, i: (b, i, 0)),
        out_shape=jax.ShapeDtypeStruct((bsz, t, GROUP_W), bf16),
        scratch_shapes=[pltpu.VMEM((rows, 1), f32), pltpu.VMEM((rows, 1), f32), pltpu.VMEM((rows, GROUP_W), f32)],
        compiler_params=_cparams(("arbitrary", "arbitrary")),
    )(fq, fkvb, ccol, crow)


def _compress_kernel(x_ref, pe_ref, w1_ref, w2_ref, o_ref):
    x = x_ref[0]
    n_grp = x.shape[0]
    top = _dot((x + pe_ref[0:1, :]).astype(bf16), w1_ref[:, 0:2 * CMP_HID])
    bot = _dot((x + pe_ref[1:2, :]).astype(bf16), w1_ref[:, 2 * CMP_HID:4 * CMP_HID])
    g = _gelu(top + pltpu.roll(bot, n_grp - 1, 0)).astype(bf16)
    w2 = w2_ref[...]
    kc = _dot(g[:, 0:CMP_HID], w2[:, 0:HEAD_DIM])
    vc = _dot(g[:, CMP_HID:2 * CMP_HID], w2[:, HEAD_DIM:2 * HEAD_DIM])
    o_ref[0] = jnp.concatenate([kc, vc], axis=-1).astype(bf16)


def _compress(x_groups, pe2, w1e, w2c):
    bsz, n_grp, w = x_groups.shape
    const = lambda a: pl.BlockSpec(a.shape, lambda b: (0,) * a.ndim)
    return pl.pallas_call(
        _compress_kernel,
        grid=(bsz,),
        in_specs=[pl.BlockSpec((1, n_grp, w), lambda b: (b, 0, 0)), const(pe2), const(w1e), const(w2c)],
        out_specs=pl.BlockSpec((1, n_grp, 2 * HEAD_DIM), lambda b: (b, 0, 0)),
        out_shape=jax.ShapeDtypeStruct((bsz, n_grp, 2 * HEAD_DIM), bf16),
        compiler_params=_cparams(("arbitrary",)),
    )(x_groups, pe2, w1e, w2c)


def _stack_heads(q, rows):
    return jnp.concatenate([q[:, h * HEAD_DIM:(h + 1) * HEAD_DIM] for h in range(GROUP_HEADS)], axis=0)


def _unstack_heads(o, rows):
    return jnp.concatenate([o[h * rows:(h + 1) * rows] for h in range(GROUP_HEADS)], axis=-1)


def _gate_col(misc, branch):
    base = N_FOX_F + branch * GROUP_HEADS
    return jnp.concatenate([misc[:, base + h:base + h + 1] for h in range(GROUP_HEADS)], axis=0)


def _nsa_compressed(qs, cmp, qpos_s, selw, nq):
    n_c = cmp.shape[0]
    kc = cmp[:, 0:HEAD_DIM]
    vc = cmp[:, HEAD_DIM:2 * HEAD_DIM]
    s = _dot_nt(qs, kc)
    cend = lax.broadcasted_iota(jnp.int32, (1, n_c), 1) * CMP_STRIDE + (CMP_LEN - 1)
    vis = cend <= qpos_s
    s = jnp.where(vis, s, NEG)
    e = jnp.exp(s - jnp.max(s, axis=-1, keepdims=True))
    p = e / jnp.sum(e, axis=-1, keepdims=True)
    p = p * jnp.where(qpos_s >= CMP_LEN - 1, 1.0, 0.0)
    pb = p.astype(bf16)
    o_c = _dot(pb, vc)
    imp = _dot(pb[0:nq], selw)
    for h in range(1, GROUP_HEADS):
        imp = imp + _dot(pb[h * nq:(h + 1) * nq], selw)
    return o_c, imp


def _nsa_select(imp, qpos_row, n_sel_pad):
    nq = imp.shape[0]
    if nq < LANES:
        imp = jnp.concatenate([imp, jnp.zeros((LANES - nq, LANES), f32)], axis=0)
    imp_t = jnp.transpose(imp)[0:n_sel_pad]
    blk = lax.broadcasted_iota(jnp.int32, (n_sel_pad, LANES), 0)
    cur = qpos_row // SEL_LEN
    valid = blk * SEL_LEN <= qpos_row
    forced = (blk == 0) | (blk == cur) | (blk == cur - 1)
    val = jnp.where(valid, jnp.where(forced, BIG, imp_t), -BIG)
    rank = jnp.zeros((n_sel_pad, LANES), f32)
    for mp in range(n_sel_pad):
        vb = val[mp:mp + 1, :]
        rank = rank + jnp.where(vb > val, 1.0, jnp.where(vb == val, jnp.where(blk > mp, 1.0, 0.0), 0.0))
    sel_t = jnp.where(valid, jnp.where(rank < SEL_TOPK, 1.0, 0.0), 0.0)
    if n_sel_pad < LANES:
        sel_t = jnp.concatenate([sel_t, jnp.zeros((LANES - n_sel_pad, LANES), f32)], axis=0)
    return jnp.transpose(sel_t)[0:nq]


def _nsa_prompt_kernel(q_ref, nb_ref, cmp_ref, misc_ref, selw_ref, exp_ref, o_ref, m_sc, l_sc, acc_sc, *, tk, t, n_sel_pad):
    qi = pl.program_id(1)
    nq = Q_BLOCK
    rows = GROUP_HEADS * nq
    qs = _stack_heads(q_ref[0], nq)
    qpos = qi * nq + lax.broadcasted_iota(jnp.int32, (nq, 1), 0)
    qpos_s = jnp.concatenate([qpos] * GROUP_HEADS, axis=0)
    qpos_row = qi * nq + lax.broadcasted_iota(jnp.int32, (1, nq), 1)

    o_c, imp = _nsa_compressed(qs, cmp_ref[0], qpos_s, selw_ref[...], nq)
    sel = _nsa_select(imp, qpos_row, n_sel_pad).astype(bf16)

    m_sc[...] = jnp.full_like(m_sc, NEG)
    l_sc[...] = jnp.zeros_like(l_sc)
    acc_sc[...] = jnp.zeros_like(acc_sc)
    n_kv = (qi * nq + nq + tk - 1) // tk

    def body(j, carry):
        k0 = pl.multiple_of(j * tk, tk)
        k = nb_ref[0, pl.ds(k0, tk), 2 * HEAD_DIM:3 * HEAD_DIM]
        v = nb_ref[0, pl.ds(k0, tk), 3 * HEAD_DIM:4 * HEAD_DIM]
        picked = _dot(sel, exp_ref[:, pl.ds(k0, tk)])
        kpos = k0 + lax.broadcasted_iota(jnp.int32, (1, tk), 1)
        keep = jnp.where(kpos <= qpos, picked, 0.0)
        keep = jnp.concatenate([keep] * GROUP_HEADS, axis=0)
        s = jnp.where(keep > 0.5, _dot_nt(qs, k), NEG)
        m_old = m_sc[...]
        m_new = jnp.maximum(m_old, jnp.max(s, axis=-1, keepdims=True))
        alpha = jnp.exp(m_old - m_new)
        p = jnp.exp(s - m_new)
        l_sc[...] = alpha * l_sc[...] + jnp.sum(p, axis=-1, keepdims=True)
        acc_sc[...] = alpha * acc_sc[...] + _dot(p.astype(bf16), v)
        m_sc[...] = m_new
        return carry

    lax.fori_loop(0, n_kv, body, 0)
    o_s = acc_sc[...] / l_sc[...]

    span = min(WINDOW + nq, t)
    w0 = pl.multiple_of(jnp.maximum(qi * nq - WINDOW, 0), nq)
    kw = nb_ref[0, pl.ds(w0, span), 4 * HEAD_DIM:5 * HEAD_DIM]
    vw = nb_ref[0, pl.ds(w0, span), 5 * HEAD_DIM:6 * HEAD_DIM]
    wpos = w0 + lax.broadcasted_iota(jnp.int32, (1, span), 1)
    s = jnp.where((wpos <= qpos_s) & (wpos > qpos_s - WINDOW), _dot_nt(qs, kw), NEG)
    e = jnp.exp(s - jnp.max(s, axis=-1, keepdims=True))
    p = e / jnp.sum(e, axis=-1, keepdims=True)
    o_w = _dot(p.astype(bf16), vw)

    misc = misc_ref[0]
    o = _gate_col(misc, 0) * o_c + _gate_col(misc, 1) * o_s + _gate_col(misc, 2) * o_w
    o_ref[0] = _unstack_heads(o, nq).astype(bf16)


def _nsa_prompt(nq_b, nb, cmp, misc, selw, expand, tk):
    bsz, t, _ = nq_b.shape
    rows = GROUP_HEADS * Q_BLOCK
    n_c = cmp.shape[1]
    n_sel_pad = -(-(t // SEL_LEN) // 8) * 8
    const = lambda a: pl.BlockSpec(a.shape, lambda b, i: (0,) * a.ndim)
    return pl.pallas_call(
        functools.partial(_nsa_prompt_kernel, tk=tk, t=t, n_sel_pad=n_sel_pad),
        grid=(bsz, t // Q_BLOCK),
        in_specs=[
            pl.BlockSpec((1, Q_BLOCK, GROUP_W), lambda b, i: (b, i, 0)),
            pl.BlockSpec((1, t, 6 * HEAD_DIM), lambda b, i: (b, 0, 0)),
            pl.BlockSpec((1, n_c, 2 * HEAD_DIM), lambda b, i: (b, 0, 0)),
            pl.BlockSpec((1, Q_BLOCK, LANES), lambda b, i: (b, i, 0)),
            const(selw), const(expand),
        ],
        out_specs=pl.BlockSpec((1, Q_BLOCK, GROUP_W), lambda b, i: (b, i, 0)),
        out_shape=jax.ShapeDtypeStruct((bsz, t, GROUP_W), bf16),
        scratch_shapes=[pltpu.VMEM((rows, 1), f32), pltpu.VMEM((rows, 1), f32), pltpu.VMEM((rows, HEAD_DIM), f32)],
        compiler_params=_cparams(("arbitrary", "arbitrary")),
    )(nq_b, nb, cmp, misc, selw, expand)


def _matmul_kernel(x_ref, w_ref, o_ref):
    o_ref[...] = _dot(x_ref[...].astype(bf16), w_ref[...])


def _sample_inproj(x2, w_pack, tm):
    n, d = x2.shape
    pw = w_pack.shape[1]
    return pl.pallas_call(
        _matmul_kernel,
        grid=(n // tm,),
        in_specs=[pl.BlockSpec((tm, d), lambda i: (i, 0)), pl.BlockSpec((d, pw), lambda i: (0, 0))],
        out_specs=pl.BlockSpec((tm, pw), lambda i: (i, 0)),
        out_shape=jax.ShapeDtypeStruct((n, pw), f32),
        compiler_params=_cparams(("arbitrary",)),
    )(x2, w_pack)


def _pad_to(a, rows):
    return jnp.concatenate([a, jnp.zeros((rows - a.shape[0], a.shape[1]), a.dtype)], axis=0)


def _softmax_tiles(tiles):
    m = tiles[0].max(axis=-1, keepdims=True)
    for s in tiles[1:]:
        m = jnp.maximum(m, s.max(axis=-1, keepdims=True))
    ps = [jnp.exp(s - m) for s in tiles]
    tot = ps[0].sum(axis=-1, keepdims=True)
    for p in ps[1:]:
        tot = tot + p.sum(axis=-1, keepdims=True)
    return ps, tot


def _sample_mixer_kernel(pt_ref, h_ref, *refs, n_pages, page, past, n_sel_pad):
    del pt_ref
    fox_pages = refs[0:n_pages]
    lf_pages = refs[n_pages:2 * n_pages]
    nsa_pages = refs[2 * n_pages:3 * n_pages]
    (win_ref, conv_ref, fb_ref, wst_ref, bs_ref, cw_ref, avg_ref, utri_ref, ones_ref, slt_ref, tri8_ref,
     pe_ref, w1_ref, w2_ref, selw_ref, exp_ref,
     y_ref, fkv_ref, misc_ref, nrow_ref, wout_ref, cout_ref, gv_ref, rows_sc) = refs[3 * n_pages:]
    tn = h_ref.shape[1]
    h = h_ref[0]

    u = _gelu(h[:, AU:AU + GROUP_W])
    vn = _head_layer_norm(_gelu(h[:, AV:AV + GROUP_W]), avg_ref[...])
    gv_ref[0] = vn
    w8 = jnp.concatenate([wst_ref[hh * CHUNK:hh * CHUNK + tn, :] for hh in range(GROUP_HEADS)], axis=0)
    z_all = _dot(w8, _pad_to(vn, CHUNK).astype(bf16))
    y_a = u * (_head_diag(z_all, tn) + bs_ref[0:tn, :])

    z = h[:, DC:DC + GROUP_W] * h[:, DH:DH + GROUP_W]
    zp = conv_ref[0, 0]
    row = lax.broadcasted_iota(jnp.int32, (tn, GROUP_W), 0)
    z1 = jnp.where(row == 0, zp[1:2, :], pltpu.roll(z, 1, 0))
    z2 = jnp.where(row == 0, zp[0:1, :], jnp.where(row == 1, zp[1:2, :], pltpu.roll(z, 2, 0)))
    cw = cw_ref[...]
    y_d = h[:, DB:DB + GROUP_W] * (cw[0:1, :] * z2 + cw[1:2, :] * z1 + cw[2:3, :] * z)
    cout_ref[0] = z

    pre = h[:, MISC:MISC + LANES] + fb_ref[0:1, :]
    lane = lax.broadcasted_iota(jnp.int32, (tn, LANES), 1)
    lf_new = jnp.where(lane < N_FOX_F, _log_sigmoid(pre), 0.0)
    misc = jnp.where(lane < N_FOX_F, lf_new, jnp.where(lane < N_FOX_F + N_GATES, jax.nn.sigmoid(pre), 0.0))
    misc_ref[0] = misc

    lf3 = _split3(jnp.concatenate([r[0, 0] for r in lf_pages], axis=0))
    tot = _dot_exact_lhs(lf3, ones_ref[...])
    off = _dot_exact_rhs(slt_ref[...], _split3(_pad_to(tot, LANES)))
    c_past = _dot_exact_lhs(lf3, utri_ref[...]) + off
    last = GROUP_HEADS * (n_pages - 1)
    total_past = off[last:last + GROUP_HEADS] + tot[last:last + GROUP_HEADS]
    c_new = jnp.transpose(_pad_to(total_past, LANES))[0:1, :] + _dot_exact_rhs(
        tri8_ref[...], _split3(_pad_to(lf_new, LANES)))
    cq = jnp.concatenate([c_new[:, hh:hh + 1] for hh in range(GROUP_HEADS)], axis=0)
    c_new_t = jnp.transpose(_pad_to(c_new, LANES))
    qpos = past + lax.broadcasted_iota(jnp.int32, (tn, 1), 0)
    qpos_s = jnp.concatenate([qpos] * GROUP_HEADS, axis=0)
    new_pos = past + lax.broadcasted_iota(jnp.int32, (1, LANES), 1)
    new_ok = new_pos <= qpos_s

    qbd = _head_blockdiag_q((h[:, BQ:BQ + GROUP_W] * SCALE).astype(bf16), tn)
    kv_new = h[:, BK:BK + 2 * GROUP_W]
    fkv_ref[0] = kv_new
    kvb_new = _pad_to(kv_new, LANES).astype(bf16)

    def head_rows(c4):
        return jnp.concatenate([jnp.broadcast_to(c4[hh:hh + 1, :], (tn, c4.shape[1])) for hh in range(GROUP_HEADS)],
                               axis=0)

    tiles = []
    for pg in range(n_pages):
        k_t = fox_pages[pg][0, 0, 0:GROUP_W, :].astype(bf16)
        tiles.append(_dot(qbd, k_t) + cq - head_rows(c_past[GROUP_HEADS * pg:GROUP_HEADS * (pg + 1)]))
    tiles.append(jnp.where(new_ok, _dot_nt(qbd, kvb_new[:, 0:GROUP_W]) + cq - head_rows(c_new_t[0:GROUP_HEADS]), NEG))
    ps, tot_p = _softmax_tiles(tiles)
    acc = _dot(ps[n_pages].astype(bf16), kvb_new[:, GROUP_W:2 * GROUP_W])
    for pg in range(n_pages):
        acc = acc + _dot_nt(ps[pg].astype(bf16), fox_pages[pg][0, 0, GROUP_W:2 * GROUP_W, :].astype(bf16))
    y_b = _head_diag(acc / tot_p, tn)

    ckv = h[:, CKV:CKV + 6 * HEAD_DIM]
    nrow_ref[0] = ckv[:, 0:4 * HEAD_DIM]
    for pg in range(n_pages):
        rows_sc[pg * page:(pg + 1) * page, :] = jnp.transpose(nsa_pages[pg][0, 0, 0:2 * HEAD_DIM, :])
    n_grp = n_pages * page // CMP_STRIDE
    half = CMP_LEN // 2
    top = None
    bot = None
    for r in range(half):
        a_r = rows_sc[pl.ds(r, n_grp, stride=CMP_STRIDE), :]
        t_r = _dot((a_r + pe_ref[r:r + 1, :]).astype(bf16), w1_ref[r])
        b_r = _dot((a_r + pe_ref[half + r:half + r + 1, :]).astype(bf16), w1_ref[half + r])
        top = t_r if top is None else top + t_r
        bot = b_r if bot is None else bot + b_r
    g = _gelu(top + pltpu.roll(bot, n_grp - 1, 0)).astype(bf16)
    w2 = w2_ref[...]
    cmp = jnp.concatenate([_dot(g[:, 0:CMP_HID], w2[:, 0:HEAD_DIM]),
                           _dot(g[:, CMP_HID:2 * CMP_HID], w2[:, HEAD_DIM:2 * HEAD_DIM])], axis=-1).astype(bf16)
    qs = _stack_heads((h[:, CQ:CQ + GROUP_W] * SCALE).astype(bf16), tn)
    o_c, imp = _nsa_compressed(qs, cmp, qpos_s, selw_ref[...], tn)
    sel = _nsa_select(imp, new_pos, n_sel_pad).astype(bf16)
    sel_new = _pad_to(ckv[:, 2 * HEAD_DIM:4 * HEAD_DIM], LANES).astype(bf16)
    tiles = []
    for pg in range(n_pages + 1):
        picked = _dot(sel, exp_ref[:, pg * page:(pg + 1) * page])
        if pg == n_pages:
            picked = jnp.where(new_pos <= qpos, picked, 0.0)
            sc = _dot_nt(qs, sel_new[:, 0:HEAD_DIM])
        else:
            sc = _dot(qs, nsa_pages[pg][0, 0, 2 * HEAD_DIM:3 * HEAD_DIM, :].astype(bf16))
        keep = jnp.concatenate([picked] * GROUP_HEADS, axis=0) > 0.5
        tiles.append(jnp.where(keep, sc, NEG))
    ps, tot_p = _softmax_tiles(tiles)
    acc = _dot(ps[n_pages].astype(bf16), sel_new[:, HEAD_DIM:2 * HEAD_DIM])
    for pg in range(n_pages):
        acc = acc + _dot_nt(ps[pg].astype(bf16), nsa_pages[pg][0, 0, 3 * HEAD_DIM:4 * HEAD_DIM, :].astype(bf16))
    o_s = acc / tot_p

    win_old = win_ref[0, 0]
    win_new = ckv[:, 4 * HEAD_DIM:6 * HEAD_DIM]
    new_t = pltpu.roll(jnp.transpose(_pad_to(win_new, LANES)), LANES - tn, 1)
    new_wide = jnp.concatenate([jnp.zeros((2 * HEAD_DIM, WINDOW - LANES), f32), new_t], axis=1)
    w_idx = lax.broadcasted_iota(jnp.int32, (1, WINDOW), 1)
    wout_ref[0] = jnp.where(w_idx >= WINDOW - tn, new_wide, pltpu.roll(win_old, WINDOW - tn, 1))
    wob = win_old.astype(bf16)
    wnb = _pad_to(win_new, LANES).astype(bf16)
    s_old = jnp.where(w_idx > qpos_s - past, _dot(qs, wob[0:HEAD_DIM, :]), NEG)
    s_new = jnp.where(new_ok, _dot_nt(qs, wnb[:, 0:HEAD_DIM]), NEG)
    ps, tot_p = _softmax_tiles([s_old, s_new])
    o_w = (_dot_nt(ps[0].astype(bf16), wob[HEAD_DIM:2 * HEAD_DIM, :])
           + _dot(ps[1].astype(bf16), wnb[:, HEAD_DIM:2 * HEAD_DIM])) / tot_p

    o = _gate_col(misc, 0) * o_c + _gate_col(misc, 1) * o_s + _gate_col(misc, 2) * o_w
    y_c = _unstack_heads(o, tn)
    y_ref[0] = jnp.concatenate([y_a, y_b, y_c, y_d], axis=-1)


def _sample_mixer(h3, fox_t, lf_t, nsa_t, win_t, conv4, page_table, layer, consts):
    bsz, tn, pw = h3.shape
    n_pages = page_table.shape[1]
    page = fox_t.shape[3]
    past = n_pages * page
    n_sel_pad = -(-(-(-(past + tn) // SEL_LEN)) // 8) * 8

    def page_spec(arr, pg):
        blk = (1, 1) + arr.shape[2:]
        return pl.BlockSpec(blk, lambda b, pt: (layer, pt[b, pg], 0, 0))

    in_specs = [pl.BlockSpec((1, tn, pw), lambda b, pt: (b, 0, 0))]
    operands = [h3]
    for arr in (fox_t, lf_t, nsa_t):
        for pg in range(n_pages):
            in_specs.append(page_spec(arr, pg))
            operands.append(arr)
    in_specs.append(pl.BlockSpec((1, 1) + win_t.shape[2:], lambda b, pt: (layer, b, 0, 0)))
    operands.append(win_t)
    in_specs.append(pl.BlockSpec((1, 1) + conv4.shape[2:], lambda b, pt: (layer, b, 0, 0)))
    operands.append(conv4)
    for a in consts:
        in_specs.append(pl.BlockSpec(a.shape, lambda b, pt, nd=a.ndim: (0,) * nd))
        operands.append(a)
    out_dims = [(tn, 4 * GROUP_W), (tn, 2 * GROUP_W), (tn, LANES), (tn, 4 * HEAD_DIM), (2 * HEAD_DIM, WINDOW),
                (tn, GROUP_W), (tn, GROUP_W)]
    out_shape = [jax.ShapeDtypeStruct((bsz, r, w), f32) for r, w in out_dims]
    out_specs = [pl.BlockSpec((1, r, w), lambda b, pt: (b, 0, 0)) for r, w in out_dims]
    return pl.pallas_call(
        functools.partial(_sample_mixer_kernel, n_pages=n_pages, page=page, past=past, n_sel_pad=n_sel_pad),
        grid_spec=pltpu.PrefetchScalarGridSpec(num_scalar_prefetch=1, grid=(bsz,), in_specs=in_specs,
                                               out_specs=out_specs,
                                               scratch_shapes=[pltpu.VMEM((past, 2 * HEAD_DIM), f32)]),
        out_shape=out_shape,
        compiler_params=_cparams(("arbitrary",)),
    )(page_table, *operands)


def _outproj_kernel(x_ref, *refs, alpha):
    y_refs = refs[:-4]
    w_ref, g_ref, b_ref, o_ref = refs[-4:]
    acc = None
    c0 = 0
    for y_ref in y_refs:
        wy = y_ref.shape[1]
        part = _dot(y_ref[...].astype(bf16), w_ref[c0:c0 + wy, :])
        acc = part if acc is None else acc + part
        c0 += wy
    o_ref[...] = _layer_norm_rows(alpha * x_ref[...] + acc, g_ref[0:1, :], b_ref[0:1, :])


def _outproj(x2, ys, w_out_b, g8, b8, alpha, tm):
    n, d = x2.shape
    assert sum(y.shape[1] for y in ys) == d
    const = lambda a: pl.BlockSpec(a.shape, lambda i: (0,) * a.ndim)
    return pl.pallas_call(
        functools.partial(_outproj_kernel, alpha=alpha),
        grid=(n // tm,),
        in_specs=[pl.BlockSpec((tm, d), lambda i: (i, 0))] + [pl.BlockSpec((tm, y.shape[1]), lambda i: (i, 0)) for y in ys]
        + [const(w_out_b), const(g8), const(b8)],
        out_specs=pl.BlockSpec((tm, d), lambda i: (i, 0)),
        out_shape=jax.ShapeDtypeStruct((n, d), f32),
        compiler_params=_cparams(("arbitrary",)),
    )(x2, *ys, w_out_b, g8, b8)


def _route(xb, wr_t, tm):
    logits = _dot_nt(wr_t, xb)
    e = jnp.exp(logits - jnp.max(logits, axis=0, keepdims=True))
    probs = e / jnp.sum(e, axis=0, keepdims=True)
    p = [probs[i:i + 1, :] for i in range(N_EXPERTS)]
    one = jnp.ones((1, tm), f32)
    zero = jnp.zeros((1, tm), f32)
    in_top = []
    gscore = []
    for g in range(N_GROUPS):
        members = range(g * EXP_PER_GROUP, (g + 1) * EXP_PER_GROUP)
        score = zero
        for i in members:
            rank = zero
            for j in members:
                if j < i:
                    rank = rank + jnp.where(p[j] >= p[i], one, zero)
                elif j > i:
                    rank = rank + jnp.where(p[j] > p[i], one, zero)
            keep = jnp.where(rank < TOP_K, one, zero)
            in_top.append(keep)
            score = score + keep * p[i]
        gscore.append(score)
    gate_rows = []
    wsum = zero
    chosen = []
    for g in range(N_GROUPS):
        win = one
        for g2 in range(N_GROUPS):
            if g2 < g:
                win = win * jnp.where(gscore[g] > gscore[g2], one, zero)
            elif g2 > g:
                win = win * jnp.where(gscore[g] >= gscore[g2], one, zero)
        for i in range(g * EXP_PER_GROUP, (g + 1) * EXP_PER_GROUP):
            c = in_top[i] * win
            chosen.append(c)
            wsum = wsum + c * p[i]
    for i in range(N_EXPERTS):
        gate_rows.append(jnp.where(chosen[i] > 0.5, p[i] / wsum, zero))
    gate_t = jnp.concatenate(gate_rows + [jnp.zeros((LANES - N_EXPERTS, tm), f32)], axis=0)
    return jnp.concatenate([jnp.transpose(gate_t[:, c * LANES:(c + 1) * LANES]) for c in range(tm // LANES)], axis=0)


def _moe_kernel(x_ref, wr_ref, wg_ref, wu_ref, wd_ref, g_ref, b_ref, o_ref, gate_sc, acc_sc, xb_sc, *, alpha, tm):
    e = pl.program_id(1)

    @pl.when(e == 0)
    def _():
        xb = x_ref[...].astype(bf16)
        xb_sc[...] = xb
        gate_sc[...] = _route(xb, wr_ref[...], tm)
        acc_sc[...] = jnp.zeros_like(acc_sc)

    xb = xb_sc[...]
    lane = lax.broadcasted_iota(jnp.int32, (tm, LANES), 1)
    gcol = jnp.sum(jnp.where(lane == e, gate_sc[...], 0.0), axis=-1, keepdims=True)
    h = jax.nn.silu(_dot(xb, wg_ref[0])) * _dot(xb, wu_ref[0])
    acc_sc[...] += _dot((h * gcol).astype(bf16), wd_ref[0])

    @pl.when(e == N_EXPERTS - 1)
    def _():
        o_ref[...] = _layer_norm_rows(alpha * x_ref[...] + acc_sc[...], g_ref[0:1, :], b_ref[0:1, :])


def _moe(x2, wr_t, wg, wu, wd, g8, b8, alpha, tm):
    n, d = x2.shape
    de = wg.shape[-1]
    const = lambda a: pl.BlockSpec(a.shape, lambda i, e: (0,) * a.ndim)
    return pl.pallas_call(
        functools.partial(_moe_kernel, alpha=alpha, tm=tm),
        grid=(n // tm, N_EXPERTS),
        in_specs=[
            pl.BlockSpec((tm, d), lambda i, e: (i, 0)), const(wr_t),
            pl.BlockSpec((1, d, de), lambda i, e: (e, 0, 0)), pl.BlockSpec((1, d, de), lambda i, e: (e, 0, 0)),
            pl.BlockSpec((1, de, d), lambda i, e: (e, 0, 0)), const(g8), const(b8),
        ],
        out_specs=pl.BlockSpec((tm, d), lambda i, e: (i, 0)),
        out_shape=jax.ShapeDtypeStruct((n, d), f32),
        scratch_shapes=[pltpu.VMEM((tm, LANES), f32), pltpu.VMEM((tm, d), f32), pltpu.VMEM((tm, d), bf16)],
        compiler_params=_cparams(("arbitrary", "arbitrary")),
    )(x2, wr_t, wg, wu, wd, g8, b8)


def _pad_rows(a, rows=8):
    return jnp.concatenate([a, jnp.zeros((rows - a.shape[0],) + a.shape[1:], a.dtype)], axis=0)


def _prep_layer(l, w_in, fox_b_f, gmlp_w_s, gmlp_b_s, nsa_cmp_pe, nsa_cmp_w1, nsa_cmp_w2, conv_w, w_out, ln_g, ln_b,
                w_exp_gate, w_exp_up, w_exp_down):
    w = w_in[l]
    d = w.shape[0]
    w_pack = jnp.concatenate([w[:, 0:1280], w[:, 1284:1924], w[:, 1936:2704], w[:, 1280:1284], w[:, 1924:1936],
                              jnp.zeros((d, LANES - N_FOX_F - N_GATES), w.dtype)], axis=1).astype(bf16)
    fb_row = _pad_rows(jnp.concatenate([fox_b_f[l], jnp.zeros((LANES - N_FOX_F,), f32)])[None, :])
    tril = jnp.asarray(np.tril(np.ones((CHUNK, CHUNK), np.float32)))
    wst = (gmlp_w_s[l] * tril).reshape(GROUP_HEADS * CHUNK, CHUNK).astype(bf16)
    bs_full = jnp.repeat(gmlp_b_s[l].T, HEAD_DIM, axis=1)
    cw8 = _pad_rows(conv_w[l])
    half = CMP_LEN // 2
    w1r = nsa_cmp_w1[l].reshape(2, CMP_LEN, HEAD_DIM, CMP_HID)
    per = nsa_cmp_pe[l]

    def w1_expand(kv, hf):
        m = jnp.zeros((half, 4 * HEAD_DIM, CMP_HID), f32)
        m = m.at[:, kv * HEAD_DIM:(kv + 1) * HEAD_DIM, :].set(w1r[kv, hf * half:(hf + 1) * half])
        return m.reshape(half * 4 * HEAD_DIM, CMP_HID)

    def pe_expand(hf):
        m = jnp.zeros((half, 4 * HEAD_DIM), f32)
        for kv in range(2):
            m = m.at[:, kv * HEAD_DIM:(kv + 1) * HEAD_DIM].set(per[kv, hf * half:(hf + 1) * half])
        return m.reshape(1, half * 4 * HEAD_DIM)

    w1e = jnp.concatenate([w1_expand(0, 0), w1_expand(1, 0), w1_expand(0, 1), w1_expand(1, 1)], axis=1).astype(bf16)
    pe2 = _pad_rows(jnp.concatenate([pe_expand(0), pe_expand(1)], axis=0))
    w2c = jnp.concatenate([nsa_cmp_w2[l, 0], nsa_cmp_w2[l, 1]], axis=1).astype(bf16)
    zed = jnp.zeros((CMP_LEN, HEAD_DIM, CMP_HID), f32)
    w1r = jnp.concatenate([jnp.concatenate([w1r[0], zed], axis=2), jnp.concatenate([zed, w1r[1]], axis=2)],
                          axis=1).astype(bf16)
    pe32 = jnp.concatenate([per[0], per[1]], axis=1)
    return dict(
        w_pack=w_pack, fb_row=fb_row, wst=wst, bs_full=bs_full, cw8=cw8, w1e=w1e, pe2=pe2, w2c=w2c, w1r=w1r, pe32=pe32,
        w_out=w_out[l].astype(bf16), g1=_pad_rows(ln_g[l, 0:1]), b1=_pad_rows(ln_b[l, 0:1]),
        g2=_pad_rows(ln_g[l, 1:2]), b2=_pad_rows(ln_b[l, 1:2]),
        wg=w_exp_gate[l].astype(bf16), wu=w_exp_up[l].astype(bf16), wd=w_exp_down[l].astype(bf16),
    )


def _sel_overlap(t, n_sel):
    n_cmp = (t - CMP_LEN) // CMP_STRIDE + 1
    start = np.arange(n_cmp) * CMP_STRIDE
    sstart = np.arange(n_sel) * SEL_LEN
    ov = np.minimum(start[:, None] + CMP_LEN, sstart[None, :] + SEL_LEN) - np.maximum(start[:, None], sstart[None, :])
    return np.clip(ov, 0, None) / CMP_STRIDE


def _head_avg():
    a = np.kron(np.eye(GROUP_HEADS, dtype=np.float32), np.full((HEAD_DIM, HEAD_DIM), 1.0 / HEAD_DIM, np.float32))
    return jnp.asarray(a, bf16)


def _post_block(x2, ys, p, wr_t, alpha, tm):
    x1 = _outproj(x2, ys, p['w_out'], p['g1'], p['b1'], alpha, tm)
    return _moe(x1, wr_t, p['wg'], p['wu'], p['wd'], p['g2'], p['b2'], alpha, tm)


def _prompt_layer(x, p, wr_t, alpha, tm=512, tk=512):
    bsz, t, d = x.shape
    n = bsz * t
    n_sel = t // SEL_LEN
    n_grp = t // CMP_STRIDE
    assert t % tm == 0 and t % tk == 0 and n_sel <= LANES and t >= WINDOW + Q_BLOCK
    tri = jnp.asarray(np.tril(np.ones((tm, tm), np.float32)), bf16)
    (ya, fq, fkv, fkvb, misc, ccol, crow, nq, nkv, nb, win, yd, cst) = _prompt_inproj(
        x, p['w_pack'], p['fb_row'], p['wst'], p['bs_full'], p['cw8'], _head_avg(), tri, tm)
    yb = _fox_prompt(fq, fkvb, ccol, crow, tk)
    cmp = _compress(nkv.reshape(bsz, n_grp, CMP_STRIDE * 4 * HEAD_DIM), p['pe2'], p['w1e'], p['w2c'])
    ov = _sel_overlap(t, n_sel)
    selw = np.zeros((n_grp, LANES), np.float32)
    selw[:ov.shape[0], :n_sel] = ov
    expand = np.zeros((LANES, t), np.float32)
    expand[np.arange(t) // SEL_LEN, np.arange(t)] = 1.0
    yc = _nsa_prompt(nq, nb, cmp, misc, jnp.asarray(selw, bf16), jnp.asarray(expand, bf16), tk)
    ys = [a.reshape(n, GROUP_W) for a in (ya, yb, yc, yd)]
    x_out = _post_block(x.reshape(n, d), ys, p, wr_t, alpha, tm).reshape(bsz, t, d)
    win_keep = min(WINDOW, t)
    states = (fkv.reshape(bsz, t, 2, GROUP_HEADS, HEAD_DIM), misc[:, :, 0:N_FOX_F],
              nkv.reshape(bsz, t, 4, HEAD_DIM), win[:, t - win_keep:].reshape(bsz, win_keep, 2, HEAD_DIM),
              cst[:, 8 - (CONV_W - 1):])
    return x_out, states, (ya, yb, yc, yd)


def _sample_consts(p, n_pages, page, tn):
    utri = np.triu(np.ones((page, page), np.float32))
    ones = np.ones((page, page), np.float32)
    r = np.arange(GROUP_HEADS * n_pages)
    slt = np.zeros((GROUP_HEADS * n_pages, LANES), np.float32)
    slt[:, :r.size] = ((r[:, None] % GROUP_HEADS) == (r[None, :] % GROUP_HEADS)) & (r[None, :] // GROUP_HEADS < r[:, None] // GROUP_HEADS)
    tri8 = np.zeros((tn, LANES), np.float32)
    tri8[:, :tn] = np.tril(np.ones((tn, tn), np.float32))
    t_all = n_pages * page + tn
    n_sel = -(-t_all // SEL_LEN)
    ov = _sel_overlap(t_all, n_sel)
    n_grp = n_pages * page // CMP_STRIDE
    assert ov.shape[0] <= n_grp and n_sel <= LANES
    selw = np.zeros((n_grp, LANES), np.float32)
    selw[:ov.shape[0], :n_sel] = ov
    n_keys = (n_pages + 1) * page
    expand = np.zeros((LANES, n_keys), np.float32)
    expand[np.arange(n_keys) // SEL_LEN, np.arange(n_keys)] = 1.0
    as_b = lambda a: jnp.asarray(np.asarray(a, np.float32), bf16)
    return [p['fb_row'], p['wst'], p['bs_full'], p['cw8'], _head_avg(), as_b(utri), as_b(ones), as_b(slt),
            as_b(tri8), p['pe32'], p['w1r'], p['w2c'], as_b(selw), as_b(expand)]


def _sample_layer(x, views, page_table, layer, p, wr_t, alpha, tm=512):
    bsz, tn, d = x.shape
    n = bsz * tn
    fox_t, lf_t, nsa_t, win_t, conv4 = views
    n_pages = page_table.shape[1]
    page = fox_t.shape[3]
    assert tn == 8 and page == LANES and win_t.shape[3] == WINDOW and n_pages * page >= WINDOW
    assert GROUP_HEADS * n_pages <= LANES
    h = _sample_inproj(x.reshape(n, d), p['w_pack'], tm).reshape(bsz, tn, P_PACK)
    y, fkv, misc, nrow, wout, cout, gv = _sample_mixer(h, fox_t, lf_t, nsa_t, win_t, conv4, page_table, layer,
                                                       _sample_consts(p, n_pages, page, tn))
    x_out = _post_block(x.reshape(n, d), [y.reshape(n, d)], p, wr_t, alpha, tm).reshape(bsz, tn, d)
    win_state = wout.reshape(bsz, 2, HEAD_DIM, WINDOW).transpose(0, 3, 1, 2)
    states = (fkv.reshape(bsz, tn, 2, GROUP_HEADS, HEAD_DIM), misc[:, :, 0:N_FOX_F], nrow.reshape(bsz, tn, 4, HEAD_DIM),
              win_state, cout[:, tn - (CONV_W - 1):], gv.reshape(bsz, tn, GROUP_HEADS, HEAD_DIM))
    return x_out, states


def kernel(x_prompt, x_sample, cache_fox_kv, cache_fox_logf, cache_nsa_kv, state_nsa_win, state_conv, page_table,
           w_in, fox_b_f, gmlp_w_s, gmlp_b_s, nsa_cmp_pe, nsa_cmp_w1, nsa_cmp_w2, conv_w, w_out, ln_g, ln_b,
           w_router, w_exp_gate, w_exp_up, w_exp_down):
    depth, n_pool, page = cache_fox_kv.shape[0:3]
    dec_b = x_sample.shape[0]
    alpha = (2 * depth) ** 0.25
    wr_t = w_router.T.astype(bf16)
    views = (cache_fox_kv.transpose(0, 1, 3, 4, 5, 2).reshape(depth, n_pool, 2 * GROUP_W, page),
             cache_fox_logf.transpose(0, 1, 3, 2),
             cache_nsa_kv.transpose(0, 1, 3, 4, 2).reshape(depth, n_pool, 4 * HEAD_DIM, page),
             state_nsa_win.transpose(0, 1, 3, 4, 2).reshape(depth, dec_b, 2 * HEAD_DIM, state_nsa_win.shape[2]),
             state_conv)
    xp, xs = x_prompt, x_sample
    st_p = [[] for _ in range(5)]
    st_s = [[] for _ in range(6)]
    for l in range(depth):
        p = _prep_layer(l, w_in, fox_b_f, gmlp_w_s, gmlp_b_s, nsa_cmp_pe, nsa_cmp_w1, nsa_cmp_w2, conv_w, w_out,
                        ln_g, ln_b, w_exp_gate, w_exp_up, w_exp_down)
        xp, sp, _ = _prompt_layer(xp, p, wr_t, alpha)
        xs, ss = _sample_layer(xs, views, page_table, l, p, wr_t, alpha)
        for lst, a in zip(st_p, sp):
            lst.append(a)
        for lst, a in zip(st_s, ss):
            lst.append(a)
    fox_kv_p, fox_lf_p, nsa_kv_p, nsa_win_p, conv_p = [jnp.stack(a, axis=0) for a in st_p]
    fox_kv_s, fox_lf_s, nsa_kv_s, nsa_win_s, conv_s, gmlp_v_s = [jnp.stack(a, axis=0) for a in st_s]
    return (xp, xs, fox_kv_p, fox_kv_s, fox_lf_p, fox_lf_s, nsa_kv_p, nsa_kv_s, nsa_win_p, nsa_win_s, conv_p, conv_s,
            gmlp_v_s)
```

```python
import functools

import numpy as np
import jax
import jax.numpy as jnp
from jax import lax
from jax.experimental import pallas as pl
from jax.experimental.pallas import tpu as pltpu

f32 = jnp.float32
bf16 = jnp.bfloat16

HEAD_DIM = 64
GROUP_W = 256
GROUP_HEADS = 4
CHUNK = 128
Q_BLOCK = 128
CMP_LEN = 32
CMP_STRIDE = 16
CMP_HID = 128
SEL_LEN = 64
SEL_TOPK = 16
WINDOW = 512
CONV_W = 3
N_EXPERTS = 16
N_GROUPS = 4
EXP_PER_GROUP = 4
TOP_K = 2
LN_EPS = 1e-5
SCALE = HEAD_DIM ** -0.5
NEG = -1e30
BIG = 1e30
LANES = 128

AU, AV, BQ, BK, BV, CQ, CKV, DH, DB, DC, MISC = 0, 256, 512, 768, 1024, 1280, 1536, 1920, 2176, 2432, 2688
P_PACK = 2816
N_FOX_F = GROUP_HEADS
N_GATES = 3 * GROUP_HEADS

VMEM_LIMIT = 56 * 1024 * 1024


def _cparams(sem):
    return pltpu.CompilerParams(dimension_semantics=sem, vmem_limit_bytes=VMEM_LIMIT)


def _dot(a, b):
    return jnp.dot(a, b, preferred_element_type=f32)


def _dot_nt(a, b):
    return lax.dot_general(a, b, (((1,), (1,)), ((), ())), preferred_element_type=f32)


def _split3(x):
    hi = x.astype(bf16)
    r = x - hi.astype(f32)
    mid = r.astype(bf16)
    lo = (r - mid.astype(f32)).astype(bf16)
    return hi, mid, lo


def _dot_exact_rhs(a, b3):
    return _dot(a, b3[0]) + _dot(a, b3[1]) + _dot(a, b3[2])


def _dot_exact_lhs(a3, b):
    return _dot(a3[0], b) + _dot(a3[1], b) + _dot(a3[2], b)


def _log_sigmoid(x):
    return jnp.minimum(x, 0.0) - jnp.log1p(jnp.exp(-jnp.abs(x)))


def _gelu(x):
    return jax.nn.gelu(x, approximate=True)


def _layer_norm_rows(x, g, b):
    mu = jnp.mean(x, axis=-1, keepdims=True)
    d = x - mu
    var = jnp.mean(d * d, axis=-1, keepdims=True)
    return d * lax.rsqrt(var + LN_EPS) * g + b


def _head_layer_norm(g, avg):
    h2 = g.astype(bf16)
    mu = _dot(h2, avg) + _dot((g - h2.astype(f32)).astype(bf16), avg)
    d = g - mu
    dd = d * d
    d2 = dd.astype(bf16)
    var = _dot(d2, avg) + _dot((dd - d2.astype(f32)).astype(bf16), avg)
    return d * lax.rsqrt(var + LN_EPS)


def _head_diag(z_all, rows):
    lane_head = lax.broadcasted_iota(jnp.int32, (rows, GROUP_W), 1) // HEAD_DIM
    out = jnp.where(lane_head == 0, z_all[0:rows], 0.0)
    for h in range(1, GROUP_HEADS):
        out = out + jnp.where(lane_head == h, z_all[h * rows:(h + 1) * rows], 0.0)
    return out


def _head_blockdiag_q(q, rows):
    lane_head = lax.broadcasted_iota(jnp.int32, (rows, GROUP_W), 1) // HEAD_DIM
    return jnp.concatenate([jnp.where(lane_head == h, q, jnp.zeros_like(q)) for h in range(GROUP_HEADS)], axis=0)


def _prompt_inproj_kernel(x_ref, exp_ref, w_ref, wt_ref, fb_ref, wst_ref, bs_ref, cw_ref, avg_ref, tri_ref,
                          ya_ref, fq_ref, fk_ref, fvt_ref, fkvt_ref, misc_ref, ccol_ref, crow_ref,
                          nq_ref, nkvt_ref, wint_ref, kse_ref, featt_ref, kcvc_ref, yd_ref, cst_ref,
                          carry_c, carry_z, *, tm):
    ti = pl.program_id(1)

    @pl.when(ti == 0)
    def _():
        carry_c[...] = jnp.zeros_like(carry_c)
        carry_z[...] = jnp.zeros_like(carry_z)

    xb = x_ref[0].astype(bf16)

    u = _gelu(_dot(xb, w_ref[:, AU:AU + GROUP_W]))
    vn = _head_layer_norm(_gelu(_dot(xb, w_ref[:, AV:AV + GROUP_W])), avg_ref[...])
    wst = wst_ref[...]
    bsf = bs_ref[...]
    for c in range(tm // CHUNK):
        r0 = c * CHUNK
        z_all = _dot(wst, vn[r0:r0 + CHUNK].astype(bf16))
        ya_ref[0, r0:r0 + CHUNK, :] = (u[r0:r0 + CHUNK] * (_head_diag(z_all, CHUNK) + bsf)).astype(bf16)

    fq_ref[0] = (_dot(xb, w_ref[:, BQ:BQ + GROUP_W]) * SCALE).astype(bf16)
    fk_ref[0] = _dot(xb, w_ref[:, BK:BK + GROUP_W]).astype(bf16)
    res_t = _dot_nt(wt_ref[...], xb)
    fkvt_ref[0] = res_t[0:2 * GROUP_W]
    fvt_ref[0] = res_t[GROUP_W:2 * GROUP_W].astype(bf16)
    nkvt_ref[0] = res_t[2 * GROUP_W:3 * GROUP_W]
    wint_ref[0] = res_t[3 * GROUP_W:3 * GROUP_W + 2 * HEAD_DIM]
    featt_ref[0] = jnp.concatenate([res_t[3 * GROUP_W - HEAD_DIM:3 * GROUP_W],
                                    res_t[3 * GROUP_W + HEAD_DIM:3 * GROUP_W + 2 * HEAD_DIM]], axis=0).astype(bf16)
    pre =_dot(xb, w_ref[:, MISC:MISC + LANES]) + fb_ref[0:1, :]
    lane = lax.broadcasted_iota(jnp.int32, (tm, LANES), 1)
    lf = jnp.where(lane < N_FOX_F, _log_sigmoid(pre), 0.0)
    misc_ref[0] = jnp.where(lane < N_FOX_F, lf, jnp.where(lane < N_FOX_F + N_GATES, jax.nn.sigmoid(pre), 0.0))
    c = _dot_exact_rhs(tri_ref[...], _split3(lf)) + carry_c[0:1, :]
    carry_c[0:1, :] = c[tm - 1:tm, :]
    ccol_ref[0] = c
    crow_ref[0] = jnp.transpose(c)[0:8, :]

    nq_ref[0] = (_dot(xb, w_ref[:, CQ:CQ + GROUP_W]) * SCALE).astype(bf16)
    ckv = _dot(xb, w_ref[:, CKV:CKV + 5 * HEAD_DIM])
    kcvc_ref[0] = ckv[:, 0:2 * HEAD_DIM]
    kse_ref[0] = jnp.concatenate([ckv[:, 2 * HEAD_DIM:3 * HEAD_DIM].astype(bf16),
                                  ckv[:, 4 * HEAD_DIM:5 * HEAD_DIM].astype(bf16), exp_ref[...]], axis=-1)

    z = _dot(xb, w_ref[:, DC:DC + GROUP_W]) * _dot(xb, w_ref[:, DH:DH + GROUP_W])
    row = lax.broadcasted_iota(jnp.int32, (tm, GROUP_W), 0)
    zp = carry_z[...]
    z1 = jnp.where(row == 0, zp[7:8, :], pltpu.roll(z, 1, 0))
    z2 = jnp.where(row == 0, zp[6:7, :], jnp.where(row == 1, zp[7:8, :], pltpu.roll(z, 2, 0)))
    cw = cw_ref[...]
    y_conv = cw[0:1, :] * z2 + cw[1:2, :] * z1 + cw[2:3, :] * z
    yd_ref[0] = (_dot(xb, w_ref[:, DB:DB + GROUP_W]) * y_conv).astype(bf16)
    carry_z[...] = z[tm - 8:tm, :]
    cst_ref[0] = z[tm - 8:tm, :]


def _prompt_inproj(x, expand_t, w_pack, w_t, fb_row, wst, bs_full, cw8, avg, tri, tm):
    bsz, t, d = x.shape
    nt = t // tm
    tok = lambda w, dt: (jax.ShapeDtypeStruct((bsz, t, w), dt), pl.BlockSpec((1, tm, w), lambda b, i: (b, i, 0)))
    feat = lambda r, dt: (jax.ShapeDtypeStruct((bsz, r, t), dt), pl.BlockSpec((1, r, tm), lambda b, i: (b, 0, i)))
    const = lambda a: pl.BlockSpec(a.shape, lambda b, i: (0,) * a.ndim)
    outs = [
        tok(GROUP_W, bf16),
        tok(GROUP_W, bf16),
        tok(GROUP_W, bf16),
        feat(GROUP_W, bf16),
        feat(2 * GROUP_W, f32),
        tok(LANES, f32),
        tok(LANES, f32),
        feat(8, f32),
        tok(GROUP_W, bf16),
        feat(4 * HEAD_DIM, f32),
        feat(2 * HEAD_DIM, f32),
        tok(GROUP_W, bf16),
        feat(2 * HEAD_DIM, bf16),
        tok(2 * HEAD_DIM, f32),
        tok(GROUP_W, bf16),
        (jax.ShapeDtypeStruct((bsz, 8, GROUP_W), f32), pl.BlockSpec((1, 8, GROUP_W), lambda b, i: (b, 0, 0))),
    ]
    consts = [w_pack, w_t, fb_row, wst, bs_full, cw8, avg, tri]
    return pl.pallas_call(
        functools.partial(_prompt_inproj_kernel, tm=tm),
        grid=(bsz, nt),
        in_specs=[pl.BlockSpec((1, tm, d), lambda b, i: (b, i, 0)), pl.BlockSpec((tm, LANES), lambda b, i: (i, 0))]
        + [const(a) for a in consts],
        out_specs=[o[1] for o in outs],
        out_shape=[o[0] for o in outs],
        scratch_shapes=[pltpu.VMEM((8, LANES), f32), pltpu.VMEM((8, GROUP_W), f32)],
        compiler_params=_cparams(("arbitrary", "arbitrary")),
    )(x, expand_t, *consts)


def _flash_update_t(chains):
    parts = []
    for s, v_t, m_sc, l_sc, acc_sc in chains:
        m_old = m_sc[...]
        m_new = jnp.maximum(m_old, jnp.max(s, axis=0, keepdims=True))
        alpha = jnp.exp(m_old - m_new)
        p = jnp.exp(s - m_new)
        l_sc[...] = alpha * l_sc[...] + jnp.sum(p, axis=0, keepdims=True)
        m_sc[...] = m_new
        parts.append((alpha, p.astype(bf16)))
    for (s, v_t, m_sc, l_sc, acc_sc), (alpha, pb) in zip(chains, parts):
        acc_sc[...] = alpha * acc_sc[...] + _dot(v_t, pb)


def _fox_prompt_kernel(q_ref, k_ref, vt_ref, ccol_ref, crow_ref, sel_ref, o_ref, crep_sc, m_sc, l_sc, acc_sc, *, tk, bq):
    qi = pl.program_id(1)
    n_pair = GROUP_HEADS // 2
    pw = 2 * HEAD_DIM

    @pl.when(qi == 0)
    def _():
        c3 = _split3(ccol_ref[0])
        for h in range(GROUP_HEADS):
            crep_sc[h] = _dot_exact_lhs(c3, sel_ref[h])

    q = q_ref[0]
    lane_hi = lax.broadcasted_iota(jnp.int32, (bq, pw), 1) >= HEAD_DIM
    cr = crow_ref[0]
    qpos = qi * bq + lax.broadcasted_iota(jnp.int32, (1, bq), 1)
    qpos = jnp.concatenate([qpos, qpos], axis=1)
    qbd = []
    cq = []
    for g in range(n_pair):
        qg = q[:, g * pw:(g + 1) * pw]
        qbd.append(jnp.concatenate([jnp.where(lane_hi, jnp.zeros_like(qg), qg),
                                    jnp.where(lane_hi, qg, jnp.zeros_like(qg))], axis=0))
        cq.append(jnp.concatenate([cr[2 * g:2 * g + 1, :], cr[2 * g + 1:2 * g + 2, :]], axis=1))
    m_sc[...] = jnp.full_like(m_sc, NEG)
    l_sc[...] = jnp.zeros_like(l_sc)
    acc_sc[...] = jnp.zeros_like(acc_sc)

    def tile(j, masked):
        k0 = pl.multiple_of(j * tk, tk)
        chains = []
        for g in range(n_pair):
            k = k_ref[0, pl.ds(k0, tk), g * pw:(g + 1) * pw]
            v_t = vt_ref[0, g * pw:(g + 1) * pw, pl.ds(k0, tk)]
            ck = jnp.concatenate([crep_sc[2 * g, pl.ds(k0, tk), :], crep_sc[2 * g + 1, pl.ds(k0, tk), :]], axis=1)
            s = _dot_nt(k, qbd[g]) + cq[g] - ck
            if masked:
                kpos = k0 + lax.broadcasted_iota(jnp.int32, (tk, 1), 0)
                s = jnp.where(kpos <= qpos, s, NEG)
            chains.append((s, v_t, m_sc.at[g], l_sc.at[g], acc_sc.at[g]))
        _flash_update_t(chains)

    n_full = (qi * bq) // tk

    def body(j, carry):
        tile(j, False)
        return carry

    lax.fori_loop(0, n_full, body, 0)
    tile(n_full, True)
    outs = []
    for g in range(n_pair):
        o = acc_sc[g] / l_sc[g]
        outs.append(o[0:HEAD_DIM, 0:bq])
        outs.append(o[HEAD_DIM:pw, bq:2 * bq])
    o_ref[0] = jnp.transpose(jnp.concatenate(outs, axis=0)).astype(bf16)


def _fox_prompt(fq, fk, fvt, ccol, crow, tk, bq):
    bsz, t, _ = fq.shape
    assert tk % bq == 0 and t % tk == 0 and bq == LANES
    sel = np.zeros((GROUP_HEADS, LANES, LANES), np.float32)
    for h in range(GROUP_HEADS):
        sel[h, h, :] = 1.0
    sel = jnp.asarray(sel, bf16)
    return pl.pallas_call(
        functools.partial(_fox_prompt_kernel, tk=tk, bq=bq),
        grid=(bsz, t // bq),
        in_specs=[
            pl.BlockSpec((1, bq, GROUP_W), lambda b, i: (b, i, 0)),
            pl.BlockSpec((1, t, GROUP_W), lambda b, i: (b, 0, 0)),
            pl.BlockSpec((1, GROUP_W, t), lambda b, i: (b, 0, 0)),
            pl.BlockSpec((1, t, LANES), lambda b, i: (b, 0, 0)),
            pl.BlockSpec((1, 8, bq), lambda b, i: (b, 0, i)),
            pl.BlockSpec(sel.shape, lambda b, i: (0, 0, 0)),
        ],
        out_specs=pl.BlockSpec((1, bq, GROUP_W), lambda b, i: (b, i, 0)),
        out_shape=jax.ShapeDtypeStruct((bsz, t, GROUP_W), bf16),
        scratch_shapes=[pltpu.VMEM((GROUP_HEADS, t, LANES), f32), pltpu.VMEM((2, 1, 2 * bq), f32),
                        pltpu.VMEM((2, 1, 2 * bq), f32), pltpu.VMEM((2, 2 * HEAD_DIM, 2 * bq), f32)],
        compiler_params=_cparams(("arbitrary", "arbitrary")),
    )(fq, fk, fvt, ccol, crow, sel)


def _compress_kernel(x_ref, pe_ref, w1_ref, w2_ref, o_ref, ot_ref):
    x = x_ref[0]
    n_grp = x.shape[0]
    top = _dot((x + pe_ref[0:1, :]).astype(bf16), w1_ref[0])
    bot = _dot((x + pe_ref[1:2, :]).astype(bf16), w1_ref[1])
    g = _gelu(top + pltpu.roll(bot, n_grp - 1, 0)).astype(bf16)
    w2 = w2_ref[...]
    kc = _dot(g[:, 0:CMP_HID], w2[:, 0:HEAD_DIM])
    vc = _dot(g[:, CMP_HID:2 * CMP_HID], w2[:, HEAD_DIM:2 * HEAD_DIM])
    cmp = jnp.concatenate([kc, vc], axis=-1)
    o_ref[0] = cmp.astype(bf16)
    ot_ref[0] = jnp.transpose(cmp).astype(bf16)


def _compress(x_groups, pe2, w1g, w2c):
    bsz, n_grp, w = x_groups.shape
    const = lambda a: pl.BlockSpec(a.shape, lambda b: (0,) * a.ndim)
    return pl.pallas_call(
        _compress_kernel,
        grid=(bsz,),
        in_specs=[pl.BlockSpec((1, n_grp, w), lambda b: (b, 0, 0)), const(pe2), const(w1g), const(w2c)],
        out_specs=[pl.BlockSpec((1, n_grp, 2 * HEAD_DIM), lambda b: (b, 0, 0)),
                   pl.BlockSpec((1, 2 * HEAD_DIM, n_grp), lambda b: (b, 0, 0))],
        out_shape=[jax.ShapeDtypeStruct((bsz, n_grp, 2 * HEAD_DIM), bf16),
                   jax.ShapeDtypeStruct((bsz, 2 * HEAD_DIM, n_grp), bf16)],
        compiler_params=_cparams(("arbitrary",)),
    )(x_groups, pe2, w1g, w2c)


def _stack_heads(q, rows):
    return jnp.concatenate([q[:, h * HEAD_DIM:(h + 1) * HEAD_DIM] for h in range(GROUP_HEADS)], axis=0)


def _unstack_heads(o, rows):
    return jnp.concatenate([o[h * rows:(h + 1) * rows] for h in range(GROUP_HEADS)], axis=-1)


def _gate_col(misc, branch):
    base = N_FOX_F + branch * GROUP_HEADS
    return jnp.concatenate([misc[:, base + h:base + h + 1] for h in range(GROUP_HEADS)], axis=0)


def _nsa_compressed(qs, cmp, qpos_s, selw, nq):
    n_c = cmp.shape[0]
    kc = cmp[:, 0:HEAD_DIM]
    vc = cmp[:, HEAD_DIM:2 * HEAD_DIM]
    s = _dot_nt(qs, kc)
    cend = lax.broadcasted_iota(jnp.int32, (1, n_c), 1) * CMP_STRIDE + (CMP_LEN - 1)
    vis = cend <= qpos_s
    s = jnp.where(vis, s, NEG)
    e = jnp.exp(s - jnp.max(s, axis=-1, keepdims=True))
    p = e / jnp.sum(e, axis=-1, keepdims=True)
    p = p * jnp.where(qpos_s >= CMP_LEN - 1, 1.0, 0.0)
    pb = p.astype(bf16)
    o_c = _dot(pb, vc)
    imp = _dot(pb[0:nq], selw)
    for h in range(1, GROUP_HEADS):
        imp = imp + _dot(pb[h * nq:(h + 1) * nq], selw)
    return o_c, imp


def _nsa_select(imp, qpos_row, n_sel_pad):
    nq = imp.shape[0]
    if nq < LANES:
        imp = jnp.concatenate([imp, jnp.zeros((LANES - nq, LANES), f32)], axis=0)
    sel_t = _nsa_select_t(jnp.transpose(imp)[0:n_sel_pad], qpos_row)
    if n_sel_pad < LANES:
        sel_t = jnp.concatenate([sel_t, jnp.zeros((LANES - n_sel_pad, LANES), f32)], axis=0)
    return jnp.transpose(sel_t)[0:nq]


def _nsa_select_t(imp_t, qpos_row):
    n_sel_pad = imp_t.shape[0]
    blk = lax.broadcasted_iota(jnp.int32, (n_sel_pad, LANES), 0)
    cur = qpos_row // SEL_LEN
    valid = blk * SEL_LEN <= qpos_row
    forced = (blk == 0) | (blk == cur) | (blk == cur - 1)
    val = jnp.where(valid, jnp.where(forced, BIG, imp_t), -BIG)
    n_g = n_sel_pad // 8
    grp = [val[8 * g:8 * g + 8] for g in range(n_g)]
    rank = [jnp.zeros((8, LANES), f32) for _ in range(n_g)]
    row8 = lax.broadcasted_iota(jnp.int32, (8, LANES), 0)
    for mp in range(n_sel_pad):
        vb = jnp.broadcast_to(val[mp:mp + 1, :], (8, LANES))
        for g in range(n_g):
            if 8 * g + 7 < mp:
                hit = jnp.where(vb > grp[g], 1.0, 0.0)
            elif 8 * g > mp:
                hit = jnp.where(vb >= grp[g], 1.0, 0.0)
            else:
                hit = jnp.where(row8 > mp - 8 * g, jnp.where(vb >= grp[g], 1.0, 0.0), jnp.where(vb > grp[g], 1.0, 0.0))
            rank[g] = rank[g] + hit
    rank = jnp.concatenate(rank, axis=0)
    return jnp.where(valid, jnp.where(rank < SEL_TOPK, 1.0, 0.0), 0.0)


def _nsa_prompt_kernel(q_ref, kse_ref, ft_ref, cmp_ref, cmpt_ref, misc_ref, selwt_ref, o_ref, m_sc, l_sc, acc_sc, *,
                       tk, t, n_sel_pad, bq):
    qi = pl.program_id(1)
    r_all = GROUP_HEADS * bq
    q_t4 = jnp.transpose(q_ref[0].astype(f32))
    q_t = jnp.concatenate([q_t4[h * HEAD_DIM:(h + 1) * HEAD_DIM] for h in range(GROUP_HEADS)], axis=1).astype(bf16)
    zq = jnp.zeros((HEAD_DIM, r_all), bf16)
    qpos1 = qi * bq + lax.broadcasted_iota(jnp.int32, (1, bq), 1)
    qpos = jnp.concatenate([qpos1] * GROUP_HEADS, axis=1)

    cmp = cmp_ref[0]
    n_c = cmp.shape[0]
    s = _dot(cmp, jnp.concatenate([q_t, zq], axis=0))
    cend = lax.broadcasted_iota(jnp.int32, (n_c, 1), 0) * CMP_STRIDE + (CMP_LEN - 1)
    s = jnp.where(cend <= qpos, s, NEG)
    e = jnp.exp(s - jnp.max(s, axis=0, keepdims=True))
    p = e / jnp.sum(e, axis=0, keepdims=True)
    pb = (p * jnp.where(qpos >= CMP_LEN - 1, 1.0, 0.0)).astype(bf16)
    o_c = _dot(cmpt_ref[0, HEAD_DIM:2 * HEAD_DIM, :], pb)
    selwt = selwt_ref[...]
    imp_t = _dot(selwt, pb[:, 0:bq])
    for h in range(1, GROUP_HEADS):
        imp_t = imp_t + _dot(selwt, pb[:, h * bq:(h + 1) * bq])
    sel_t = _nsa_select_t(imp_t[0:n_sel_pad], qpos1)
    unsel = jnp.where(sel_t > 0.5, 0.0, -(2.0 ** 100))
    if n_sel_pad < LANES:
        unsel = jnp.concatenate([unsel, jnp.zeros((LANES - n_sel_pad, bq), f32)], axis=0)
    unsel = jnp.concatenate([unsel.astype(bf16)] * GROUP_HEADS, axis=1)

    q_sel = jnp.concatenate([q_t, zq, unsel], axis=0)
    m_sc[...] = jnp.full_like(m_sc, NEG)
    l_sc[...] = jnp.zeros_like(l_sc)
    acc_sc[...] = jnp.zeros_like(acc_sc)

    n_chain = 2
    rc = r_all // n_chain

    def tile(j, diagonal):
        k0 = pl.multiple_of(j * tk, tk)
        kse = kse_ref[0, pl.ds(k0, tk), :]
        v_t = ft_ref[0, 0:HEAD_DIM, pl.ds(k0, tk)]
        chains = []
        for c in range(n_chain):
            s = _dot(kse, q_sel[:, c * rc:(c + 1) * rc])
            if diagonal:
                kpos = k0 + lax.broadcasted_iota(jnp.int32, (tk, 1), 0)
                s = jnp.where(kpos <= qpos[:, c * rc:(c + 1) * rc], s, NEG)
            chains.append((s, v_t, m_sc.at[c], l_sc.at[c], acc_sc.at[c]))
        _flash_update_t(chains)

    n_full = (qi * bq) // tk

    def body(j, carry):
        tile(j, False)
        return carry

    lax.fori_loop(0, n_full, body, 0)
    tile(n_full, True)
    o_s = jnp.concatenate([acc_sc[c] / l_sc[c] for c in range(n_chain)], axis=1)

    span = min(WINDOW + bq, t)
    w0 = pl.multiple_of(jnp.maximum(qi * bq - WINDOW, 0), LANES)
    q_win = jnp.concatenate([zq, q_t, jnp.zeros((LANES, r_all), bf16)], axis=0)
    s = _dot(kse_ref[0, pl.ds(w0, span), :], q_win)
    wpos = w0 + lax.broadcasted_iota(jnp.int32, (span, 1), 0)
    s = jnp.where((wpos <= qpos) & (wpos > qpos - WINDOW), s, NEG)
    e = jnp.exp(s - jnp.max(s, axis=0, keepdims=True))
    p = e / jnp.sum(e, axis=0, keepdims=True)
    o_w = _dot(ft_ref[0, HEAD_DIM:2 * HEAD_DIM, pl.ds(w0, span)], p.astype(bf16))

    misc_t = jnp.transpose(misc_ref[0])

    def gate(branch):
        base = N_FOX_F + branch * GROUP_HEADS
        return jnp.concatenate([misc_t[base + h:base + h + 1, :] for h in range(GROUP_HEADS)], axis=1)

    o = gate(0) * o_c + gate(1) * o_s + gate(2) * o_w
    o = jnp.concatenate([o[:, h * bq:(h + 1) * bq] for h in range(GROUP_HEADS)], axis=0)
    o_ref[0] = jnp.transpose(o).astype(bf16)


def _nsa_prompt(nq_b, kse, feat_t, cmp, cmp_t, misc, selw_t, tk, bq):
    bsz, t, _ = nq_b.shape
    r_all = GROUP_HEADS * bq
    n_c = cmp.shape[1]
    n_sel_pad = -(-(t // SEL_LEN) // 8) * 8
    assert tk % bq == 0 and bq == LANES
    return pl.pallas_call(
        functools.partial(_nsa_prompt_kernel, tk=tk, t=t, n_sel_pad=n_sel_pad, bq=bq),
        grid=(bsz, t // bq),
        in_specs=[
            pl.BlockSpec((1, bq, GROUP_W), lambda b, i: (b, i, 0)),
            pl.BlockSpec((1, t, GROUP_W), lambda b, i: (b, 0, 0)),
            pl.BlockSpec((1, 2 * HEAD_DIM, t), lambda b, i: (b, 0, 0)),
            pl.BlockSpec((1, n_c, 2 * HEAD_DIM), lambda b, i: (b, 0, 0)),
            pl.BlockSpec((1, 2 * HEAD_DIM, n_c), lambda b, i: (b, 0, 0)),
            pl.BlockSpec((1, bq, LANES), lambda b, i: (b, i, 0)),
            pl.BlockSpec(selw_t.shape, lambda b, i: (0, 0)),
        ],
        out_specs=pl.BlockSpec((1, bq, GROUP_W), lambda b, i: (b, i, 0)),
        out_shape=jax.ShapeDtypeStruct((bsz, t, GROUP_W), bf16),
        scratch_shapes=[pltpu.VMEM((2, 1, r_all // 2), f32), pltpu.VMEM((2, 1, r_all // 2), f32),
                        pltpu.VMEM((2, HEAD_DIM, r_all // 2), f32)],
        compiler_params=_cparams(("arbitrary", "arbitrary")),
    )(nq_b, kse, feat_t, cmp, cmp_t, misc, selw_t)


def _matmul_kernel(x_ref, w_ref, o_ref):
    o_ref[...] = _dot(x_ref[...].astype(bf16), w_ref[...])


def _sample_inproj(x2, w_pack, tm):
    n, d = x2.shape
    pw = w_pack.shape[1]
    return pl.pallas_call(
        _matmul_kernel,
        grid=(n // tm,),
        in_specs=[pl.BlockSpec((tm, d), lambda i: (i, 0)), pl.BlockSpec((d, pw), lambda i: (0, 0))],
        out_specs=pl.BlockSpec((tm, pw), lambda i: (i, 0)),
        out_shape=jax.ShapeDtypeStruct((n, pw), f32),
        compiler_params=_cparams(("arbitrary",)),
    )(x2, w_pack)


def _pad_to(a, rows):
    return jnp.concatenate([a, jnp.zeros((rows - a.shape[0], a.shape[1]), a.dtype)], axis=0)


def _softmax_tiles(tiles):
    m = tiles[0].max(axis=-1, keepdims=True)
    for s in tiles[1:]:
        m = jnp.maximum(m, s.max(axis=-1, keepdims=True))
    ps = [jnp.exp(s - m) for s in tiles]
    tot = ps[0].sum(axis=-1, keepdims=True)
    for p in ps[1:]:
        tot = tot + p.sum(axis=-1, keepdims=True)
    return ps, tot


def _sample_mixer_kernel(pt_ref, h_ref, *refs, n_pages, page, past, n_sel_pad):
    del pt_ref
    fox_pages = refs[0:n_pages]
    lf_pages = refs[n_pages:2 * n_pages]
    nsa_pages = refs[2 * n_pages:3 * n_pages]
    (win_ref, conv_ref, fb_ref, wst_ref, bs_ref, cw_ref, avg_ref, utri_ref, ones_ref, slt_ref, tri8_ref,
     pe_ref, w1_ref, w2_ref, selw_ref, exp_ref,
     y_ref, fkv_ref, misc_ref, nrow_ref, wout_ref, cout_ref, gv_ref, rows_sc) = refs[3 * n_pages:]
    tn = h_ref.shape[1]
    h = h_ref[0]

    u = _gelu(h[:, AU:AU + GROUP_W])
    vn = _head_layer_norm(_gelu(h[:, AV:AV + GROUP_W]), avg_ref[...])
    gv_ref[0] = vn
    w8 = jnp.concatenate([wst_ref[hh * CHUNK:hh * CHUNK + tn, :] for hh in range(GROUP_HEADS)], axis=0)
    z_all = _dot(w8, _pad_to(vn, CHUNK).astype(bf16))
    y_a = u * (_head_diag(z_all, tn) + bs_ref[0:tn, :])

    z = h[:, DC:DC + GROUP_W] * h[:, DH:DH + GROUP_W]
    zp = conv_ref[0, 0]
    row = lax.broadcasted_iota(jnp.int32, (tn, GROUP_W), 0)
    z1 = jnp.where(row == 0, zp[1:2, :], pltpu.roll(z, 1, 0))
    z2 = jnp.where(row == 0, zp[0:1, :], jnp.where(row == 1, zp[1:2, :], pltpu.roll(z, 2, 0)))
    cw = cw_ref[...]
    y_d = h[:, DB:DB + GROUP_W] * (cw[0:1, :] * z2 + cw[1:2, :] * z1 + cw[2:3, :] * z)
    cout_ref[0] = z

    pre = h[:, MISC:MISC + LANES] + fb_ref[0:1, :]
    lane = lax.broadcasted_iota(jnp.int32, (tn, LANES), 1)
    lf_new = jnp.where(lane < N_FOX_F, _log_sigmoid(pre), 0.0)
    misc = jnp.where(lane < N_FOX_F, lf_new, jnp.where(lane < N_FOX_F + N_GATES, jax.nn.sigmoid(pre), 0.0))
    misc_ref[0] = misc

    lf3 = _split3(jnp.concatenate([r[0, 0] for r in lf_pages], axis=0))
    tot = _dot_exact_lhs(lf3, ones_ref[...])
    off = _dot_exact_rhs(slt_ref[...], _split3(_pad_to(tot, LANES)))
    c_past = _dot_exact_lhs(lf3, utri_ref[...]) + off
    last = GROUP_HEADS * (n_pages - 1)
    total_past = off[last:last + GROUP_HEADS] + tot[last:last + GROUP_HEADS]
    c_new = jnp.transpose(_pad_to(total_past, LANES))[0:1, :] + _dot_exact_rhs(
        tri8_ref[...], _split3(_pad_to(lf_new, LANES)))
    cq = jnp.concatenate([c_new[:, hh:hh + 1] for hh in range(GROUP_HEADS)], axis=0)
    c_new_t = jnp.transpose(_pad_to(c_new, LANES))
    qpos = past + lax.broadcasted_iota(jnp.int32, (tn, 1), 0)
    qpos_s = jnp.concatenate([qpos] * GROUP_HEADS, axis=0)
    new_pos = past + lax.broadcasted_iota(jnp.int32, (1, LANES), 1)
    new_ok = new_pos <= qpos_s

    qbd = _head_blockdiag_q((h[:, BQ:BQ + GROUP_W] * SCALE).astype(bf16), tn)
    kv_new = h[:, BK:BK + 2 * GROUP_W]
    fkv_ref[0] = kv_new
    kvb_new = _pad_to(kv_new, LANES).astype(bf16)

    def head_rows(c4):
        return jnp.concatenate([jnp.broadcast_to(c4[hh:hh + 1, :], (tn, c4.shape[1])) for hh in range(GROUP_HEADS)],
                               axis=0)

    tiles = []
    for pg in range(n_pages):
        k_t = fox_pages[pg][0, 0, 0:GROUP_W, :].astype(bf16)
        tiles.append(_dot(qbd, k_t) + cq - head_rows(c_past[GROUP_HEADS * pg:GROUP_HEADS * (pg + 1)]))
    tiles.append(jnp.where(new_ok, _dot_nt(qbd, kvb_new[:, 0:GROUP_W]) + cq - head_rows(c_new_t[0:GROUP_HEADS]), NEG))
    ps, tot_p = _softmax_tiles(tiles)
    acc = _dot(ps[n_pages].astype(bf16), kvb_new[:, GROUP_W:2 * GROUP_W])
    for pg in range(n_pages):
        acc = acc + _dot_nt(ps[pg].astype(bf16), fox_pages[pg][0, 0, GROUP_W:2 * GROUP_W, :].astype(bf16))
    y_b = _head_diag(acc / tot_p, tn)

    ckv = h[:, CKV:CKV + 6 * HEAD_DIM]
    nrow_ref[0] = ckv[:, 0:4 * HEAD_DIM]
    for pg in range(n_pages):
        rows_sc[pg * page:(pg + 1) * page, :] = jnp.transpose(nsa_pages[pg][0, 0, 0:2 * HEAD_DIM, :])
    n_grp = n_pages * page // CMP_STRIDE
    half = CMP_LEN // 2
    top = None
    bot = None
    for r in range(half):
        a_r = rows_sc[pl.ds(r, n_grp, stride=CMP_STRIDE), :]
        t_r = _dot((a_r + pe_ref[r:r + 1, :]).astype(bf16), w1_ref[r])
        b_r = _dot((a_r + pe_ref[half + r:half + r + 1, :]).astype(bf16), w1_ref[half + r])
        top = t_r if top is None else top + t_r
        bot = b_r if bot is None else bot + b_r
    g = _gelu(top + pltpu.roll(bot, n_grp - 1, 0)).astype(bf16)
    w2 = w2_ref[...]
    cmp = jnp.concatenate([_dot(g[:, 0:CMP_HID], w2[:, 0:HEAD_DIM]),
                           _dot(g[:, CMP_HID:2 * CMP_HID], w2[:, HEAD_DIM:2 * HEAD_DIM])], axis=-1).astype(bf16)
    qs = _stack_heads((h[:, CQ:CQ + GROUP_W] * SCALE).astype(bf16), tn)
    o_c, imp = _nsa_compressed(qs, cmp, qpos_s, selw_ref[...], tn)
    sel = _nsa_select(imp, new_pos, n_sel_pad).astype(bf16)
    sel_new = _pad_to(ckv[:, 2 * HEAD_DIM:4 * HEAD_DIM], LANES).astype(bf16)
    tiles = []
    for pg in range(n_pages + 1):
        picked = _dot(sel, exp_ref[:, pg * page:(pg + 1) * page])
        if pg == n_pages:
            picked = jnp.where(new_pos <= qpos, picked, 0.0)
            sc = _dot_nt(qs, sel_new[:, 0:HEAD_DIM])
        else:
            sc = _dot(qs, nsa_pages[pg][0, 0, 2 * HEAD_DIM:3 * HEAD_DIM, :].astype(bf16))
        keep = jnp.concatenate([picked] * GROUP_HEADS, axis=0) > 0.5
        tiles.append(jnp.where(keep, sc, NEG))
    ps, tot_p = _softmax_tiles(tiles)
    acc = _dot(ps[n_pages].astype(bf16), sel_new[:, HEAD_DIM:2 * HEAD_DIM])
    for pg in range(n_pages):
        acc = acc + _dot_nt(ps[pg].astype(bf16), nsa_pages[pg][0, 0, 3 * HEAD_DIM:4 * HEAD_DIM, :].astype(bf16))
    o_s = acc / tot_p

    win_old = win_ref[0, 0]
    win_new = ckv[:, 4 * HEAD_DIM:6 * HEAD_DIM]
    new_t = pltpu.roll(jnp.transpose(_pad_to(win_new, LANES)), LANES - tn, 1)
    new_wide = jnp.concatenate([jnp.zeros((2 * HEAD_DIM, WINDOW - LANES), f32), new_t], axis=1)
    w_idx = lax.broadcasted_iota(jnp.int32, (1, WINDOW), 1)
    wout_ref[0] = jnp.where(w_idx >= WINDOW - tn, new_wide, pltpu.roll(win_old, WINDOW - tn, 1))
    wob = win_old.astype(bf16)
    wnb = _pad_to(win_new, LANES).astype(bf16)
    s_old = jnp.where(w_idx > qpos_s - past, _dot(qs, wob[0:HEAD_DIM, :]), NEG)
    s_new = jnp.where(new_ok, _dot_nt(qs, wnb[:, 0:HEAD_DIM]), NEG)
    ps, tot_p = _softmax_tiles([s_old, s_new])
    o_w = (_dot_nt(ps[0].astype(bf16), wob[HEAD_DIM:2 * HEAD_DIM, :])
           + _dot(ps[1].astype(bf16), wnb[:, HEAD_DIM:2 * HEAD_DIM])) / tot_p

    o = _gate_col(misc, 0) * o_c + _gate_col(misc, 1) * o_s + _gate_col(misc, 2) * o_w
    y_c = _unstack_heads(o, tn)
    y_ref[0] = jnp.concatenate([y_a, y_b, y_c, y_d], axis=-1)


def _sample_mixer(h3, fox_t, lf_t, nsa_t, win_t, conv4, page_table, layer, consts):
    bsz, tn, pw = h3.shape
    n_pages = page_table.shape[1]
    page = fox_t.shape[3]
    past = n_pages * page
    n_sel_pad = -(-(-(-(past + tn) // SEL_LEN)) // 8) * 8

    def page_spec(arr, pg):
        blk = (1, 1) + arr.shape[2:]
        return pl.BlockSpec(blk, lambda b, pt: (layer, pt[b, pg], 0, 0))

    in_specs = [pl.BlockSpec((1, tn, pw), lambda b, pt: (b, 0, 0))]
    operands = [h3]
    for arr in (fox_t, lf_t, nsa_t):
        for pg in range(n_pages):
            in_specs.append(page_spec(arr, pg))
            operands.append(arr)
    in_specs.append(pl.BlockSpec((1, 1) + win_t.shape[2:], lambda b, pt: (layer, b, 0, 0)))
    operands.append(win_t)
    in_specs.append(pl.BlockSpec((1, 1) + conv4.shape[2:], lambda b, pt: (layer, b, 0, 0)))
    operands.append(conv4)
    for a in consts:
        in_specs.append(pl.BlockSpec(a.shape, lambda b, pt, nd=a.ndim: (0,) * nd))
        operands.append(a)
    out_dims = [(tn, 4 * GROUP_W), (tn, 2 * GROUP_W), (tn, LANES), (tn, 4 * HEAD_DIM), (2 * HEAD_DIM, WINDOW),
                (tn, GROUP_W), (tn, GROUP_W)]
    out_shape = [jax.ShapeDtypeStruct((bsz, r, w), f32) for r, w in out_dims]
    out_specs = [pl.BlockSpec((1, r, w), lambda b, pt: (b, 0, 0)) for r, w in out_dims]
    return pl.pallas_call(
        functools.partial(_sample_mixer_kernel, n_pages=n_pages, page=page, past=past, n_sel_pad=n_sel_pad),
        grid_spec=pltpu.PrefetchScalarGridSpec(num_scalar_prefetch=1, grid=(bsz,), in_specs=in_specs,
                                               out_specs=out_specs,
                                               scratch_shapes=[pltpu.VMEM((past, 2 * HEAD_DIM), f32)]),
        out_shape=out_shape,
        compiler_params=_cparams(("arbitrary",)),
    )(page_table, *operands)


def _outproj_kernel(x_ref, *refs, alpha):
    y_refs = refs[:-4]
    w_ref, g_ref, b_ref, o_ref = refs[-4:]
    acc = None
    c0 = 0
    for y_ref in y_refs:
        wy = y_ref.shape[1]
        part = _dot(y_ref[...].astype(bf16), w_ref[c0:c0 + wy, :])
        acc = part if acc is None else acc + part
        c0 += wy
    o_ref[...] = _layer_norm_rows(alpha * x_ref[...] + acc, g_ref[0:1, :], b_ref[0:1, :])


def _outproj(x2, ys, w_out_b, g8, b8, alpha, tm):
    n, d = x2.shape
    assert sum(y.shape[1] for y in ys) == d
    const = lambda a: pl.BlockSpec(a.shape, lambda i: (0,) * a.ndim)
    return pl.pallas_call(
        functools.partial(_outproj_kernel, alpha=alpha),
        grid=(n // tm,),
        in_specs=[pl.BlockSpec((tm, d), lambda i: (i, 0))] + [pl.BlockSpec((tm, y.shape[1]), lambda i: (i, 0)) for y in ys]
        + [const(w_out_b), const(g8), const(b8)],
        out_specs=pl.BlockSpec((tm, d), lambda i: (i, 0)),
        out_shape=jax.ShapeDtypeStruct((n, d), f32),
        compiler_params=_cparams(("arbitrary",)),
    )(x2, *ys, w_out_b, g8, b8)


def _route(xb, wr_t, tm):
    logits = _dot_nt(wr_t, xb)
    e = jnp.exp(logits - jnp.max(logits, axis=0, keepdims=True))
    probs = e / jnp.sum(e, axis=0, keepdims=True)
    p = [probs[i:i + 1, :] for i in range(N_EXPERTS)]
    one = jnp.ones((1, tm), f32)
    zero = jnp.zeros((1, tm), f32)
    in_top = []
    gscore = []
    for g in range(N_GROUPS):
        members = range(g * EXP_PER_GROUP, (g + 1) * EXP_PER_GROUP)
        score = zero
        for i in members:
            rank = zero
            for j in members:
                if j < i:
                    rank = rank + jnp.where(p[j] >= p[i], one, zero)
                elif j > i:
                    rank = rank + jnp.where(p[j] > p[i], one, zero)
            keep = jnp.where(rank < TOP_K, one, zero)
            in_top.append(keep)
            score = score + keep * p[i]
        gscore.append(score)
    gate_rows = []
    wsum = zero
    chosen = []
    for g in range(N_GROUPS):
        win = one
        for g2 in range(N_GROUPS):
            if g2 < g:
                win = win * jnp.where(gscore[g] > gscore[g2], one, zero)
            elif g2 > g:
                win = win * jnp.where(gscore[g] >= gscore[g2], one, zero)
        for i in range(g * EXP_PER_GROUP, (g + 1) * EXP_PER_GROUP):
            c = in_top[i] * win
            chosen.append(c)
            wsum = wsum + c * p[i]
    for i in range(N_EXPERTS):
        gate_rows.append(jnp.where(chosen[i] > 0.5, p[i] / wsum, zero))
    gate_t = jnp.concatenate(gate_rows + [jnp.zeros((LANES - N_EXPERTS, tm), f32)], axis=0)
    return jnp.concatenate([jnp.transpose(gate_t[:, c * LANES:(c + 1) * LANES]) for c in range(tm // LANES)], axis=0)


def _moe_kernel(x_ref, wr_ref, wg_ref, wu_ref, wd_ref, g_ref, b_ref, o_ref, gate_sc, acc_sc, xb_sc, *, alpha, tm):
    e = pl.program_id(1)

    @pl.when(e == 0)
    def _():
        xb = x_ref[...].astype(bf16)
        xb_sc[...] = xb
        gate_sc[...] = _route(xb, wr_ref[...], tm)
        acc_sc[...] = jnp.zeros_like(acc_sc)

    xb = xb_sc[...]
    lane = lax.broadcasted_iota(jnp.int32, (tm, LANES), 1)
    gcol = jnp.sum(jnp.where(lane == e, gate_sc[...], 0.0), axis=-1, keepdims=True)
    h = jax.nn.silu(_dot(xb, wg_ref[0])) * _dot(xb, wu_ref[0])
    acc_sc[...] += _dot((h * gcol).astype(bf16), wd_ref[0])

    @pl.when(e == N_EXPERTS - 1)
    def _():
        o_ref[...] = _layer_norm_rows(alpha * x_ref[...] + acc_sc[...], g_ref[0:1, :], b_ref[0:1, :])


def _moe(x2, wr_t, wg, wu, wd, g8, b8, alpha, tm):
    n, d = x2.shape
    de = wg.shape[-1]
    const = lambda a: pl.BlockSpec(a.shape, lambda i, e: (0,) * a.ndim)
    return pl.pallas_call(
        functools.partial(_moe_kernel, alpha=alpha, tm=tm),
        grid=(n // tm, N_EXPERTS),
        in_specs=[
            pl.BlockSpec((tm, d), lambda i, e: (i, 0)), const(wr_t),
            pl.BlockSpec((1, d, de), lambda i, e: (e, 0, 0)), pl.BlockSpec((1, d, de), lambda i, e: (e, 0, 0)),
            pl.BlockSpec((1, de, d), lambda i, e: (e, 0, 0)), const(g8), const(b8),
        ],
        out_specs=pl.BlockSpec((tm, d), lambda i, e: (i, 0)),
        out_shape=jax.ShapeDtypeStruct((n, d), f32),
        scratch_shapes=[pltpu.VMEM((tm, LANES), f32), pltpu.VMEM((tm, d), f32), pltpu.VMEM((tm, d), bf16)],
        compiler_params=_cparams(("arbitrary", "arbitrary")),
    )(x2, wr_t, wg, wu, wd, g8, b8)


def _pad_rows(a, rows=8):
    return jnp.concatenate([a, jnp.zeros((rows - a.shape[0],) + a.shape[1:], a.dtype)], axis=0)


def _prep_layer(l, w_in, fox_b_f, gmlp_w_s, gmlp_b_s, nsa_cmp_pe, nsa_cmp_w1, nsa_cmp_w2, conv_w, w_out, ln_g, ln_b,
                w_exp_gate, w_exp_up, w_exp_down):
    w = w_in[l]
    d = w.shape[0]
    w_pack = jnp.concatenate([w[:, 0:1280], w[:, 1284:1924], w[:, 1936:2704], w[:, 1280:1284], w[:, 1924:1936],
                              jnp.zeros((d, LANES - N_FOX_F - N_GATES), w.dtype)], axis=1).astype(bf16)
    fb_row = _pad_rows(jnp.concatenate([fox_b_f[l], jnp.zeros((LANES - N_FOX_F,), f32)])[None, :])
    tril = jnp.asarray(np.tril(np.ones((CHUNK, CHUNK), np.float32)))
    wst = (gmlp_w_s[l] * tril).reshape(GROUP_HEADS * CHUNK, CHUNK).astype(bf16)
    bs_full = jnp.repeat(gmlp_b_s[l].T, HEAD_DIM, axis=1)
    cw8 = _pad_rows(conv_w[l])
    half = CMP_LEN // 2
    w1r = nsa_cmp_w1[l].reshape(2, CMP_LEN, HEAD_DIM, CMP_HID)
    per = nsa_cmp_pe[l]
    w2c = jnp.concatenate([nsa_cmp_w2[l, 0], nsa_cmp_w2[l, 1]], axis=1).astype(bf16)
    zed = jnp.zeros((CMP_LEN, HEAD_DIM, CMP_HID), f32)
    w1r = jnp.concatenate([jnp.concatenate([w1r[0], zed], axis=2), jnp.concatenate([zed, w1r[1]], axis=2)],
                          axis=1).astype(bf16)
    pe32 = jnp.concatenate([per[0], per[1]], axis=1)
    w1g = w1r.reshape(2, half * 2 * HEAD_DIM, 2 * CMP_HID)
    pe2 = _pad_rows(pe32.reshape(2, half * 2 * HEAD_DIM))
    w_t = jnp.concatenate([w_pack[:, BK:BK + 2 * GROUP_W], w_pack[:, CKV:CKV + 6 * HEAD_DIM]], axis=1).T
    return dict(
        w_pack=w_pack, w_t=w_t, fb_row=fb_row, wst=wst, bs_full=bs_full, cw8=cw8, w1g=w1g, pe2=pe2, w2c=w2c, w1r=w1r,
        pe32=pe32,
        w_out=w_out[l].astype(bf16), g1=_pad_rows(ln_g[l, 0:1]), b1=_pad_rows(ln_b[l, 0:1]),
        g2=_pad_rows(ln_g[l, 1:2]), b2=_pad_rows(ln_b[l, 1:2]),
        wg=w_exp_gate[l].astype(bf16), wu=w_exp_up[l].astype(bf16), wd=w_exp_down[l].astype(bf16),
    )


def _sel_overlap(t, n_sel):
    n_cmp = (t - CMP_LEN) // CMP_STRIDE + 1
    start = np.arange(n_cmp) * CMP_STRIDE
    sstart = np.arange(n_sel) * SEL_LEN
    ov = np.minimum(start[:, None] + CMP_LEN, sstart[None, :] + SEL_LEN) - np.maximum(start[:, None], sstart[None, :])
    return np.clip(ov, 0, None) / CMP_STRIDE


def _head_avg():
    a = np.kron(np.eye(GROUP_HEADS, dtype=np.float32), np.full((HEAD_DIM, HEAD_DIM), 1.0 / HEAD_DIM, np.float32))
    return jnp.asarray(a, bf16)


def _post_block(x2, ys, p, wr_t, alpha, tm):
    x1 = _outproj(x2, ys, p['w_out'], p['g1'], p['b1'], alpha, tm)
    return _moe(x1, wr_t, p['wg'], p['wu'], p['wd'], p['g2'], p['b2'], alpha, tm)


def _prompt_layer(x, p, wr_t, alpha, tm=512, tk=512, bq=128):
    bsz, t, d = x.shape
    n = bsz * t
    n_sel = t // SEL_LEN
    n_grp = t // CMP_STRIDE
    assert t % tm == 0 and t % tk == 0 and n_sel <= LANES and t >= WINDOW + bq
    tri = jnp.asarray(np.tril(np.ones((tm, tm), np.float32)), bf16)
    expand_t = np.zeros((t, LANES), np.float32)
    expand_t[np.arange(t), np.arange(t) // SEL_LEN] = 1.0
    (ya, fq, fk, fvt, fkvt, misc, ccol, crow, nq, nkvt, wint, kse, featt, kcvc, yd, cst) = _prompt_inproj(
        x, jnp.asarray(expand_t, bf16), p['w_pack'], p['w_t'], p['fb_row'], p['wst'], p['bs_full'], p['cw8'],
        _head_avg(), tri, tm)
    yb = _fox_prompt(fq, fk, fvt, ccol, crow, tk, bq)
    cmp, cmp_t = _compress(kcvc.reshape(bsz, n_grp, CMP_STRIDE * 2 * HEAD_DIM), p['pe2'], p['w1g'], p['w2c'])
    ov = _sel_overlap(t, n_sel)
    selw_t = np.zeros((LANES, n_grp), np.float32)
    selw_t[:n_sel, :ov.shape[0]] = ov.T
    yc = _nsa_prompt(nq, kse, featt, cmp, cmp_t, misc, jnp.asarray(selw_t, bf16), tk, bq)
    ys = [a.reshape(n, GROUP_W) for a in (ya, yb, yc, yd)]
    x_out = _post_block(x.reshape(n, d), ys, p, wr_t, alpha, tm).reshape(bsz, t, d)
    win_keep = min(WINDOW, t)
    states = (fkvt.reshape(bsz, 2, GROUP_HEADS, HEAD_DIM, t).transpose(0, 4, 1, 2, 3), misc[:, :, 0:N_FOX_F],
              nkvt.reshape(bsz, 4, HEAD_DIM, t).transpose(0, 3, 1, 2),
              wint[:, :, t - win_keep:].reshape(bsz, 2, HEAD_DIM, win_keep).transpose(0, 3, 1, 2),
              cst[:, 8 - (CONV_W - 1):])
    return x_out, states, (ya, yb, yc, yd)


def _sample_consts(p, n_pages, page, tn):
    utri = np.triu(np.ones((page, page), np.float32))
    ones = np.ones((page, page), np.float32)
    r = np.arange(GROUP_HEADS * n_pages)
    slt = np.zeros((GROUP_HEADS * n_pages, LANES), np.float32)
    slt[:, :r.size] = ((r[:, None] % GROUP_HEADS) == (r[None, :] % GROUP_HEADS)) & (r[None, :] // GROUP_HEADS < r[:, None] // GROUP_HEADS)
    tri8 = np.zeros((tn, LANES), np.float32)
    tri8[:, :tn] = np.tril(np.ones((tn, tn), np.float32))
    t_all = n_pages * page + tn
    n_sel = -(-t_all // SEL_LEN)
    ov = _sel_overlap(t_all, n_sel)
    n_grp = n_pages * page // CMP_STRIDE
    assert ov.shape[0] <= n_grp and n_sel <= LANES
    selw = np.zeros((n_grp, LANES), np.float32)
    selw[:ov.shape[0], :n_sel] = ov
    n_keys = (n_pages + 1) * page
    expand = np.zeros((LANES, n_keys), np.float32)
    expand[np.arange(n_keys) // SEL_LEN, np.arange(n_keys)] = 1.0
    as_b = lambda a: jnp.asarray(np.asarray(a, np.float32), bf16)
    return [p['fb_row'], p['wst'], p['bs_full'], p['cw8'], _head_avg(), as_b(utri), as_b(ones), as_b(slt),
            as_b(tri8), p['pe32'], p['w1r'], p['w2c'], as_b(selw), as_b(expand)]


def _sample_layer(x, views, page_table, layer, p, wr_t, alpha, tm=512):
    bsz, tn, d = x.shape
    n = bsz * tn
    fox_t, lf_t, nsa_t, win_t, conv4 = views
    n_pages = page_table.shape[1]
    page = fox_t.shape[3]
    assert tn == 8 and page == LANES and win_t.shape[3] == WINDOW and n_pages * page >= WINDOW
    assert GROUP_HEADS * n_pages <= LANES
    h = _sample_inproj(x.reshape(n, d), p['w_pack'], tm).reshape(bsz, tn, P_PACK)
    y, fkv, misc, nrow, wout, cout, gv = _sample_mixer(h, fox_t, lf_t, nsa_t, win_t, conv4, page_table, layer,
                                                       _sample_consts(p, n_pages, page, tn))
    x_out = _post_block(x.reshape(n, d), [y.reshape(n, d)], p, wr_t, alpha, tm).reshape(bsz, tn, d)
    win_state = wout.reshape(bsz, 2, HEAD_DIM, WINDOW).transpose(0, 3, 1, 2)
    states = (fkv.reshape(bsz, tn, 2, GROUP_HEADS, HEAD_DIM), misc[:, :, 0:N_FOX_F], nrow.reshape(bsz, tn, 4, HEAD_DIM),
              win_state, cout[:, tn - (CONV_W - 1):], gv.reshape(bsz, tn, GROUP_HEADS, HEAD_DIM))
    return x_out, states


def kernel(x_prompt, x_sample, cache_fox_kv, cache_fox_logf, cache_nsa_kv, state_nsa_win, state_conv, page_table,
           w_in, fox_b_f, gmlp_w_s, gmlp_b_s, nsa_cmp_pe, nsa_cmp_w1, nsa_cmp_w2, conv_w, w_out, ln_g, ln_b,
           w_router, w_exp_gate, w_exp_up, w_exp_down):
    depth, n_pool, page = cache_fox_kv.shape[0:3]
    dec_b = x_sample.shape[0]
    alpha = (2 * depth) ** 0.25
    wr_t = w_router.T.astype(bf16)
    views = (cache_fox_kv.transpose(0, 1, 3, 4, 5, 2).reshape(depth, n_pool, 2 * GROUP_W, page),
             cache_fox_logf.transpose(0, 1, 3, 2),
             cache_nsa_kv.transpose(0, 1, 3, 4, 2).reshape(depth, n_pool, 4 * HEAD_DIM, page),
             state_nsa_win.transpose(0, 1, 3, 4, 2).reshape(depth, dec_b, 2 * HEAD_DIM, state_nsa_win.shape[2]),
             state_conv)
    xp, xs = x_prompt, x_sample
    st_p = [[] for _ in range(5)]
    st_s = [[] for _ in range(6)]
    for l in range(depth):
        p = _prep_layer(l, w_in, fox_b_f, gmlp_w_s, gmlp_b_s, nsa_cmp_pe, nsa_cmp_w1, nsa_cmp_w2, conv_w, w_out,
                        ln_g, ln_b, w_exp_gate, w_exp_up, w_exp_down)
        xp, sp, _ = _prompt_layer(xp, p, wr_t, alpha)
        xs, ss = _sample_layer(xs, views, page_table, l, p, wr_t, alpha)
        for lst, a in zip(st_p, sp):
            lst.append(a)
        for lst, a in zip(st_s, ss):
            lst.append(a)
    fox_kv_p, fox_lf_p, nsa_kv_p, nsa_win_p, conv_p = [jnp.stack(a, axis=0) for a in st_p]
    fox_kv_s, fox_lf_s, nsa_kv_s, nsa_win_s, conv_s, gmlp_v_s = [jnp.stack(a, axis=0) for a in st_s]
    return (xp, xs, fox_kv_p, fox_kv_s, fox_lf_p, fox_lf_s, nsa_kv_p, nsa_kv_s, nsa_win_p, nsa_win_s, conv_p, conv_s,
            gmlp_v_s)
```

```python
import functools

import numpy as np
import jax
import jax.numpy as jnp
from jax import lax
from jax.experimental import pallas as pl
from jax.experimental.pallas import tpu as pltpu

f32 = jnp.float32
bf16 = jnp.bfloat16

HEAD_DIM = 64
GROUP_W = 256
GROUP_HEADS = 4
CHUNK = 128
Q_BLOCK = 128
CMP_LEN = 32
CMP_STRIDE = 16
CMP_HID = 128
SEL_LEN = 64
SEL_TOPK = 16
WINDOW = 512
CONV_W = 3
N_EXPERTS = 16
N_GROUPS = 4
EXP_PER_GROUP = 4
TOP_K = 2
LN_EPS = 1e-5
SCALE = HEAD_DIM ** -0.5
NEG = -1e30
BIG = 1e30
LANES = 128

AU, AV, BQ, BK, BV, CQ, CKV, DH, DB, DC, MISC = 0, 256, 512, 768, 1024, 1280, 1536, 1920, 2176, 2432, 2688
P_PACK = 2816
N_FOX_F = GROUP_HEADS
N_GATES = 3 * GROUP_HEADS

VMEM_LIMIT = 56 * 1024 * 1024


def _cparams(sem):
    return pltpu.CompilerParams(dimension_semantics=sem, vmem_limit_bytes=VMEM_LIMIT)


def _dot(a, b):
    return jnp.dot(a, b, preferred_element_type=f32)


def _dot_nt(a, b):
    return lax.dot_general(a, b, (((1,), (1,)), ((), ())), preferred_element_type=f32)


def _split3(x):
    hi = x.astype(bf16)
    r = x - hi.astype(f32)
    mid = r.astype(bf16)
    lo = (r - mid.astype(f32)).astype(bf16)
    return hi, mid, lo


def _dot_exact_rhs(a, b3):
    return _dot(a, b3[0]) + _dot(a, b3[1]) + _dot(a, b3[2])


def _dot_exact_lhs(a3, b):
    return _dot(a3[0], b) + _dot(a3[1], b) + _dot(a3[2], b)


def _log_sigmoid(x):
    return jnp.minimum(x, 0.0) - jnp.log1p(jnp.exp(-jnp.abs(x)))


def _gelu(x):
    return jax.nn.gelu(x, approximate=True)


def _layer_norm_rows(x, g, b):
    mu = jnp.mean(x, axis=-1, keepdims=True)
    d = x - mu
    var = jnp.mean(d * d, axis=-1, keepdims=True)
    return d * lax.rsqrt(var + LN_EPS) * g + b


def _head_layer_norm(g, avg):
    h2 = g.astype(bf16)
    mu = _dot(h2, avg) + _dot((g - h2.astype(f32)).astype(bf16), avg)
    d = g - mu
    dd = d * d
    d2 = dd.astype(bf16)
    var = _dot(d2, avg) + _dot((dd - d2.astype(f32)).astype(bf16), avg)
    return d * lax.rsqrt(var + LN_EPS)


def _head_diag(z_all, rows):
    lane_head = lax.broadcasted_iota(jnp.int32, (rows, GROUP_W), 1) // HEAD_DIM
    out = jnp.where(lane_head == 0, z_all[0:rows], 0.0)
    for h in range(1, GROUP_HEADS):
        out = out + jnp.where(lane_head == h, z_all[h * rows:(h + 1) * rows], 0.0)
    return out


def _head_blockdiag_q(q, rows):
    lane_head = lax.broadcasted_iota(jnp.int32, (rows, GROUP_W), 1) // HEAD_DIM
    return jnp.concatenate([jnp.where(lane_head == h, q, jnp.zeros_like(q)) for h in range(GROUP_HEADS)], axis=0)


def _prompt_inproj_kernel(x_ref, exp_ref, w_ref, wt_ref, fb_ref, wst_ref, bs_ref, cw_ref, avg_ref, tri_ref,
                          ya_ref, fq_ref, fk_ref, fvt_ref, fkvt_ref, misc_ref, ccol_ref, crow_ref,
                          nq_ref, nkvt_ref, wint_ref, kse_ref, featt_ref, kcvc_ref, yd_ref, cst_ref,
                          carry_c, carry_z, *, tm):
    ti = pl.program_id(1)

    @pl.when(ti == 0)
    def _():
        carry_c[...] = jnp.zeros_like(carry_c)
        carry_z[...] = jnp.zeros_like(carry_z)

    xb = x_ref[0].astype(bf16)

    u = _gelu(_dot(xb, w_ref[:, AU:AU + GROUP_W]))
    vn = _head_layer_norm(_gelu(_dot(xb, w_ref[:, AV:AV + GROUP_W])), avg_ref[...])
    wst = wst_ref[...]
    bsf = bs_ref[...]
    for c in range(tm // CHUNK):
        r0 = c * CHUNK
        z_all = _dot(wst, vn[r0:r0 + CHUNK].astype(bf16))
        ya_ref[0, r0:r0 + CHUNK, :] = (u[r0:r0 + CHUNK] * (_head_diag(z_all, CHUNK) + bsf)).astype(bf16)

    fq_ref[0] = (_dot(xb, w_ref[:, BQ:BQ + GROUP_W]) * SCALE).astype(bf16)
    fk_ref[0] = _dot(xb, w_ref[:, BK:BK + GROUP_W]).astype(bf16)
    res_t = _dot_nt(wt_ref[...], xb)
    fkvt_ref[0] = res_t[0:2 * GROUP_W]
    fvt_ref[0] = res_t[GROUP_W:2 * GROUP_W].astype(bf16)
    nkvt_ref[0] = res_t[2 * GROUP_W:3 * GROUP_W]
    wint_ref[0] = res_t[3 * GROUP_W:3 * GROUP_W + 2 * HEAD_DIM]
    featt_ref[0] = jnp.concatenate([res_t[3 * GROUP_W - HEAD_DIM:3 * GROUP_W],
                                    res_t[3 * GROUP_W + HEAD_DIM:3 * GROUP_W + 2 * HEAD_DIM]], axis=0).astype(bf16)
    pre =_dot(xb, w_ref[:, MISC:MISC + LANES]) + fb_ref[0:1, :]
    lane = lax.broadcasted_iota(jnp.int32, (tm, LANES), 1)
    lf = jnp.where(lane < N_FOX_F, _log_sigmoid(pre), 0.0)
    misc_ref[0] = jnp.where(lane < N_FOX_F, lf, jnp.where(lane < N_FOX_F + N_GATES, jax.nn.sigmoid(pre), 0.0))
    c = _dot_exact_rhs(tri_ref[...], _split3(lf)) + carry_c[0:1, :]
    carry_c[0:1, :] = c[tm - 1:tm, :]
    ccol_ref[0] = c
    crow_ref[0] = jnp.transpose(c)[0:8, :]

    nq_ref[0] = (_dot(xb, w_ref[:, CQ:CQ + GROUP_W]) * SCALE).astype(bf16)
    ckv = _dot(xb, w_ref[:, CKV:CKV + 5 * HEAD_DIM])
    kcvc_ref[0] = ckv[:, 0:2 * HEAD_DIM]
    kse_ref[0] = jnp.concatenate([ckv[:, 2 * HEAD_DIM:3 * HEAD_DIM].astype(bf16),
                                  ckv[:, 4 * HEAD_DIM:5 * HEAD_DIM].astype(bf16), exp_ref[...]], axis=-1)

    z = _dot(xb, w_ref[:, DC:DC + GROUP_W]) * _dot(xb, w_ref[:, DH:DH + GROUP_W])
    row = lax.broadcasted_iota(jnp.int32, (tm, GROUP_W), 0)
    zp = carry_z[...]
    z1 = jnp.where(row == 0, zp[7:8, :], pltpu.roll(z, 1, 0))
    z2 = jnp.where(row == 0, zp[6:7, :], jnp.where(row == 1, zp[7:8, :], pltpu.roll(z, 2, 0)))
    cw = cw_ref[...]
    y_conv = cw[0:1, :] * z2 + cw[1:2, :] * z1 + cw[2:3, :] * z
    yd_ref[0] = (_dot(xb, w_ref[:, DB:DB + GROUP_W]) * y_conv).astype(bf16)
    carry_z[...] = z[tm - 8:tm, :]
    cst_ref[0] = z[tm - 8:tm, :]


def _prompt_inproj(x, expand_t, w_pack, w_t, fb_row, wst, bs_full, cw8, avg, tri, tm):
    bsz, t, d = x.shape
    nt = t // tm
    tok = lambda w, dt: (jax.ShapeDtypeStruct((bsz, t, w), dt), pl.BlockSpec((1, tm, w), lambda b, i: (b, i, 0)))
    feat = lambda r, dt: (jax.ShapeDtypeStruct((bsz, r, t), dt), pl.BlockSpec((1, r, tm), lambda b, i: (b, 0, i)))
    const = lambda a: pl.BlockSpec(a.shape, lambda b, i: (0,) * a.ndim)
    outs = [
        tok(GROUP_W, bf16),
        tok(GROUP_W, bf16),
        tok(GROUP_W, bf16),
        feat(GROUP_W, bf16),
        feat(2 * GROUP_W, f32),
        tok(LANES, f32),
        tok(LANES, f32),
        feat(8, f32),
        tok(GROUP_W, bf16),
        feat(4 * HEAD_DIM, f32),
        feat(2 * HEAD_DIM, f32),
        tok(GROUP_W, bf16),
        feat(2 * HEAD_DIM, bf16),
        tok(2 * HEAD_DIM, f32),
        tok(GROUP_W, bf16),
        (jax.ShapeDtypeStruct((bsz, 8, GROUP_W), f32), pl.BlockSpec((1, 8, GROUP_W), lambda b, i: (b, 0, 0))),
    ]
    consts = [w_pack, w_t, fb_row, wst, bs_full, cw8, avg, tri]
    return pl.pallas_call(
        functools.partial(_prompt_inproj_kernel, tm=tm),
        grid=(bsz, nt),
        in_specs=[pl.BlockSpec((1, tm, d), lambda b, i: (b, i, 0)), pl.BlockSpec((tm, LANES), lambda b, i: (i, 0))]
        + [const(a) for a in consts],
        out_specs=[o[1] for o in outs],
        out_shape=[o[0] for o in outs],
        scratch_shapes=[pltpu.VMEM((8, LANES), f32), pltpu.VMEM((8, GROUP_W), f32)],
        compiler_params=_cparams(("arbitrary", "arbitrary")),
    )(x, expand_t, *consts)


def _flash_update_t(chains):
    parts = []
    for s, v_t, m_sc, l_sc, acc_sc in chains:
        m_old = m_sc[...]
        m_new = jnp.maximum(m_old, jnp.max(s, axis=0, keepdims=True))
        alpha = jnp.exp(m_old - m_new)
        p = jnp.exp(s - m_new)
        l_sc[...] = alpha * l_sc[...] + jnp.sum(p, axis=0, keepdims=True)
        m_sc[...] = m_new
        parts.append((alpha, p.astype(bf16)))
    for (s, v_t, m_sc, l_sc, acc_sc), (alpha, pb) in zip(chains, parts):
        acc_sc[...] = alpha * acc_sc[...] + _dot(v_t, pb)


def _fox_prompt_kernel(q_ref, k_ref, vt_ref, ccol_ref, crow_ref, sel_ref, o_ref, crep_sc, m_sc, l_sc, acc_sc, *, tk, bq):
    qi = pl.program_id(1)
    n_pair = GROUP_HEADS // 2
    pw = 2 * HEAD_DIM

    @pl.when(qi == 0)
    def _():
        c3 = _split3(ccol_ref[0])
        for h in range(GROUP_HEADS):
            crep_sc[h] = _dot_exact_lhs(c3, sel_ref[h])

    q = q_ref[0]
    lane_hi = lax.broadcasted_iota(jnp.int32, (bq, pw), 1) >= HEAD_DIM
    cr = crow_ref[0]
    qpos = qi * bq + lax.broadcasted_iota(jnp.int32, (1, bq), 1)
    qpos = jnp.concatenate([qpos, qpos], axis=1)
    qbd = []
    cq = []
    for g in range(n_pair):
        qg = q[:, g * pw:(g + 1) * pw]
        qbd.append(jnp.concatenate([jnp.where(lane_hi, jnp.zeros_like(qg), qg),
                                    jnp.where(lane_hi, qg, jnp.zeros_like(qg))], axis=0))
        cq.append(jnp.concatenate([cr[2 * g:2 * g + 1, :], cr[2 * g + 1:2 * g + 2, :]], axis=1))
    m_sc[...] = jnp.full_like(m_sc, NEG)
    l_sc[...] = jnp.zeros_like(l_sc)
    acc_sc[...] = jnp.zeros_like(acc_sc)

    def tile(j, masked):
        k0 = pl.multiple_of(j * tk, tk)
        chains = []
        for g in range(n_pair):
            k = k_ref[0, pl.ds(k0, tk), g * pw:(g + 1) * pw]
            v_t = vt_ref[0, g * pw:(g + 1) * pw, pl.ds(k0, tk)]
            ck = jnp.concatenate([crep_sc[2 * g, pl.ds(k0, tk), :], crep_sc[2 * g + 1, pl.ds(k0, tk), :]], axis=1)
            s = _dot_nt(k, qbd[g]) + cq[g] - ck
            if masked:
                kpos = k0 + lax.broadcasted_iota(jnp.int32, (tk, 1), 0)
                s = jnp.where(kpos <= qpos, s, NEG)
            chains.append((s, v_t, m_sc.at[g], l_sc.at[g], acc_sc.at[g]))
        _flash_update_t(chains)

    n_full = (qi * bq) // tk

    def body(j, carry):
        tile(j, False)
        return carry

    lax.fori_loop(0, n_full, body, 0)
    tile(n_full, True)
    outs = []
    for g in range(n_pair):
        o = acc_sc[g] / l_sc[g]
        outs.append(o[0:HEAD_DIM, 0:bq])
        outs.append(o[HEAD_DIM:pw, bq:2 * bq])
    o_ref[0] = jnp.transpose(jnp.concatenate(outs, axis=0)).astype(bf16)


def _fox_prompt(fq, fk, fvt, ccol, crow, tk, bq):
    bsz, t, _ = fq.shape
    assert tk % bq == 0 and t % tk == 0 and bq == LANES
    sel = np.zeros((GROUP_HEADS, LANES, LANES), np.float32)
    for h in range(GROUP_HEADS):
        sel[h, h, :] = 1.0
    sel = jnp.asarray(sel, bf16)
    return pl.pallas_call(
        functools.partial(_fox_prompt_kernel, tk=tk, bq=bq),
        grid=(bsz, t // bq),
        in_specs=[
            pl.BlockSpec((1, bq, GROUP_W), lambda b, i: (b, i, 0)),
            pl.BlockSpec((1, t, GROUP_W), lambda b, i: (b, 0, 0)),
            pl.BlockSpec((1, GROUP_W, t), lambda b, i: (b, 0, 0)),
            pl.BlockSpec((1, t, LANES), lambda b, i: (b, 0, 0)),
            pl.BlockSpec((1, 8, bq), lambda b, i: (b, 0, i)),
            pl.BlockSpec(sel.shape, lambda b, i: (0, 0, 0)),
        ],
        out_specs=pl.BlockSpec((1, bq, GROUP_W), lambda b, i: (b, i, 0)),
        out_shape=jax.ShapeDtypeStruct((bsz, t, GROUP_W), bf16),
        scratch_shapes=[pltpu.VMEM((GROUP_HEADS, t, LANES), f32), pltpu.VMEM((2, 1, 2 * bq), f32),
                        pltpu.VMEM((2, 1, 2 * bq), f32), pltpu.VMEM((2, 2 * HEAD_DIM, 2 * bq), f32)],
        compiler_params=_cparams(("arbitrary", "arbitrary")),
    )(fq, fk, fvt, ccol, crow, sel)


def _compress_kernel(x_ref, pe_ref, w1_ref, w2_ref, o_ref, ot_ref):
    x = x_ref[0]
    n_grp = x.shape[0]
    top = _dot((x + pe_ref[0:1, :]).astype(bf16), w1_ref[0])
    bot = _dot((x + pe_ref[1:2, :]).astype(bf16), w1_ref[1])
    g = _gelu(top + pltpu.roll(bot, n_grp - 1, 0)).astype(bf16)
    w2 = w2_ref[...]
    kc = _dot(g[:, 0:CMP_HID], w2[:, 0:HEAD_DIM])
    vc = _dot(g[:, CMP_HID:2 * CMP_HID], w2[:, HEAD_DIM:2 * HEAD_DIM])
    cmp = jnp.concatenate([kc, vc], axis=-1)
    o_ref[0] = cmp.astype(bf16)
    ot_ref[0] = jnp.transpose(cmp).astype(bf16)


def _compress(x_groups, pe2, w1g, w2c):
    bsz, n_grp, w = x_groups.shape
    const = lambda a: pl.BlockSpec(a.shape, lambda b: (0,) * a.ndim)
    return pl.pallas_call(
        _compress_kernel,
        grid=(bsz,),
        in_specs=[pl.BlockSpec((1, n_grp, w), lambda b: (b, 0, 0)), const(pe2), const(w1g), const(w2c)],
        out_specs=[pl.BlockSpec((1, n_grp, 2 * HEAD_DIM), lambda b: (b, 0, 0)),
                   pl.BlockSpec((1, 2 * HEAD_DIM, n_grp), lambda b: (b, 0, 0))],
        out_shape=[jax.ShapeDtypeStruct((bsz, n_grp, 2 * HEAD_DIM), bf16),
                   jax.ShapeDtypeStruct((bsz, 2 * HEAD_DIM, n_grp), bf16)],
        compiler_params=_cparams(("arbitrary",)),
    )(x_groups, pe2, w1g, w2c)


def _stack_heads(q, rows):
    return jnp.concatenate([q[:, h * HEAD_DIM:(h + 1) * HEAD_DIM] for h in range(GROUP_HEADS)], axis=0)


def _unstack_heads(o, rows):
    return jnp.concatenate([o[h * rows:(h + 1) * rows] for h in range(GROUP_HEADS)], axis=-1)


def _gate_col(misc, branch):
    base = N_FOX_F + branch * GROUP_HEADS
    return jnp.concatenate([misc[:, base + h:base + h + 1] for h in range(GROUP_HEADS)], axis=0)


def _nsa_compressed(qs, cmp, qpos_s, selw, nq):
    n_c = cmp.shape[0]
    kc = cmp[:, 0:HEAD_DIM]
    vc = cmp[:, HEAD_DIM:2 * HEAD_DIM]
    s = _dot_nt(qs, kc)
    cend = lax.broadcasted_iota(jnp.int32, (1, n_c), 1) * CMP_STRIDE + (CMP_LEN - 1)
    vis = cend <= qpos_s
    s = jnp.where(vis, s, NEG)
    e = jnp.exp(s - jnp.max(s, axis=-1, keepdims=True))
    p = e / jnp.sum(e, axis=-1, keepdims=True)
    p = p * jnp.where(qpos_s >= CMP_LEN - 1, 1.0, 0.0)
    pb = p.astype(bf16)
    o_c = _dot(pb, vc)
    imp = _dot(pb[0:nq], selw)
    for h in range(1, GROUP_HEADS):
        imp = imp + _dot(pb[h * nq:(h + 1) * nq], selw)
    return o_c, imp


def _nsa_select(imp, qpos_row, n_sel_pad):
    nq = imp.shape[0]
    if nq < LANES:
        imp = jnp.concatenate([imp, jnp.zeros((LANES - nq, LANES), f32)], axis=0)
    sel_t = _nsa_select_t(jnp.transpose(imp)[0:n_sel_pad], qpos_row)
    if n_sel_pad < LANES:
        sel_t = jnp.concatenate([sel_t, jnp.zeros((LANES - n_sel_pad, LANES), f32)], axis=0)
    return jnp.transpose(sel_t)[0:nq]


def _nsa_select_t(imp_t, qpos_row):
    n_sel_pad = imp_t.shape[0]
    blk = lax.broadcasted_iota(jnp.int32, (n_sel_pad, LANES), 0)
    cur = qpos_row // SEL_LEN
    valid = blk * SEL_LEN <= qpos_row
    forced = (blk == 0) | (blk == cur) | (blk == cur - 1)
    val = jnp.where(valid, jnp.where(forced, BIG, imp_t), -BIG)
    n_g = n_sel_pad // 8
    grp = [val[8 * g:8 * g + 8] for g in range(n_g)]
    rank = [jnp.zeros((8, LANES), f32) for _ in range(n_g)]
    row8 = lax.broadcasted_iota(jnp.int32, (8, LANES), 0)
    for mp in range(n_sel_pad):
        vb = jnp.broadcast_to(val[mp:mp + 1, :], (8, LANES))
        for g in range(n_g):
            if 8 * g + 7 < mp:
                hit = jnp.where(vb > grp[g], 1.0, 0.0)
            elif 8 * g > mp:
                hit = jnp.where(vb >= grp[g], 1.0, 0.0)
            else:
                hit = jnp.where(row8 > mp - 8 * g, jnp.where(vb >= grp[g], 1.0, 0.0), jnp.where(vb > grp[g], 1.0, 0.0))
            rank[g] = rank[g] + hit
    rank = jnp.concatenate(rank, axis=0)
    return jnp.where(valid, jnp.where(rank < SEL_TOPK, 1.0, 0.0), 0.0)


def _nsa_prompt_kernel(q_ref, kse_ref, ft_ref, cmp_ref, cmpt_ref, misc_ref, selwt_ref, o_ref, m_sc, l_sc, acc_sc, *,
                       tk, t, n_sel_pad, bq):
    qi = pl.program_id(1)
    r_all = GROUP_HEADS * bq
    q_t4 = jnp.transpose(q_ref[0].astype(f32))
    q_t = jnp.concatenate([q_t4[h * HEAD_DIM:(h + 1) * HEAD_DIM] for h in range(GROUP_HEADS)], axis=1).astype(bf16)
    zq = jnp.zeros((HEAD_DIM, r_all), bf16)
    qpos1 = qi * bq + lax.broadcasted_iota(jnp.int32, (1, bq), 1)
    qpos = jnp.concatenate([qpos1] * GROUP_HEADS, axis=1)

    span = min(WINDOW + bq, t)
    w0 = pl.multiple_of(jnp.maximum(qi * bq - WINDOW, 0), LANES)
    q_win = jnp.concatenate([zq, q_t, jnp.zeros((LANES, r_all), bf16)], axis=0)
    cmp = cmp_ref[0]
    n_c = cmp.shape[0]
    s_w = _dot(kse_ref[0, pl.ds(w0, span), :], q_win)
    s_c = _dot(cmp, jnp.concatenate([q_t, zq], axis=0))
    wpos = w0 + lax.broadcasted_iota(jnp.int32, (span, 1), 0)
    s_w = jnp.where((wpos <= qpos) & (wpos > qpos - WINDOW), s_w, NEG)
    cend = lax.broadcasted_iota(jnp.int32, (n_c, 1), 0) * CMP_STRIDE + (CMP_LEN - 1)
    s_c = jnp.where(cend <= qpos, s_c, NEG)
    e_c = jnp.exp(s_c - jnp.max(s_c, axis=0, keepdims=True))
    p_c = e_c / jnp.sum(e_c, axis=0, keepdims=True)
    pb = (p_c * jnp.where(qpos >= CMP_LEN - 1, 1.0, 0.0)).astype(bf16)
    e_w = jnp.exp(s_w - jnp.max(s_w, axis=0, keepdims=True))
    p_w = (e_w / jnp.sum(e_w, axis=0, keepdims=True)).astype(bf16)
    o_c = _dot(cmpt_ref[0, HEAD_DIM:2 * HEAD_DIM, :], pb)
    o_w = _dot(ft_ref[0, HEAD_DIM:2 * HEAD_DIM, pl.ds(w0, span)], p_w)
    selwt = selwt_ref[...]
    imp_t = _dot(selwt, pb[:, 0:bq])
    for h in range(1, GROUP_HEADS):
        imp_t = imp_t + _dot(selwt, pb[:, h * bq:(h + 1) * bq])
    sel_t = _nsa_select_t(imp_t[0:n_sel_pad], qpos1)
    unsel = jnp.where(sel_t > 0.5, 0.0, -(2.0 ** 100))
    if n_sel_pad < LANES:
        unsel = jnp.concatenate([unsel, jnp.zeros((LANES - n_sel_pad, bq), f32)], axis=0)
    unsel = jnp.concatenate([unsel.astype(bf16)] * GROUP_HEADS, axis=1)

    q_sel = jnp.concatenate([q_t, zq, unsel], axis=0)
    m_sc[...] = jnp.full_like(m_sc, NEG)
    l_sc[...] = jnp.zeros_like(l_sc)
    acc_sc[...] = jnp.zeros_like(acc_sc)

    n_chain = 2
    rc = r_all // n_chain

    def scores(j):
        kse = kse_ref[0, pl.ds(pl.multiple_of(j * tk, tk), tk), :]
        return tuple(_dot(kse, q_sel[:, c * rc:(c + 1) * rc]) for c in range(n_chain))

    def consume(j, s, diagonal):
        k0 = pl.multiple_of(j * tk, tk)
        v_t = ft_ref[0, 0:HEAD_DIM, pl.ds(k0, tk)]
        chains = []
        for c in range(n_chain):
            sc = s[c]
            if diagonal:
                kpos = k0 + lax.broadcasted_iota(jnp.int32, (tk, 1), 0)
                sc = jnp.where(kpos <= qpos[:, c * rc:(c + 1) * rc], sc, NEG)
            chains.append((sc, v_t, m_sc.at[c], l_sc.at[c], acc_sc.at[c]))
        _flash_update_t(chains)

    n_full = (qi * bq) // tk

    def body(j, s):
        s_next = scores(j + 1)
        consume(j, s, False)
        return s_next

    s_last = lax.fori_loop(0, n_full, body, scores(0))
    consume(n_full, s_last, True)
    o_s = jnp.concatenate([acc_sc[c] / l_sc[c] for c in range(n_chain)], axis=1)

    misc_t = jnp.transpose(misc_ref[0])

    def gate(branch):
        base = N_FOX_F + branch * GROUP_HEADS
        return jnp.concatenate([misc_t[base + h:base + h + 1, :] for h in range(GROUP_HEADS)], axis=1)

    o = gate(0) * o_c + gate(1) * o_s + gate(2) * o_w
    o = jnp.concatenate([o[:, h * bq:(h + 1) * bq] for h in range(GROUP_HEADS)], axis=0)
    o_ref[0] = jnp.transpose(o).astype(bf16)


def _nsa_prompt(nq_b, kse, feat_t, cmp, cmp_t, misc, selw_t, tk, bq):
    bsz, t, _ = nq_b.shape
    r_all = GROUP_HEADS * bq
    n_c = cmp.shape[1]
    n_sel_pad = -(-(t // SEL_LEN) // 8) * 8
    assert tk % bq == 0 and bq == LANES
    return pl.pallas_call(
        functools.partial(_nsa_prompt_kernel, tk=tk, t=t, n_sel_pad=n_sel_pad, bq=bq),
        grid=(bsz, t // bq),
        in_specs=[
            pl.BlockSpec((1, bq, GROUP_W), lambda b, i: (b, i, 0)),
            pl.BlockSpec((1, t, GROUP_W), lambda b, i: (b, 0, 0)),
            pl.BlockSpec((1, 2 * HEAD_DIM, t), lambda b, i: (b, 0, 0)),
            pl.BlockSpec((1, n_c, 2 * HEAD_DIM), lambda b, i: (b, 0, 0)),
            pl.BlockSpec((1, 2 * HEAD_DIM, n_c), lambda b, i: (b, 0, 0)),
            pl.BlockSpec((1, bq, LANES), lambda b, i: (b, i, 0)),
            pl.BlockSpec(selw_t.shape, lambda b, i: (0, 0)),
        ],
        out_specs=pl.BlockSpec((1, bq, GROUP_W), lambda b, i: (b, i, 0)),
        out_shape=jax.ShapeDtypeStruct((bsz, t, GROUP_W), bf16),
        scratch_shapes=[pltpu.VMEM((2, 1, r_all // 2), f32), pltpu.VMEM((2, 1, r_all // 2), f32),
                        pltpu.VMEM((2, HEAD_DIM, r_all // 2), f32)],
        compiler_params=_cparams(("arbitrary", "arbitrary")),
    )(nq_b, kse, feat_t, cmp, cmp_t, misc, selw_t)


def _matmul_kernel(x_ref, w_ref, o_ref):
    o_ref[...] = _dot(x_ref[...].astype(bf16), w_ref[...])


def _sample_inproj(x2, w_pack, tm):
    n, d = x2.shape
    pw = w_pack.shape[1]
    return pl.pallas_call(
        _matmul_kernel,
        grid=(n // tm,),
        in_specs=[pl.BlockSpec((tm, d), lambda i: (i, 0)), pl.BlockSpec((d, pw), lambda i: (0, 0))],
        out_specs=pl.BlockSpec((tm, pw), lambda i: (i, 0)),
        out_shape=jax.ShapeDtypeStruct((n, pw), f32),
        compiler_params=_cparams(("arbitrary",)),
    )(x2, w_pack)


def _pad_to(a, rows):
    return jnp.concatenate([a, jnp.zeros((rows - a.shape[0], a.shape[1]), a.dtype)], axis=0)


def _softmax_tiles(tiles):
    m = tiles[0].max(axis=-1, keepdims=True)
    for s in tiles[1:]:
        m = jnp.maximum(m, s.max(axis=-1, keepdims=True))
    ps = [jnp.exp(s - m) for s in tiles]
    tot = ps[0].sum(axis=-1, keepdims=True)
    for p in ps[1:]:
        tot = tot + p.sum(axis=-1, keepdims=True)
    return ps, tot


def _sample_mixer_kernel(pt_ref, h_ref, *refs, n_pages, page, past, n_sel_pad):
    del pt_ref
    fox_pages = refs[0:n_pages]
    lf_pages = refs[n_pages:2 * n_pages]
    nsa_pages = refs[2 * n_pages:3 * n_pages]
    (win_ref, conv_ref, fb_ref, wst_ref, bs_ref, cw_ref, avg_ref, utri_ref, ones_ref, slt_ref, tri8_ref,
     pe_ref, w1_ref, w2_ref, selw_ref, exp_ref,
     y_ref, fkv_ref, misc_ref, nrow_ref, wout_ref, cout_ref, gv_ref, rows_sc) = refs[3 * n_pages:]
    tn = h_ref.shape[1]
    h = h_ref[0]
    qpos = past + lax.broadcasted_iota(jnp.int32, (tn, 1), 0)
    qpos_s = jnp.concatenate([qpos] * GROUP_HEADS, axis=0)
    new_pos = past + lax.broadcasted_iota(jnp.int32, (1, LANES), 1)
    new_ok = new_pos <= qpos_s

    ckv = h[:, CKV:CKV + 6 * HEAD_DIM]
    nrow_ref[0] = ckv[:, 0:4 * HEAD_DIM]
    for pg in range(n_pages):
        rows_sc[pg * page:(pg + 1) * page, :] = jnp.transpose(nsa_pages[pg][0, 0, 0:2 * HEAD_DIM, :])
    n_grp = n_pages * page // CMP_STRIDE
    half = CMP_LEN // 2
    tops = []
    bots = []
    for r in range(half):
        a_r = rows_sc[pl.ds(r, n_grp, stride=CMP_STRIDE), :]
        tops.append((a_r + pe_ref[r:r + 1, :]).astype(bf16))
        bots.append((a_r + pe_ref[half + r:half + r + 1, :]).astype(bf16))
    top = _dot(jnp.concatenate(tops, axis=1), w1_ref[0])
    bot = _dot(jnp.concatenate(bots, axis=1), w1_ref[1])
    g = _gelu(top + pltpu.roll(bot, n_grp - 1, 0)).astype(bf16)
    w2 = w2_ref[...]
    cmp = jnp.concatenate([_dot(g[:, 0:CMP_HID], w2[:, 0:HEAD_DIM]),
                           _dot(g[:, CMP_HID:2 * CMP_HID], w2[:, HEAD_DIM:2 * HEAD_DIM])], axis=-1).astype(bf16)
    qs = _stack_heads((h[:, CQ:CQ + GROUP_W] * SCALE).astype(bf16), tn)
    o_c, imp = _nsa_compressed(qs, cmp, qpos_s, selw_ref[...], tn)
    sel = _nsa_select(imp, new_pos, n_sel_pad).astype(bf16)

    pre = h[:, MISC:MISC + LANES] + fb_ref[0:1, :]
    lane = lax.broadcasted_iota(jnp.int32, (tn, LANES), 1)
    lf_new = jnp.where(lane < N_FOX_F, _log_sigmoid(pre), 0.0)
    misc = jnp.where(lane < N_FOX_F, lf_new, jnp.where(lane < N_FOX_F + N_GATES, jax.nn.sigmoid(pre), 0.0))
    misc_ref[0] = misc

    lf3 = _split3(jnp.concatenate([r[0, 0] for r in lf_pages], axis=0))
    tot = _dot_exact_lhs(lf3, ones_ref[...])
    off = _dot_exact_rhs(slt_ref[...], _split3(_pad_to(tot, LANES)))
    c_past = _dot_exact_lhs(lf3, utri_ref[...]) + off
    last = GROUP_HEADS * (n_pages - 1)
    total_past = off[last:last + GROUP_HEADS] + tot[last:last + GROUP_HEADS]
    c_new = jnp.transpose(_pad_to(total_past, LANES))[0:1, :] + _dot_exact_rhs(
        tri8_ref[...], _split3(_pad_to(lf_new, LANES)))
    cq = jnp.concatenate([c_new[:, hh:hh + 1] for hh in range(GROUP_HEADS)], axis=0)
    c_new_t = jnp.transpose(_pad_to(c_new, LANES))

    qbd =_head_blockdiag_q((h[:, BQ:BQ + GROUP_W] * SCALE).astype(bf16), tn)
    kv_new = h[:, BK:BK + 2 * GROUP_W]
    fkv_ref[0] = kv_new
    kvb_new = _pad_to(kv_new, LANES).astype(bf16)

    def head_rows(c4):
        return jnp.concatenate([jnp.broadcast_to(c4[hh:hh + 1, :], (tn, c4.shape[1])) for hh in range(GROUP_HEADS)],
                               axis=0)

    tiles = []
    for pg in range(n_pages):
        k_t = fox_pages[pg][0, 0, 0:GROUP_W, :].astype(bf16)
        tiles.append(_dot(qbd, k_t) + cq - head_rows(c_past[GROUP_HEADS * pg:GROUP_HEADS * (pg + 1)]))
    tiles.append(jnp.where(new_ok, _dot_nt(qbd, kvb_new[:, 0:GROUP_W]) + cq - head_rows(c_new_t[0:GROUP_HEADS]), NEG))
    ps, tot_p = _softmax_tiles(tiles)
    acc = _dot(ps[n_pages].astype(bf16), kvb_new[:, GROUP_W:2 * GROUP_W])
    for pg in range(n_pages):
        acc = acc + _dot_nt(ps[pg].astype(bf16), fox_pages[pg][0, 0, GROUP_W:2 * GROUP_W, :].astype(bf16))
    y_b = _head_diag(acc / tot_p, tn)

    u = _gelu(h[:, AU:AU + GROUP_W])
    vn = _head_layer_norm(_gelu(h[:, AV:AV + GROUP_W]), avg_ref[...])
    gv_ref[0] = vn
    w8 = jnp.concatenate([wst_ref[hh * CHUNK:hh * CHUNK + tn, :] for hh in range(GROUP_HEADS)], axis=0)
    z_all = _dot(w8, _pad_to(vn, CHUNK).astype(bf16))
    y_a = u * (_head_diag(z_all, tn) + bs_ref[0:tn, :])

    z = h[:, DC:DC + GROUP_W] * h[:, DH:DH + GROUP_W]
    zp = conv_ref[0, 0]
    row = lax.broadcasted_iota(jnp.int32, (tn, GROUP_W), 0)
    z1 = jnp.where(row == 0, zp[1:2, :], pltpu.roll(z, 1, 0))
    z2 = jnp.where(row == 0, zp[0:1, :], jnp.where(row == 1, zp[1:2, :], pltpu.roll(z, 2, 0)))
    cw = cw_ref[...]
    y_d = h[:, DB:DB + GROUP_W] * (cw[0:1, :] * z2 + cw[1:2, :] * z1 + cw[2:3, :] * z)
    cout_ref[0] = z

    sel_new =_pad_to(ckv[:, 2 * HEAD_DIM:4 * HEAD_DIM], LANES).astype(bf16)
    tiles = []
    for pg in range(n_pages + 1):
        picked = _dot(sel, exp_ref[:, pg * page:(pg + 1) * page])
        if pg == n_pages:
            picked = jnp.where(new_pos <= qpos, picked, 0.0)
            sc = _dot_nt(qs, sel_new[:, 0:HEAD_DIM])
        else:
            sc = _dot(qs, nsa_pages[pg][0, 0, 2 * HEAD_DIM:3 * HEAD_DIM, :].astype(bf16))
        keep = jnp.concatenate([picked] * GROUP_HEADS, axis=0) > 0.5
        tiles.append(jnp.where(keep, sc, NEG))
    ps, tot_p = _softmax_tiles(tiles)
    acc = _dot(ps[n_pages].astype(bf16), sel_new[:, HEAD_DIM:2 * HEAD_DIM])
    for pg in range(n_pages):
        acc = acc + _dot_nt(ps[pg].astype(bf16), nsa_pages[pg][0, 0, 3 * HEAD_DIM:4 * HEAD_DIM, :].astype(bf16))
    o_s = acc / tot_p

    win_old = win_ref[0, 0]
    win_new = ckv[:, 4 * HEAD_DIM:6 * HEAD_DIM]
    new_t = pltpu.roll(jnp.transpose(_pad_to(win_new, LANES)), LANES - tn, 1)
    new_wide = jnp.concatenate([jnp.zeros((2 * HEAD_DIM, WINDOW - LANES), f32), new_t], axis=1)
    w_idx = lax.broadcasted_iota(jnp.int32, (1, WINDOW), 1)
    wout_ref[0] = jnp.where(w_idx >= WINDOW - tn, new_wide, pltpu.roll(win_old, WINDOW - tn, 1))
    wob = win_old.astype(bf16)
    wnb = _pad_to(win_new, LANES).astype(bf16)
    s_old = jnp.where(w_idx > qpos_s - past, _dot(qs, wob[0:HEAD_DIM, :]), NEG)
    s_new = jnp.where(new_ok, _dot_nt(qs, wnb[:, 0:HEAD_DIM]), NEG)
    ps, tot_p = _softmax_tiles([s_old, s_new])
    o_w = (_dot_nt(ps[0].astype(bf16), wob[HEAD_DIM:2 * HEAD_DIM, :])
           + _dot(ps[1].astype(bf16), wnb[:, HEAD_DIM:2 * HEAD_DIM])) / tot_p

    o = _gate_col(misc, 0) * o_c + _gate_col(misc, 1) * o_s + _gate_col(misc, 2) * o_w
    y_c = _unstack_heads(o, tn)
    y_ref[0] = jnp.concatenate([y_a, y_b, y_c, y_d], axis=-1)


def _sample_mixer(h3, fox_t, lf_t, nsa_t, win_t, conv4, page_table, layer, consts):
    bsz, tn, pw = h3.shape
    n_pages = page_table.shape[1]
    page = fox_t.shape[3]
    past = n_pages * page
    n_sel_pad = -(-(-(-(past + tn) // SEL_LEN)) // 8) * 8

    def page_spec(arr, pg):
        blk = (1, 1) + arr.shape[2:]
        return pl.BlockSpec(blk, lambda b, pt: (layer, pt[b, pg], 0, 0))

    in_specs = [pl.BlockSpec((1, tn, pw), lambda b, pt: (b, 0, 0))]
    operands = [h3]
    for arr in (fox_t, lf_t, nsa_t):
        for pg in range(n_pages):
            in_specs.append(page_spec(arr, pg))
            operands.append(arr)
    in_specs.append(pl.BlockSpec((1, 1) + win_t.shape[2:], lambda b, pt: (layer, b, 0, 0)))
    operands.append(win_t)
    in_specs.append(pl.BlockSpec((1, 1) + conv4.shape[2:], lambda b, pt: (layer, b, 0, 0)))
    operands.append(conv4)
    for a in consts:
        in_specs.append(pl.BlockSpec(a.shape, lambda b, pt, nd=a.ndim: (0,) * nd))
        operands.append(a)
    out_dims = [(tn, 4 * GROUP_W), (tn, 2 * GROUP_W), (tn, LANES), (tn, 4 * HEAD_DIM), (2 * HEAD_DIM, WINDOW),
                (tn, GROUP_W), (tn, GROUP_W)]
    out_shape = [jax.ShapeDtypeStruct((bsz, r, w), f32) for r, w in out_dims]
    out_specs = [pl.BlockSpec((1, r, w), lambda b, pt: (b, 0, 0)) for r, w in out_dims]
    return pl.pallas_call(
        functools.partial(_sample_mixer_kernel, n_pages=n_pages, page=page, past=past, n_sel_pad=n_sel_pad),
        grid_spec=pltpu.PrefetchScalarGridSpec(num_scalar_prefetch=1, grid=(bsz,), in_specs=in_specs,
                                               out_specs=out_specs,
                                               scratch_shapes=[pltpu.VMEM((past, 2 * HEAD_DIM), f32)]),
        out_shape=out_shape,
        compiler_params=_cparams(("arbitrary",)),
    )(page_table, *operands)


def _outproj_kernel(x_ref, *refs, alpha):
    y_refs = refs[:-4]
    w_ref, g_ref, b_ref, o_ref = refs[-4:]
    acc = None
    c0 = 0
    for y_ref in y_refs:
        wy = y_ref.shape[1]
        part = _dot(y_ref[...].astype(bf16), w_ref[c0:c0 + wy, :])
        acc = part if acc is None else acc + part
        c0 += wy
    o_ref[...] = _layer_norm_rows(alpha * x_ref[...] + acc, g_ref[0:1, :], b_ref[0:1, :])


def _outproj(x2, ys, w_out_b, g8, b8, alpha, tm):
    n, d = x2.shape
    assert sum(y.shape[1] for y in ys) == d
    const = lambda a: pl.BlockSpec(a.shape, lambda i: (0,) * a.ndim)
    return pl.pallas_call(
        functools.partial(_outproj_kernel, alpha=alpha),
        grid=(n // tm,),
        in_specs=[pl.BlockSpec((tm, d), lambda i: (i, 0))] + [pl.BlockSpec((tm, y.shape[1]), lambda i: (i, 0)) for y in ys]
        + [const(w_out_b), const(g8), const(b8)],
        out_specs=pl.BlockSpec((tm, d), lambda i: (i, 0)),
        out_shape=jax.ShapeDtypeStruct((n, d), f32),
        compiler_params=_cparams(("arbitrary",)),
    )(x2, *ys, w_out_b, g8, b8)


def _route(xb, wr_t, tm):
    logits = _dot_nt(wr_t, xb)
    e = jnp.exp(logits - jnp.max(logits, axis=0, keepdims=True))
    probs = e / jnp.sum(e, axis=0, keepdims=True)
    p = [probs[i:i + 1, :] for i in range(N_EXPERTS)]
    one = jnp.ones((1, tm), f32)
    zero = jnp.zeros((1, tm), f32)
    in_top = []
    gscore = []
    for g in range(N_GROUPS):
        members = range(g * EXP_PER_GROUP, (g + 1) * EXP_PER_GROUP)
        score = zero
        for i in members:
            rank = zero
            for j in members:
                if j < i:
                    rank = rank + jnp.where(p[j] >= p[i], one, zero)
                elif j > i:
                    rank = rank + jnp.where(p[j] > p[i], one, zero)
            keep = jnp.where(rank < TOP_K, one, zero)
            in_top.append(keep)
            score = score + keep * p[i]
        gscore.append(score)
    gate_rows = []
    wsum = zero
    chosen = []
    for g in range(N_GROUPS):
        win = one
        for g2 in range(N_GROUPS):
            if g2 < g:
                win = win * jnp.where(gscore[g] > gscore[g2], one, zero)
            elif g2 > g:
                win = win * jnp.where(gscore[g] >= gscore[g2], one, zero)
        for i in range(g * EXP_PER_GROUP, (g + 1) * EXP_PER_GROUP):
            c = in_top[i] * win
            chosen.append(c)
            wsum = wsum + c * p[i]
    for i in range(N_EXPERTS):
        gate_rows.append(jnp.where(chosen[i] > 0.5, p[i] / wsum, zero))
    gate_t = jnp.concatenate(gate_rows + [jnp.zeros((LANES - N_EXPERTS, tm), f32)], axis=0)
    return jnp.concatenate([jnp.transpose(gate_t[:, c * LANES:(c + 1) * LANES]) for c in range(tm // LANES)], axis=0)


def _moe_kernel(x_ref, wr_ref, wg_ref, wu_ref, wd_ref, g_ref, b_ref, o_ref, gate_sc, acc_sc, xb_sc, *, alpha, tm):
    grp = pl.program_id(1)
    n_e = wg_ref.shape[0]

    @pl.when(grp == 0)
    def _():
        xb = x_ref[...].astype(bf16)
        xb_sc[...] = xb
        gate_sc[...] = _route(xb, wr_ref[...], tm)

    xb = xb_sc[...]
    lane = lax.broadcasted_iota(jnp.int32, (tm, LANES), 1)
    gate = gate_sc[...]
    gates = [_dot(xb, wg_ref[j]) for j in range(n_e)]
    ups = [_dot(xb, wu_ref[j]) for j in range(n_e)]
    hs = []
    for j in range(n_e):
        gcol = jnp.sum(jnp.where(lane == grp * n_e + j, gate, 0.0), axis=-1, keepdims=True)
        hs.append((jax.nn.silu(gates[j]) * ups[j] * gcol).astype(bf16))
    y = _dot(jnp.concatenate(hs, axis=-1), wd_ref[...].reshape(n_e * wd_ref.shape[1], wd_ref.shape[2]))

    @pl.when(grp == 0)
    def _():
        acc_sc[...] = y

    @pl.when(grp > 0)
    def _():
        acc_sc[...] += y

    @pl.when(grp == pl.num_programs(1) - 1)
    def _():
        o_ref[...] = _layer_norm_rows(alpha * x_ref[...] + acc_sc[...], g_ref[0:1, :], b_ref[0:1, :])


def _moe(x2, wr_t, wg, wu, wd, g8, b8, alpha, tm):
    n, d = x2.shape
    de = wg.shape[-1]
    es = EXP_PER_GROUP
    const = lambda a: pl.BlockSpec(a.shape, lambda i, e: (0,) * a.ndim)
    return pl.pallas_call(
        functools.partial(_moe_kernel, alpha=alpha, tm=tm),
        grid=(n // tm, N_EXPERTS // es),
        in_specs=[
            pl.BlockSpec((tm, d), lambda i, e: (i, 0)), const(wr_t),
            pl.BlockSpec((es, d, de), lambda i, e: (e, 0, 0)), pl.BlockSpec((es, d, de), lambda i, e: (e, 0, 0)),
            pl.BlockSpec((es, de, d), lambda i, e: (e, 0, 0)), const(g8), const(b8),
        ],
        out_specs=pl.BlockSpec((tm, d), lambda i, e: (i, 0)),
        out_shape=jax.ShapeDtypeStruct((n, d), f32),
        scratch_shapes=[pltpu.VMEM((tm, LANES), f32), pltpu.VMEM((tm, d), f32), pltpu.VMEM((tm, d), bf16)],
        compiler_params=_cparams(("arbitrary", "arbitrary")),
    )(x2, wr_t, wg, wu, wd, g8, b8)


def _pad_rows(a, rows=8):
    return jnp.concatenate([a, jnp.zeros((rows - a.shape[0],) + a.shape[1:], a.dtype)], axis=0)


def _prep_layer(l, w_in, fox_b_f, gmlp_w_s, gmlp_b_s, nsa_cmp_pe, nsa_cmp_w1, nsa_cmp_w2, conv_w, w_out, ln_g, ln_b,
                w_exp_gate, w_exp_up, w_exp_down):
    w = w_in[l]
    d = w.shape[0]
    w_pack = jnp.concatenate([w[:, 0:1280], w[:, 1284:1924], w[:, 1936:2704], w[:, 1280:1284], w[:, 1924:1936],
                              jnp.zeros((d, LANES - N_FOX_F - N_GATES), w.dtype)], axis=1).astype(bf16)
    fb_row = _pad_rows(jnp.concatenate([fox_b_f[l], jnp.zeros((LANES - N_FOX_F,), f32)])[None, :])
    tril = jnp.asarray(np.tril(np.ones((CHUNK, CHUNK), np.float32)))
    wst = (gmlp_w_s[l] * tril).reshape(GROUP_HEADS * CHUNK, CHUNK).astype(bf16)
    bs_full = jnp.repeat(gmlp_b_s[l].T, HEAD_DIM, axis=1)
    cw8 = _pad_rows(conv_w[l])
    half = CMP_LEN // 2
    w1r = nsa_cmp_w1[l].reshape(2, CMP_LEN, HEAD_DIM, CMP_HID)
    per = nsa_cmp_pe[l]
    w2c = jnp.concatenate([nsa_cmp_w2[l, 0], nsa_cmp_w2[l, 1]], axis=1).astype(bf16)
    zed = jnp.zeros((CMP_LEN, HEAD_DIM, CMP_HID), f32)
    w1r = jnp.concatenate([jnp.concatenate([w1r[0], zed], axis=2), jnp.concatenate([zed, w1r[1]], axis=2)],
                          axis=1).astype(bf16)
    pe32 = jnp.concatenate([per[0], per[1]], axis=1)
    w1g = w1r.reshape(2, half * 2 * HEAD_DIM, 2 * CMP_HID)
    pe2 = _pad_rows(pe32.reshape(2, half * 2 * HEAD_DIM))
    w_t = jnp.concatenate([w_pack[:, BK:BK + 2 * GROUP_W], w_pack[:, CKV:CKV + 6 * HEAD_DIM]], axis=1).T
    return dict(
        w_pack=w_pack, w_t=w_t, fb_row=fb_row, wst=wst, bs_full=bs_full, cw8=cw8, w1g=w1g, pe2=pe2, w2c=w2c, w1r=w1r,
        pe32=pe32,
        w_out=w_out[l].astype(bf16), g1=_pad_rows(ln_g[l, 0:1]), b1=_pad_rows(ln_b[l, 0:1]),
        g2=_pad_rows(ln_g[l, 1:2]), b2=_pad_rows(ln_b[l, 1:2]),
        wg=w_exp_gate[l].astype(bf16), wu=w_exp_up[l].astype(bf16), wd=w_exp_down[l].astype(bf16),
    )


def _sel_overlap(t, n_sel):
    n_cmp = (t - CMP_LEN) // CMP_STRIDE + 1
    start = np.arange(n_cmp) * CMP_STRIDE
    sstart = np.arange(n_sel) * SEL_LEN
    ov = np.minimum(start[:, None] + CMP_LEN, sstart[None, :] + SEL_LEN) - np.maximum(start[:, None], sstart[None, :])
    return np.clip(ov, 0, None) / CMP_STRIDE


def _head_avg():
    a = np.kron(np.eye(GROUP_HEADS, dtype=np.float32), np.full((HEAD_DIM, HEAD_DIM), 1.0 / HEAD_DIM, np.float32))
    return jnp.asarray(a, bf16)


def _post_block(x2, ys, p, wr_t, alpha, tm):
    x1 = _outproj(x2, ys, p['w_out'], p['g1'], p['b1'], alpha, tm)
    return _moe(x1, wr_t, p['wg'], p['wu'], p['wd'], p['g2'], p['b2'], alpha, tm)


def _prompt_layer(x, p, wr_t, alpha, tm=512, tk=512, bq=128):
    bsz, t, d = x.shape
    n = bsz * t
    n_sel = t // SEL_LEN
    n_grp = t // CMP_STRIDE
    assert t % tm == 0 and t % tk == 0 and n_sel <= LANES and t >= WINDOW + bq
    tri = jnp.asarray(np.tril(np.ones((tm, tm), np.float32)), bf16)
    expand_t = np.zeros((t, LANES), np.float32)
    expand_t[np.arange(t), np.arange(t) // SEL_LEN] = 1.0
    (ya, fq, fk, fvt, fkvt, misc, ccol, crow, nq, nkvt, wint, kse, featt, kcvc, yd, cst) = _prompt_inproj(
        x, jnp.asarray(expand_t, bf16), p['w_pack'], p['w_t'], p['fb_row'], p['wst'], p['bs_full'], p['cw8'],
        _head_avg(), tri, tm)
    yb = _fox_prompt(fq, fk, fvt, ccol, crow, tk, bq)
    cmp, cmp_t = _compress(kcvc.reshape(bsz, n_grp, CMP_STRIDE * 2 * HEAD_DIM), p['pe2'], p['w1g'], p['w2c'])
    ov = _sel_overlap(t, n_sel)
    selw_t = np.zeros((LANES, n_grp), np.float32)
    selw_t[:n_sel, :ov.shape[0]] = ov.T
    yc = _nsa_prompt(nq, kse, featt, cmp, cmp_t, misc, jnp.asarray(selw_t, bf16), tk, bq)
    ys = [a.reshape(n, GROUP_W) for a in (ya, yb, yc, yd)]
    x_out = _post_block(x.reshape(n, d), ys, p, wr_t, alpha, tm).reshape(bsz, t, d)
    win_keep = min(WINDOW, t)
    states = (fkvt.reshape(bsz, 2, GROUP_HEADS, HEAD_DIM, t).transpose(0, 4, 1, 2, 3), misc[:, :, 0:N_FOX_F],
              nkvt.reshape(bsz, 4, HEAD_DIM, t).transpose(0, 3, 1, 2),
              wint[:, :, t - win_keep:].reshape(bsz, 2, HEAD_DIM, win_keep).transpose(0, 3, 1, 2),
              cst[:, 8 - (CONV_W - 1):])
    return x_out, states, (ya, yb, yc, yd)


def _sample_consts(p, n_pages, page, tn):
    utri = np.triu(np.ones((page, page), np.float32))
    ones = np.ones((page, page), np.float32)
    r = np.arange(GROUP_HEADS * n_pages)
    slt = np.zeros((GROUP_HEADS * n_pages, LANES), np.float32)
    slt[:, :r.size] = ((r[:, None] % GROUP_HEADS) == (r[None, :] % GROUP_HEADS)) & (r[None, :] // GROUP_HEADS < r[:, None] // GROUP_HEADS)
    tri8 = np.zeros((tn, LANES), np.float32)
    tri8[:, :tn] = np.tril(np.ones((tn, tn), np.float32))
    t_all = n_pages * page + tn
    n_sel = -(-t_all // SEL_LEN)
    ov = _sel_overlap(t_all, n_sel)
    n_grp = n_pages * page // CMP_STRIDE
    assert ov.shape[0] <= n_grp and n_sel <= LANES
    selw = np.zeros((n_grp, LANES), np.float32)
    selw[:ov.shape[0], :n_sel] = ov
    n_keys = (n_pages + 1) * page
    expand = np.zeros((LANES, n_keys), np.float32)
    expand[np.arange(n_keys) // SEL_LEN, np.arange(n_keys)] = 1.0
    as_b = lambda a: jnp.asarray(np.asarray(a, np.float32), bf16)
    return [p['fb_row'], p['wst'], p['bs_full'], p['cw8'], _head_avg(), as_b(utri), as_b(ones), as_b(slt),
            as_b(tri8), p['pe32'], p['w1g'], p['w2c'], as_b(selw), as_b(expand)]


def _sample_layer(x, views, page_table, layer, p, wr_t, alpha, tm=512):
    bsz, tn, d = x.shape
    n = bsz * tn
    fox_t, lf_t, nsa_t, win_t, conv4 = views
    n_pages = page_table.shape[1]
    page = fox_t.shape[3]
    assert tn == 8 and page == LANES and win_t.shape[3] == WINDOW and n_pages * page >= WINDOW
    assert GROUP_HEADS * n_pages <= LANES
    h = _sample_inproj(x.reshape(n, d), p['w_pack'], tm).reshape(bsz, tn, P_PACK)
    y, fkv, misc, nrow, wout, cout, gv = _sample_mixer(h, fox_t, lf_t, nsa_t, win_t, conv4, page_table, layer,
                                                       _sample_consts(p, n_pages, page, tn))
    x_out = _post_block(x.reshape(n, d), [y.reshape(n, d)], p, wr_t, alpha, tm).reshape(bsz, tn, d)
    win_state = wout.reshape(bsz, 2, HEAD_DIM, WINDOW).transpose(0, 3, 1, 2)
    states = (fkv.reshape(bsz, tn, 2, GROUP_HEADS, HEAD_DIM), misc[:, :, 0:N_FOX_F], nrow.reshape(bsz, tn, 4, HEAD_DIM),
              win_state, cout[:, tn - (CONV_W - 1):], gv.reshape(bsz, tn, GROUP_HEADS, HEAD_DIM))
    return x_out, states


def kernel(x_prompt, x_sample, cache_fox_kv, cache_fox_logf, cache_nsa_kv, state_nsa_win, state_conv, page_table,
           w_in, fox_b_f, gmlp_w_s, gmlp_b_s, nsa_cmp_pe, nsa_cmp_w1, nsa_cmp_w2, conv_w, w_out, ln_g, ln_b,
           w_router, w_exp_gate, w_exp_up, w_exp_down):
    depth, n_pool, page = cache_fox_kv.shape[0:3]
    dec_b = x_sample.shape[0]
    alpha = (2 * depth) ** 0.25
    wr_t = w_router.T.astype(bf16)
    views = (cache_fox_kv.transpose(0, 1, 3, 4, 5, 2).reshape(depth, n_pool, 2 * GROUP_W, page),
             cache_fox_logf.transpose(0, 1, 3, 2),
             cache_nsa_kv.transpose(0, 1, 3, 4, 2).reshape(depth, n_pool, 4 * HEAD_DIM, page),
             state_nsa_win.transpose(0, 1, 3, 4, 2).reshape(depth, dec_b, 2 * HEAD_DIM, state_nsa_win.shape[2]),
             state_conv)
    xp, xs = x_prompt, x_sample
    st_p = [[] for _ in range(5)]
    st_s = [[] for _ in range(6)]
    for l in range(depth):
        p = _prep_layer(l, w_in, fox_b_f, gmlp_w_s, gmlp_b_s, nsa_cmp_pe, nsa_cmp_w1, nsa_cmp_w2, conv_w, w_out,
                        ln_g, ln_b, w_exp_gate, w_exp_up, w_exp_down)
        xp, sp, _ = _prompt_layer(xp, p, wr_t, alpha)
        xs, ss = _sample_layer(xs, views, page_table, l, p, wr_t, alpha)
        for lst, a in zip(st_p, sp):
            lst.append(a)
        for lst, a in zip(st_s, ss):
            lst.append(a)
    fox_kv_p, fox_lf_p, nsa_kv_p, nsa_win_p, conv_p = [jnp.stack(a, axis=0) for a in st_p]
    fox_kv_s, fox_lf_s, nsa_kv_s, nsa_win_s, conv_s, gmlp_v_s = [jnp.stack(a, axis=0) for a in st_s]
    return (xp, xs, fox_kv_p, fox_kv_s, fox_lf_p, fox_lf_s, nsa_kv_p, nsa_kv_s, nsa_win_p, nsa_win_s, conv_p, conv_s,
            gmlp_v_s)
```

```python
import functools

import numpy as np
import jax
import jax.numpy as jnp
from jax import lax
from jax.experimental import pallas as pl
from jax.experimental.pallas import tpu as pltpu

f32 = jnp.float32
bf16 = jnp.bfloat16

HEAD_DIM = 64
GROUP_W = 256
GROUP_HEADS = 4
CHUNK = 128
Q_BLOCK = 128
CMP_LEN = 32
CMP_STRIDE = 16
CMP_HID = 128
SEL_LEN = 64
SEL_TOPK = 16
WINDOW = 512
CONV_W = 3
N_EXPERTS = 16
N_GROUPS = 4
EXP_PER_GROUP = 4
TOP_K = 2
LN_EPS = 1e-5
SCALE = HEAD_DIM ** -0.5
NEG = -1e30
BIG = 1e30
LANES = 128

AU, AV, BQ, BK, BV, CQ, CKV, DH, DB, DC, MISC = 0, 256, 512, 768, 1024, 1280, 1536, 1920, 2176, 2432, 2688
P_PACK = 2816
N_FOX_F = GROUP_HEADS
N_GATES = 3 * GROUP_HEADS

VMEM_LIMIT = 56 * 1024 * 1024


def _cparams(sem):
    return pltpu.CompilerParams(dimension_semantics=sem, vmem_limit_bytes=VMEM_LIMIT)


def _dot(a, b):
    return jnp.dot(a, b, preferred_element_type=f32)


def _dot_nt(a, b):
    return lax.dot_general(a, b, (((1,), (1,)), ((), ())), preferred_element_type=f32)


def _split3(x):
    hi = x.astype(bf16)
    r = x - hi.astype(f32)
    mid = r.astype(bf16)
    lo = (r - mid.astype(f32)).astype(bf16)
    return hi, mid, lo


def _dot_exact_rhs(a, b3):
    return _dot(a, b3[0]) + _dot(a, b3[1]) + _dot(a, b3[2])


def _dot_exact_lhs(a3, b):
    return _dot(a3[0], b) + _dot(a3[1], b) + _dot(a3[2], b)


def _log_sigmoid(x):
    return jnp.minimum(x, 0.0) - jnp.log1p(jnp.exp(-jnp.abs(x)))


def _gelu(x):
    return jax.nn.gelu(x, approximate=True)


def _layer_norm_rows(x, g, b):
    mu = jnp.mean(x, axis=-1, keepdims=True)
    d = x - mu
    var = jnp.mean(d * d, axis=-1, keepdims=True)
    return d * lax.rsqrt(var + LN_EPS) * g + b


def _head_layer_norm(g, avg):
    h2 = g.astype(bf16)
    mu = _dot(h2, avg) + _dot((g - h2.astype(f32)).astype(bf16), avg)
    d = g - mu
    dd = d * d
    d2 = dd.astype(bf16)
    var = _dot(d2, avg) + _dot((dd - d2.astype(f32)).astype(bf16), avg)
    return d * lax.rsqrt(var + LN_EPS)


def _head_diag(z_all, rows):
    lane_head = lax.broadcasted_iota(jnp.int32, (rows, GROUP_W), 1) // HEAD_DIM
    out = jnp.where(lane_head == 0, z_all[0:rows], 0.0)
    for h in range(1, GROUP_HEADS):
        out = out + jnp.where(lane_head == h, z_all[h * rows:(h + 1) * rows], 0.0)
    return out


def _head_blockdiag_q(q, rows):
    lane_head = lax.broadcasted_iota(jnp.int32, (rows, GROUP_W), 1) // HEAD_DIM
    return jnp.concatenate([jnp.where(lane_head == h, q, jnp.zeros_like(q)) for h in range(GROUP_HEADS)], axis=0)


def _prompt_inproj_kernel(x_ref, exp_ref, w_ref, wt_ref, fb_ref, wst_ref, bs_ref, cw_ref, avg_ref, tri_ref,
                          ya_ref, fq_ref, fk_ref, fvt_ref, fkvt_ref, misc_ref, ccol_ref, crow_ref,
                          nq_ref, nkvt_ref, wint_ref, kse_ref, featt_ref, kcvc_ref, yd_ref, cst_ref,
                          carry_c, carry_z, *, tm):
    ti = pl.program_id(1)

    @pl.when(ti == 0)
    def _():
        carry_c[...] = jnp.zeros_like(carry_c)
        carry_z[...] = jnp.zeros_like(carry_z)

    xb = x_ref[0].astype(bf16)

    u = _gelu(_dot(xb, w_ref[:, AU:AU + GROUP_W]))
    vn = _head_layer_norm(_gelu(_dot(xb, w_ref[:, AV:AV + GROUP_W])), avg_ref[...])
    wst = wst_ref[...]
    bsf = bs_ref[...]
    for c in range(tm // CHUNK):
        r0 = c * CHUNK
        z_all = _dot(wst, vn[r0:r0 + CHUNK].astype(bf16))
        ya_ref[0, r0:r0 + CHUNK, :] = (u[r0:r0 + CHUNK] * (_head_diag(z_all, CHUNK) + bsf)).astype(bf16)

    fq_ref[0] = (_dot(xb, w_ref[:, BQ:BQ + GROUP_W]) * SCALE).astype(bf16)
    res_t = _dot_nt(wt_ref[...], xb)
    fk_ref[0] = jnp.transpose(res_t[0:GROUP_W]).astype(bf16)
    fkvt_ref[0] = res_t[0:2 * GROUP_W]
    fvt_ref[0] = res_t[GROUP_W:2 * GROUP_W].astype(bf16)
    nkvt_ref[0] = res_t[2 * GROUP_W:3 * GROUP_W]
    wint_ref[0] = res_t[3 * GROUP_W:3 * GROUP_W + 2 * HEAD_DIM]
    featt_ref[0] = jnp.concatenate([res_t[3 * GROUP_W - HEAD_DIM:3 * GROUP_W],
                                    res_t[3 * GROUP_W + HEAD_DIM:3 * GROUP_W + 2 * HEAD_DIM]], axis=0).astype(bf16)
    pre =_dot(xb, w_ref[:, MISC:MISC + LANES]) + fb_ref[0:1, :]
    lane = lax.broadcasted_iota(jnp.int32, (tm, LANES), 1)
    lf = jnp.where(lane < N_FOX_F, _log_sigmoid(pre), 0.0)
    misc_ref[0] = jnp.where(lane < N_FOX_F, lf, jnp.where(lane < N_FOX_F + N_GATES, jax.nn.sigmoid(pre), 0.0))
    c_t = _dot_exact_lhs(_split3(jnp.transpose(lf)[0:8, :]), tri_ref[...]) + carry_c[:, 0:1]
    carry_c[...] = jnp.broadcast_to(c_t[:, tm - 1:tm], (8, LANES))
    crow_ref[0] = c_t
    ccol_ref[0] = jnp.transpose(jnp.concatenate([c_t, jnp.zeros((LANES - 8, tm), f32)], axis=0))

    nq_ref[0] = (_dot(xb, w_ref[:, CQ:CQ + GROUP_W]) * SCALE).astype(bf16)
    nsa0 = 2 * GROUP_W
    kcvc_ref[0] = jnp.transpose(res_t[nsa0:nsa0 + 2 * HEAD_DIM])
    ks_kw = jnp.transpose(jnp.concatenate([res_t[nsa0 + 2 * HEAD_DIM:nsa0 + 3 * HEAD_DIM],
                                           res_t[nsa0 + 4 * HEAD_DIM:nsa0 + 5 * HEAD_DIM]], axis=0))
    kse_ref[0] = jnp.concatenate([ks_kw.astype(bf16), exp_ref[...]], axis=-1)

    z = _dot(xb, w_ref[:, DC:DC + GROUP_W]) * _dot(xb, w_ref[:, DH:DH + GROUP_W])
    row = lax.broadcasted_iota(jnp.int32, (tm, GROUP_W), 0)
    zp = carry_z[...]
    z1 = jnp.where(row == 0, zp[7:8, :], pltpu.roll(z, 1, 0))
    z2 = jnp.where(row == 0, zp[6:7, :], jnp.where(row == 1, zp[7:8, :], pltpu.roll(z, 2, 0)))
    cw = cw_ref[...]
    y_conv = cw[0:1, :] * z2 + cw[1:2, :] * z1 + cw[2:3, :] * z
    yd_ref[0] = (_dot(xb, w_ref[:, DB:DB + GROUP_W]) * y_conv).astype(bf16)
    carry_z[...] = z[tm - 8:tm, :]
    cst_ref[0] = z[tm - 8:tm, :]


def _prompt_inproj(x, expand_t, w_pack, w_t, fb_row, wst, bs_full, cw8, avg, tri, tm):
    bsz, t, d = x.shape
    nt = t // tm
    tok = lambda w, dt: (jax.ShapeDtypeStruct((bsz, t, w), dt), pl.BlockSpec((1, tm, w), lambda b, i: (b, i, 0)))
    feat = lambda r, dt: (jax.ShapeDtypeStruct((bsz, r, t), dt), pl.BlockSpec((1, r, tm), lambda b, i: (b, 0, i)))
    const = lambda a: pl.BlockSpec(a.shape, lambda b, i: (0,) * a.ndim)
    outs = [
        tok(GROUP_W, bf16),
        tok(GROUP_W, bf16),
        tok(GROUP_W, bf16),
        feat(GROUP_W, bf16),
        feat(2 * GROUP_W, f32),
        tok(LANES, f32),
        tok(LANES, f32),
        feat(8, f32),
        tok(GROUP_W, bf16),
        feat(4 * HEAD_DIM, f32),
        feat(2 * HEAD_DIM, f32),
        tok(GROUP_W, bf16),
        feat(2 * HEAD_DIM, bf16),
        tok(2 * HEAD_DIM, f32),
        tok(GROUP_W, bf16),
        (jax.ShapeDtypeStruct((bsz, 8, GROUP_W), f32), pl.BlockSpec((1, 8, GROUP_W), lambda b, i: (b, 0, 0))),
    ]
    consts = [w_pack, w_t, fb_row, wst, bs_full, cw8, avg, tri]
    return pl.pallas_call(
        functools.partial(_prompt_inproj_kernel, tm=tm),
        grid=(bsz, nt),
        in_specs=[pl.BlockSpec((1, tm, d), lambda b, i: (b, i, 0)), pl.BlockSpec((tm, LANES), lambda b, i: (i, 0))]
        + [const(a) for a in consts],
        out_specs=[o[1] for o in outs],
        out_shape=[o[0] for o in outs],
        scratch_shapes=[pltpu.VMEM((8, LANES), f32), pltpu.VMEM((8, GROUP_W), f32)],
        compiler_params=_cparams(("arbitrary", "arbitrary")),
    )(x, expand_t, *consts)


def _flash_update_t(chains):
    parts = []
    for s, v_t, m_sc, l_sc, acc_sc in chains:
        m_old = m_sc[...]
        m_new = jnp.maximum(m_old, jnp.max(s, axis=0, keepdims=True))
        alpha = jnp.exp(m_old - m_new)
        p = jnp.exp(s - m_new)
        l_sc[...] = alpha * l_sc[...] + jnp.sum(p, axis=0, keepdims=True)
        m_sc[...] = m_new
        parts.append((alpha, p.astype(bf16)))
    for (s, v_t, m_sc, l_sc, acc_sc), (alpha, pb) in zip(chains, parts):
        acc_sc[...] = alpha * acc_sc[...] + _dot(v_t, pb)


def _fox_prompt_kernel(q_ref, k_ref, vt_ref, ccol_ref, crow_ref, sel_ref, o_ref, crep_sc, m_sc, l_sc, acc_sc, *, tk, bq):
    qi = pl.program_id(1)
    n_pair = GROUP_HEADS // 2
    pw = 2 * HEAD_DIM

    @pl.when(qi == 0)
    def _():
        c3 = _split3(ccol_ref[0])
        for h in range(GROUP_HEADS):
            crep_sc[h] = _dot_exact_lhs(c3, sel_ref[h])

    q = q_ref[0]
    lane_hi = lax.broadcasted_iota(jnp.int32, (bq, pw), 1) >= HEAD_DIM
    cr = crow_ref[0]
    qpos = qi * bq + lax.broadcasted_iota(jnp.int32, (1, bq), 1)
    qpos = jnp.concatenate([qpos, qpos], axis=1)
    qbd = []
    cq = []
    for g in range(n_pair):
        qg = q[:, g * pw:(g + 1) * pw]
        qbd.append(jnp.concatenate([jnp.where(lane_hi, jnp.zeros_like(qg), qg),
                                    jnp.where(lane_hi, qg, jnp.zeros_like(qg))], axis=0))
        cq.append(jnp.concatenate([cr[2 * g:2 * g + 1, :], cr[2 * g + 1:2 * g + 2, :]], axis=1))
    m_sc[...] = jnp.full_like(m_sc, NEG)
    l_sc[...] = jnp.zeros_like(l_sc)
    acc_sc[...] = jnp.zeros_like(acc_sc)

    def tile(j, masked):
        k0 = pl.multiple_of(j * tk, tk)
        chains = []
        for g in range(n_pair):
            k = k_ref[0, pl.ds(k0, tk), g * pw:(g + 1) * pw]
            v_t = vt_ref[0, g * pw:(g + 1) * pw, pl.ds(k0, tk)]
            ck = jnp.concatenate([crep_sc[2 * g, pl.ds(k0, tk), :], crep_sc[2 * g + 1, pl.ds(k0, tk), :]], axis=1)
            s = _dot_nt(k, qbd[g]) + cq[g] - ck
            if masked:
                kpos = k0 + lax.broadcasted_iota(jnp.int32, (tk, 1), 0)
                s = jnp.where(kpos <= qpos, s, NEG)
            chains.append((s, v_t, m_sc.at[g], l_sc.at[g], acc_sc.at[g]))
        _flash_update_t(chains)

    n_full = (qi * bq) // tk

    def body(j, carry):
        tile(j, False)
        return carry

    lax.fori_loop(0, n_full, body, 0)
    tile(n_full, True)
    outs = []
    for g in range(n_pair):
        o = acc_sc[g] / l_sc[g]
        outs.append(o[0:HEAD_DIM, 0:bq])
        outs.append(o[HEAD_DIM:pw, bq:2 * bq])
    o_ref[0] = jnp.transpose(jnp.concatenate(outs, axis=0)).astype(bf16)


def _fox_prompt(fq, fk, fvt, ccol, crow, tk, bq):
    bsz, t, _ = fq.shape
    assert tk % bq == 0 and t % tk == 0 and bq == LANES
    sel = np.zeros((GROUP_HEADS, LANES, LANES), np.float32)
    for h in range(GROUP_HEADS):
        sel[h, h, :] = 1.0
    sel = jnp.asarray(sel, bf16)
    return pl.pallas_call(
        functools.partial(_fox_prompt_kernel, tk=tk, bq=bq),
        grid=(bsz, t // bq),
        in_specs=[
            pl.BlockSpec((1, bq, GROUP_W), lambda b, i: (b, i, 0)),
            pl.BlockSpec((1, t, GROUP_W), lambda b, i: (b, 0, 0)),
            pl.BlockSpec((1, GROUP_W, t), lambda b, i: (b, 0, 0)),
            pl.BlockSpec((1, t, LANES), lambda b, i: (b, 0, 0)),
            pl.BlockSpec((1, 8, bq), lambda b, i: (b, 0, i)),
            pl.BlockSpec(sel.shape, lambda b, i: (0, 0, 0)),
        ],
        out_specs=pl.BlockSpec((1, bq, GROUP_W), lambda b, i: (b, i, 0)),
        out_shape=jax.ShapeDtypeStruct((bsz, t, GROUP_W), bf16),
        scratch_shapes=[pltpu.VMEM((GROUP_HEADS, t, LANES), f32), pltpu.VMEM((2, 1, 2 * bq), f32),
                        pltpu.VMEM((2, 1, 2 * bq), f32), pltpu.VMEM((2, 2 * HEAD_DIM, 2 * bq), f32)],
        compiler_params=_cparams(("arbitrary", "arbitrary")),
    )(fq, fk, fvt, ccol, crow, sel)


def _compress_kernel(x_ref, pe_ref, w1_ref, w2_ref, o_ref, ot_ref):
    x = x_ref[0]
    n_grp = x.shape[0]
    top = _dot((x + pe_ref[0:1, :]).astype(bf16), w1_ref[0])
    bot = _dot((x + pe_ref[1:2, :]).astype(bf16), w1_ref[1])
    g = _gelu(top + pltpu.roll(bot, n_grp - 1, 0)).astype(bf16)
    w2 = w2_ref[...]
    kc = _dot(g[:, 0:CMP_HID], w2[:, 0:HEAD_DIM])
    vc = _dot(g[:, CMP_HID:2 * CMP_HID], w2[:, HEAD_DIM:2 * HEAD_DIM])
    cmp = jnp.concatenate([kc, vc], axis=-1)
    o_ref[0] = cmp.astype(bf16)
    ot_ref[0] = jnp.transpose(cmp).astype(bf16)


def _compress(x_groups, pe2, w1g, w2c):
    bsz, n_grp, w = x_groups.shape
    const = lambda a: pl.BlockSpec(a.shape, lambda b: (0,) * a.ndim)
    return pl.pallas_call(
        _compress_kernel,
        grid=(bsz,),
        in_specs=[pl.BlockSpec((1, n_grp, w), lambda b: (b, 0, 0)), const(pe2), const(w1g), const(w2c)],
        out_specs=[pl.BlockSpec((1, n_grp, 2 * HEAD_DIM), lambda b: (b, 0, 0)),
                   pl.BlockSpec((1, 2 * HEAD_DIM, n_grp), lambda b: (b, 0, 0))],
        out_shape=[jax.ShapeDtypeStruct((bsz, n_grp, 2 * HEAD_DIM), bf16),
                   jax.ShapeDtypeStruct((bsz, 2 * HEAD_DIM, n_grp), bf16)],
        compiler_params=_cparams(("arbitrary",)),
    )(x_groups, pe2, w1g, w2c)


def _stack_heads(q, rows):
    return jnp.concatenate([q[:, h * HEAD_DIM:(h + 1) * HEAD_DIM] for h in range(GROUP_HEADS)], axis=0)


def _unstack_heads(o, rows):
    return jnp.concatenate([o[h * rows:(h + 1) * rows] for h in range(GROUP_HEADS)], axis=-1)


def _gate_col(misc, branch):
    base = N_FOX_F + branch * GROUP_HEADS
    return jnp.concatenate([misc[:, base + h:base + h + 1] for h in range(GROUP_HEADS)], axis=0)


def _nsa_compressed(qs, cmp, qpos_s, selw, nq):
    n_c = cmp.shape[0]
    kc = cmp[:, 0:HEAD_DIM]
    vc = cmp[:, HEAD_DIM:2 * HEAD_DIM]
    s = _dot_nt(qs, kc)
    cend = lax.broadcasted_iota(jnp.int32, (1, n_c), 1) * CMP_STRIDE + (CMP_LEN - 1)
    vis = cend <= qpos_s
    s = jnp.where(vis, s, NEG)
    e = jnp.exp(s - jnp.max(s, axis=-1, keepdims=True))
    p = e / jnp.sum(e, axis=-1, keepdims=True)
    p = p * jnp.where(qpos_s >= CMP_LEN - 1, 1.0, 0.0)
    pb = p.astype(bf16)
    o_c = _dot(pb, vc)
    imp = _dot(pb[0:nq], selw)
    for h in range(1, GROUP_HEADS):
        imp = imp + _dot(pb[h * nq:(h + 1) * nq], selw)
    return o_c, imp


def _nsa_select(imp, qpos_row, n_sel_pad):
    nq = imp.shape[0]
    if nq < LANES:
        imp = jnp.concatenate([imp, jnp.zeros((LANES - nq, LANES), f32)], axis=0)
    sel_t = _nsa_select_t(jnp.transpose(imp)[0:n_sel_pad], qpos_row)
    if n_sel_pad < LANES:
        sel_t = jnp.concatenate([sel_t, jnp.zeros((LANES - n_sel_pad, LANES), f32)], axis=0)
    return jnp.transpose(sel_t)[0:nq]


def _nsa_select_t(imp_t, qpos_row):
    n_sel_pad = imp_t.shape[0]
    blk = lax.broadcasted_iota(jnp.int32, (n_sel_pad, LANES), 0)
    cur = qpos_row // SEL_LEN
    valid = blk * SEL_LEN <= qpos_row
    forced = (blk == 0) | (blk == cur) | (blk == cur - 1)
    val = jnp.where(valid, jnp.where(forced, BIG, imp_t), -BIG)
    n_g = n_sel_pad // 8
    grp = [val[8 * g:8 * g + 8] for g in range(n_g)]
    rank = [jnp.zeros((8, LANES), f32) for _ in range(n_g)]
    row8 = lax.broadcasted_iota(jnp.int32, (8, LANES), 0)
    for mp in range(n_sel_pad):
        vb = jnp.broadcast_to(val[mp:mp + 1, :], (8, LANES))
        for g in range(n_g):
            if 8 * g + 7 < mp:
                hit = jnp.where(vb > grp[g], 1.0, 0.0)
            elif 8 * g > mp:
                hit = jnp.where(vb >= grp[g], 1.0, 0.0)
            else:
                hit = jnp.where(row8 > mp - 8 * g, jnp.where(vb >= grp[g], 1.0, 0.0), jnp.where(vb > grp[g], 1.0, 0.0))
            rank[g] = rank[g] + hit
    rank = jnp.concatenate(rank, axis=0)
    return jnp.where(valid, jnp.where(rank < SEL_TOPK, 1.0, 0.0), 0.0)


def _nsa_prompt_kernel(q_ref, kse_ref, ft_ref, cmp_ref, cmpt_ref, misc_ref, selwt_ref, o_ref, m_sc, l_sc, acc_sc, *,
                       tk, t, n_sel_pad, bq):
    qi = pl.program_id(1)
    r_all = GROUP_HEADS * bq
    q_t4 = jnp.transpose(q_ref[0].astype(f32))
    q_t = jnp.concatenate([q_t4[h * HEAD_DIM:(h + 1) * HEAD_DIM] for h in range(GROUP_HEADS)], axis=1).astype(bf16)
    zq = jnp.zeros((HEAD_DIM, r_all), bf16)
    qpos1 = qi * bq + lax.broadcasted_iota(jnp.int32, (1, bq), 1)
    qpos = jnp.concatenate([qpos1] * GROUP_HEADS, axis=1)

    span = min(WINDOW + bq, t)
    w0 = pl.multiple_of(jnp.maximum(qi * bq - WINDOW, 0), LANES)
    q_win = jnp.concatenate([zq, q_t, jnp.zeros((LANES, r_all), bf16)], axis=0)
    cmp = cmp_ref[0]
    n_c = cmp.shape[0]
    s_w = _dot(kse_ref[0, pl.ds(w0, span), :], q_win)
    s_c = _dot(cmp, jnp.concatenate([q_t, zq], axis=0))
    wpos = w0 + lax.broadcasted_iota(jnp.int32, (span, 1), 0)
    s_w = jnp.where((wpos <= qpos) & (wpos > qpos - WINDOW), s_w, NEG)
    cend = lax.broadcasted_iota(jnp.int32, (n_c, 1), 0) * CMP_STRIDE + (CMP_LEN - 1)
    s_c = jnp.where(cend <= qpos, s_c, NEG)
    e_c = jnp.exp(s_c - jnp.max(s_c, axis=0, keepdims=True))
    p_c = e_c / jnp.sum(e_c, axis=0, keepdims=True)
    pb = (p_c * jnp.where(qpos >= CMP_LEN - 1, 1.0, 0.0)).astype(bf16)
    e_w = jnp.exp(s_w - jnp.max(s_w, axis=0, keepdims=True))
    p_w = (e_w / jnp.sum(e_w, axis=0, keepdims=True)).astype(bf16)
    o_c = _dot(cmpt_ref[0, HEAD_DIM:2 * HEAD_DIM, :], pb)
    o_w = _dot(ft_ref[0, HEAD_DIM:2 * HEAD_DIM, pl.ds(w0, span)], p_w)
    selwt = selwt_ref[...]
    imp_t = _dot(selwt, pb[:, 0:bq])
    for h in range(1, GROUP_HEADS):
        imp_t = imp_t + _dot(selwt, pb[:, h * bq:(h + 1) * bq])
    sel_t = _nsa_select_t(imp_t[0:n_sel_pad], qpos1)
    unsel = jnp.where(sel_t > 0.5, 0.0, -(2.0 ** 100))
    if n_sel_pad < LANES:
        unsel = jnp.concatenate([unsel, jnp.zeros((LANES - n_sel_pad, bq), f32)], axis=0)
    unsel = jnp.concatenate([unsel.astype(bf16)] * GROUP_HEADS, axis=1)

    q_sel = jnp.concatenate([q_t, zq, unsel], axis=0)
    m_sc[...] = jnp.full_like(m_sc, NEG)
    l_sc[...] = jnp.zeros_like(l_sc)
    acc_sc[...] = jnp.zeros_like(acc_sc)

    n_chain = 2
    rc = r_all // n_chain

    def scores(j):
        kse = kse_ref[0, pl.ds(pl.multiple_of(j * tk, tk), tk), :]
        return tuple(_dot(kse, q_sel[:, c * rc:(c + 1) * rc]) for c in range(n_chain))

    def consume(j, s, diagonal):
        k0 = pl.multiple_of(j * tk, tk)
        v_t = ft_ref[0, 0:HEAD_DIM, pl.ds(k0, tk)]
        chains = []
        for c in range(n_chain):
            sc = s[c]
            if diagonal:
                kpos = k0 + lax.broadcasted_iota(jnp.int32, (tk, 1), 0)
                sc = jnp.where(kpos <= qpos[:, c * rc:(c + 1) * rc], sc, NEG)
            chains.append((sc, v_t, m_sc.at[c], l_sc.at[c], acc_sc.at[c]))
        _flash_update_t(chains)

    n_full = (qi * bq) // tk

    def body(j, s):
        s_next = scores(j + 1)
        consume(j, s, False)
        return s_next

    s_last = lax.fori_loop(0, n_full, body, scores(0))
    consume(n_full, s_last, True)
    o_s = jnp.concatenate([acc_sc[c] / l_sc[c] for c in range(n_chain)], axis=1)

    misc_t = jnp.transpose(misc_ref[0])

    def gate(branch):
        base = N_FOX_F + branch * GROUP_HEADS
        return jnp.concatenate([misc_t[base + h:base + h + 1, :] for h in range(GROUP_HEADS)], axis=1)

    o = gate(0) * o_c + gate(1) * o_s + gate(2) * o_w
    o = jnp.concatenate([o[:, h * bq:(h + 1) * bq] for h in range(GROUP_HEADS)], axis=0)
    o_ref[0] = jnp.transpose(o).astype(bf16)


def _nsa_prompt(nq_b, kse, feat_t, cmp, cmp_t, misc, selw_t, tk, bq):
    bsz, t, _ = nq_b.shape
    r_all = GROUP_HEADS * bq
    n_c = cmp.shape[1]
    n_sel_pad = -(-(t // SEL_LEN) // 8) * 8
    assert tk % bq == 0 and bq == LANES
    return pl.pallas_call(
        functools.partial(_nsa_prompt_kernel, tk=tk, t=t, n_sel_pad=n_sel_pad, bq=bq),
        grid=(bsz, t // bq),
        in_specs=[
            pl.BlockSpec((1, bq, GROUP_W), lambda b, i: (b, i, 0)),
            pl.BlockSpec((1, t, GROUP_W), lambda b, i: (b, 0, 0)),
            pl.BlockSpec((1, 2 * HEAD_DIM, t), lambda b, i: (b, 0, 0)),
            pl.BlockSpec((1, n_c, 2 * HEAD_DIM), lambda b, i: (b, 0, 0)),
            pl.BlockSpec((1, 2 * HEAD_DIM, n_c), lambda b, i: (b, 0, 0)),
            pl.BlockSpec((1, bq, LANES), lambda b, i: (b, i, 0)),
            pl.BlockSpec(selw_t.shape, lambda b, i: (0, 0)),
        ],
        out_specs=pl.BlockSpec((1, bq, GROUP_W), lambda b, i: (b, i, 0)),
        out_shape=jax.ShapeDtypeStruct((bsz, t, GROUP_W), bf16),
        scratch_shapes=[pltpu.VMEM((2, 1, r_all // 2), f32), pltpu.VMEM((2, 1, r_all // 2), f32),
                        pltpu.VMEM((2, HEAD_DIM, r_all // 2), f32)],
        compiler_params=_cparams(("arbitrary", "arbitrary")),
    )(nq_b, kse, feat_t, cmp, cmp_t, misc, selw_t)


def _matmul_kernel(x_ref, w_ref, o_ref):
    o_ref[...] = _dot(x_ref[...].astype(bf16), w_ref[...])


def _sample_inproj(x2, w_pack, tm):
    n, d = x2.shape
    pw = w_pack.shape[1]
    return pl.pallas_call(
        _matmul_kernel,
        grid=(n // tm,),
        in_specs=[pl.BlockSpec((tm, d), lambda i: (i, 0)), pl.BlockSpec((d, pw), lambda i: (0, 0))],
        out_specs=pl.BlockSpec((tm, pw), lambda i: (i, 0)),
        out_shape=jax.ShapeDtypeStruct((n, pw), f32),
        compiler_params=_cparams(("arbitrary",)),
    )(x2, w_pack)


def _pad_to(a, rows):
    return jnp.concatenate([a, jnp.zeros((rows - a.shape[0], a.shape[1]), a.dtype)], axis=0)


def _softmax_tiles(tiles):
    m = tiles[0].max(axis=-1, keepdims=True)
    for s in tiles[1:]:
        m = jnp.maximum(m, s.max(axis=-1, keepdims=True))
    ps = [jnp.exp(s - m) for s in tiles]
    tot = ps[0].sum(axis=-1, keepdims=True)
    for p in ps[1:]:
        tot = tot + p.sum(axis=-1, keepdims=True)
    return ps, tot


def _sample_mixer_kernel(pt_ref, h_ref, *refs, n_pages, page, past, n_sel_pad):
    del pt_ref
    fox_pages = refs[0:n_pages]
    lf_pages = refs[n_pages:2 * n_pages]
    nsa_pages = refs[2 * n_pages:3 * n_pages]
    (win_ref, conv_ref, fb_ref, wst_ref, bs_ref, cw_ref, avg_ref, utri_ref, ones_ref, slt_ref, tri8_ref,
     pe_ref, w1_ref, w2_ref, selw_ref, exp_ref,
     y_ref, fkv_ref, misc_ref, nrow_ref, wout_ref, cout_ref, gv_ref, rows_sc) = refs[3 * n_pages:]
    tn = h_ref.shape[1]
    h = h_ref[0]
    qpos = past + lax.broadcasted_iota(jnp.int32, (tn, 1), 0)
    qpos_s = jnp.concatenate([qpos] * GROUP_HEADS, axis=0)
    new_pos = past + lax.broadcasted_iota(jnp.int32, (1, LANES), 1)
    new_ok = new_pos <= qpos_s

    ckv = h[:, CKV:CKV + 6 * HEAD_DIM]
    nrow_ref[0] = ckv[:, 0:4 * HEAD_DIM]
    for pg in range(n_pages):
        rows_sc[pg * page:(pg + 1) * page, :] = jnp.transpose(nsa_pages[pg][0, 0, 0:2 * HEAD_DIM, :])
    n_grp = n_pages * page // CMP_STRIDE
    half = CMP_LEN // 2
    tops = []
    bots = []
    for r in range(half):
        a_r = rows_sc[pl.ds(r, n_grp, stride=CMP_STRIDE), :]
        tops.append((a_r + pe_ref[r:r + 1, :]).astype(bf16))
        bots.append((a_r + pe_ref[half + r:half + r + 1, :]).astype(bf16))
    top = _dot(jnp.concatenate(tops, axis=1), w1_ref[0])
    bot = _dot(jnp.concatenate(bots, axis=1), w1_ref[1])
    g = _gelu(top + pltpu.roll(bot, n_grp - 1, 0)).astype(bf16)
    w2 = w2_ref[...]
    cmp = jnp.concatenate([_dot(g[:, 0:CMP_HID], w2[:, 0:HEAD_DIM]),
                           _dot(g[:, CMP_HID:2 * CMP_HID], w2[:, HEAD_DIM:2 * HEAD_DIM])], axis=-1).astype(bf16)
    qs = _stack_heads((h[:, CQ:CQ + GROUP_W] * SCALE).astype(bf16), tn)
    o_c, imp = _nsa_compressed(qs, cmp, qpos_s, selw_ref[...], tn)
    sel = _nsa_select(imp, new_pos, n_sel_pad).astype(bf16)

    pre = h[:, MISC:MISC + LANES] + fb_ref[0:1, :]
    lane = lax.broadcasted_iota(jnp.int32, (tn, LANES), 1)
    lf_new = jnp.where(lane < N_FOX_F, _log_sigmoid(pre), 0.0)
    misc = jnp.where(lane < N_FOX_F, lf_new, jnp.where(lane < N_FOX_F + N_GATES, jax.nn.sigmoid(pre), 0.0))
    misc_ref[0] = misc

    lf3 = _split3(jnp.concatenate([r[0, 0] for r in lf_pages], axis=0))
    tot = _dot_exact_lhs(lf3, ones_ref[...])
    off = _dot_exact_rhs(slt_ref[...], _split3(_pad_to(tot, LANES)))
    c_past = _dot_exact_lhs(lf3, utri_ref[...]) + off
    last = GROUP_HEADS * (n_pages - 1)
    total_past = off[last:last + GROUP_HEADS] + tot[last:last + GROUP_HEADS]
    c_new = jnp.transpose(_pad_to(total_past, LANES))[0:1, :] + _dot_exact_rhs(
        tri8_ref[...], _split3(_pad_to(lf_new, LANES)))
    cq = jnp.concatenate([c_new[:, hh:hh + 1] for hh in range(GROUP_HEADS)], axis=0)
    c_new_t = jnp.transpose(_pad_to(c_new, LANES))

    qbd =_head_blockdiag_q((h[:, BQ:BQ + GROUP_W] * SCALE).astype(bf16), tn)
    kv_new = h[:, BK:BK + 2 * GROUP_W]
    fkv_ref[0] = kv_new
    kvb_new = _pad_to(kv_new, LANES).astype(bf16)

    def head_rows(c4):
        return jnp.concatenate([jnp.broadcast_to(c4[hh:hh + 1, :], (tn, c4.shape[1])) for hh in range(GROUP_HEADS)],
                               axis=0)

    tiles = []
    for pg in range(n_pages):
        k_t = fox_pages[pg][0, 0, 0:GROUP_W, :].astype(bf16)
        tiles.append(_dot(qbd, k_t) + cq - head_rows(c_past[GROUP_HEADS * pg:GROUP_HEADS * (pg + 1)]))
    tiles.append(jnp.where(new_ok, _dot_nt(qbd, kvb_new[:, 0:GROUP_W]) + cq - head_rows(c_new_t[0:GROUP_HEADS]), NEG))
    ps, tot_p = _softmax_tiles(tiles)
    acc = _dot(ps[n_pages].astype(bf16), kvb_new[:, GROUP_W:2 * GROUP_W])
    for pg in range(n_pages):
        acc = acc + _dot_nt(ps[pg].astype(bf16), fox_pages[pg][0, 0, GROUP_W:2 * GROUP_W, :].astype(bf16))
    y_b = _head_diag(acc / tot_p, tn)

    u = _gelu(h[:, AU:AU + GROUP_W])
    vn = _head_layer_norm(_gelu(h[:, AV:AV + GROUP_W]), avg_ref[...])
    gv_ref[0] = vn
    w8 = jnp.concatenate([wst_ref[hh * CHUNK:hh * CHUNK + tn, :] for hh in range(GROUP_HEADS)], axis=0)
    z_all = _dot(w8, _pad_to(vn, CHUNK).astype(bf16))
    y_a = u * (_head_diag(z_all, tn) + bs_ref[0:tn, :])

    z = h[:, DC:DC + GROUP_W] * h[:, DH:DH + GROUP_W]
    zp = conv_ref[0, 0]
    row = lax.broadcasted_iota(jnp.int32, (tn, GROUP_W), 0)
    z1 = jnp.where(row == 0, zp[1:2, :], pltpu.roll(z, 1, 0))
    z2 = jnp.where(row == 0, zp[0:1, :], jnp.where(row == 1, zp[1:2, :], pltpu.roll(z, 2, 0)))
    cw = cw_ref[...]
    y_d = h[:, DB:DB + GROUP_W] * (cw[0:1, :] * z2 + cw[1:2, :] * z1 + cw[2:3, :] * z)
    cout_ref[0] = z

    sel_new =_pad_to(ckv[:, 2 * HEAD_DIM:4 * HEAD_DIM], LANES).astype(bf16)
    tiles = []
    for pg in range(n_pages + 1):
        picked = _dot(sel, exp_ref[:, pg * page:(pg + 1) * page])
        if pg == n_pages:
            picked = jnp.where(new_pos <= qpos, picked, 0.0)
            sc = _dot_nt(qs, sel_new[:, 0:HEAD_DIM])
        else:
            sc = _dot(qs, nsa_pages[pg][0, 0, 2 * HEAD_DIM:3 * HEAD_DIM, :].astype(bf16))
        keep = jnp.concatenate([picked] * GROUP_HEADS, axis=0) > 0.5
        tiles.append(jnp.where(keep, sc, NEG))
    ps, tot_p = _softmax_tiles(tiles)
    acc = _dot(ps[n_pages].astype(bf16), sel_new[:, HEAD_DIM:2 * HEAD_DIM])
    for pg in range(n_pages):
        acc = acc + _dot_nt(ps[pg].astype(bf16), nsa_pages[pg][0, 0, 3 * HEAD_DIM:4 * HEAD_DIM, :].astype(bf16))
    o_s = acc / tot_p

    win_old = win_ref[0, 0]
    win_new = ckv[:, 4 * HEAD_DIM:6 * HEAD_DIM]
    new_t = pltpu.roll(jnp.transpose(_pad_to(win_new, LANES)), LANES - tn, 1)
    new_wide = jnp.concatenate([jnp.zeros((2 * HEAD_DIM, WINDOW - LANES), f32), new_t], axis=1)
    w_idx = lax.broadcasted_iota(jnp.int32, (1, WINDOW), 1)
    wout_ref[0] = jnp.where(w_idx >= WINDOW - tn, new_wide, pltpu.roll(win_old, WINDOW - tn, 1))
    wob = win_old.astype(bf16)
    wnb = _pad_to(win_new, LANES).astype(bf16)
    s_old = jnp.where(w_idx > qpos_s - past, _dot(qs, wob[0:HEAD_DIM, :]), NEG)
    s_new = jnp.where(new_ok, _dot_nt(qs, wnb[:, 0:HEAD_DIM]), NEG)
    ps, tot_p = _softmax_tiles([s_old, s_new])
    o_w = (_dot_nt(ps[0].astype(bf16), wob[HEAD_DIM:2 * HEAD_DIM, :])
           + _dot(ps[1].astype(bf16), wnb[:, HEAD_DIM:2 * HEAD_DIM])) / tot_p

    o = _gate_col(misc, 0) * o_c + _gate_col(misc, 1) * o_s + _gate_col(misc, 2) * o_w
    y_c = _unstack_heads(o, tn)
    y_ref[0] = jnp.concatenate([y_a, y_b, y_c, y_d], axis=-1)


def _sample_mixer(h3, fox_t, lf_t, nsa_t, win_t, conv4, page_table, layer, consts):
    bsz, tn, pw = h3.shape
    n_pages = page_table.shape[1]
    page = fox_t.shape[3]
    past = n_pages * page
    n_sel_pad = -(-(-(-(past + tn) // SEL_LEN)) // 8) * 8

    def page_spec(arr, pg):
        blk = (1, 1) + arr.shape[2:]
        return pl.BlockSpec(blk, lambda b, pt: (layer, pt[b, pg], 0, 0))

    in_specs = [pl.BlockSpec((1, tn, pw), lambda b, pt: (b, 0, 0))]
    operands = [h3]
    for arr in (fox_t, lf_t, nsa_t):
        for pg in range(n_pages):
            in_specs.append(page_spec(arr, pg))
            operands.append(arr)
    in_specs.append(pl.BlockSpec((1, 1) + win_t.shape[2:], lambda b, pt: (layer, b, 0, 0)))
    operands.append(win_t)
    in_specs.append(pl.BlockSpec((1, 1) + conv4.shape[2:], lambda b, pt: (layer, b, 0, 0)))
    operands.append(conv4)
    for a in consts:
        in_specs.append(pl.BlockSpec(a.shape, lambda b, pt, nd=a.ndim: (0,) * nd))
        operands.append(a)
    out_dims = [(tn, 4 * GROUP_W), (tn, 2 * GROUP_W), (tn, LANES), (tn, 4 * HEAD_DIM), (2 * HEAD_DIM, WINDOW),
                (tn, GROUP_W), (tn, GROUP_W)]
    out_shape = [jax.ShapeDtypeStruct((bsz, r, w), f32) for r, w in out_dims]
    out_specs = [pl.BlockSpec((1, r, w), lambda b, pt: (b, 0, 0)) for r, w in out_dims]
    return pl.pallas_call(
        functools.partial(_sample_mixer_kernel, n_pages=n_pages, page=page, past=past, n_sel_pad=n_sel_pad),
        grid_spec=pltpu.PrefetchScalarGridSpec(num_scalar_prefetch=1, grid=(bsz,), in_specs=in_specs,
                                               out_specs=out_specs,
                                               scratch_shapes=[pltpu.VMEM((past, 2 * HEAD_DIM), f32)]),
        out_shape=out_shape,
        compiler_params=_cparams(("arbitrary",)),
    )(page_table, *operands)


def _outproj_kernel(x_ref, *refs, alpha):
    y_refs = refs[:-4]
    w_ref, g_ref, b_ref, o_ref = refs[-4:]
    acc = None
    c0 = 0
    for y_ref in y_refs:
        wy = y_ref.shape[1]
        part = _dot(y_ref[...].astype(bf16), w_ref[c0:c0 + wy, :])
        acc = part if acc is None else acc + part
        c0 += wy
    o_ref[...] = _layer_norm_rows(alpha * x_ref[...] + acc, g_ref[0:1, :], b_ref[0:1, :])


def _outproj(x2, ys, w_out_b, g8, b8, alpha, tm):
    n, d = x2.shape
    assert sum(y.shape[1] for y in ys) == d
    const = lambda a: pl.BlockSpec(a.shape, lambda i: (0,) * a.ndim)
    return pl.pallas_call(
        functools.partial(_outproj_kernel, alpha=alpha),
        grid=(n // tm,),
        in_specs=[pl.BlockSpec((tm, d), lambda i: (i, 0))] + [pl.BlockSpec((tm, y.shape[1]), lambda i: (i, 0)) for y in ys]
        + [const(w_out_b), const(g8), const(b8)],
        out_specs=pl.BlockSpec((tm, d), lambda i: (i, 0)),
        out_shape=jax.ShapeDtypeStruct((n, d), f32),
        compiler_params=_cparams(("arbitrary",)),
    )(x2, *ys, w_out_b, g8, b8)


def _route(xb, wr_t, tm):
    logits = _dot_nt(wr_t, xb)
    e = jnp.exp(logits - jnp.max(logits, axis=0, keepdims=True))
    probs = e / jnp.sum(e, axis=0, keepdims=True)
    p = [probs[i:i + 1, :] for i in range(N_EXPERTS)]
    one = jnp.ones((1, tm), f32)
    zero = jnp.zeros((1, tm), f32)
    in_top = []
    gscore = []
    for g in range(N_GROUPS):
        members = range(g * EXP_PER_GROUP, (g + 1) * EXP_PER_GROUP)
        score = zero
        for i in members:
            rank = zero
            for j in members:
                if j < i:
                    rank = rank + jnp.where(p[j] >= p[i], one, zero)
                elif j > i:
                    rank = rank + jnp.where(p[j] > p[i], one, zero)
            keep = jnp.where(rank < TOP_K, one, zero)
            in_top.append(keep)
            score = score + keep * p[i]
        gscore.append(score)
    gate_rows = []
    wsum = zero
    chosen = []
    for g in range(N_GROUPS):
        win = one
        for g2 in range(N_GROUPS):
            if g2 < g:
                win = win * jnp.where(gscore[g] > gscore[g2], one, zero)
            elif g2 > g:
                win = win * jnp.where(gscore[g] >= gscore[g2], one, zero)
        for i in range(g * EXP_PER_GROUP, (g + 1) * EXP_PER_GROUP):
            c = in_top[i] * win
            chosen.append(c)
            wsum = wsum + c * p[i]
    for i in range(N_EXPERTS):
        gate_rows.append(jnp.where(chosen[i] > 0.5, p[i] / wsum, zero))
    gate_t = jnp.concatenate(gate_rows + [jnp.zeros((LANES - N_EXPERTS, tm), f32)], axis=0)
    return jnp.concatenate([jnp.transpose(gate_t[:, c * LANES:(c + 1) * LANES]) for c in range(tm // LANES)], axis=0)


def _moe_kernel(x_ref, wr_ref, wg_ref, wu_ref, wd_ref, g_ref, b_ref, o_ref, gate_sc, acc_sc, xb_sc, *, alpha, tm):
    grp = pl.program_id(1)
    n_e = wg_ref.shape[0]

    @pl.when(grp == 0)
    def _():
        xb = x_ref[...].astype(bf16)
        xb_sc[...] = xb
        gate_sc[...] = _route(xb, wr_ref[...], tm)

    xb = xb_sc[...]
    lane = lax.broadcasted_iota(jnp.int32, (tm, LANES), 1)
    gate = gate_sc[...]
    gates = [_dot(xb, wg_ref[j]) for j in range(n_e)]
    ups = [_dot(xb, wu_ref[j]) for j in range(n_e)]
    hs = []
    for j in range(n_e):
        gcol = jnp.sum(jnp.where(lane == grp * n_e + j, gate, 0.0), axis=-1, keepdims=True)
        hs.append((jax.nn.silu(gates[j]) * ups[j] * gcol).astype(bf16))
    y = _dot(jnp.concatenate(hs, axis=-1), wd_ref[...].reshape(n_e * wd_ref.shape[1], wd_ref.shape[2]))

    @pl.when(grp == 0)
    def _():
        acc_sc[...] = y

    @pl.when(grp > 0)
    def _():
        acc_sc[...] += y

    @pl.when(grp == pl.num_programs(1) - 1)
    def _():
        o_ref[...] = _layer_norm_rows(alpha * x_ref[...] + acc_sc[...], g_ref[0:1, :], b_ref[0:1, :])


def _moe(x2, wr_t, wg, wu, wd, g8, b8, alpha, tm):
    n, d = x2.shape
    de = wg.shape[-1]
    es = EXP_PER_GROUP
    const = lambda a: pl.BlockSpec(a.shape, lambda i, e: (0,) * a.ndim)
    return pl.pallas_call(
        functools.partial(_moe_kernel, alpha=alpha, tm=tm),
        grid=(n // tm, N_EXPERTS // es),
        in_specs=[
            pl.BlockSpec((tm, d), lambda i, e: (i, 0)), const(wr_t),
            pl.BlockSpec((es, d, de), lambda i, e: (e, 0, 0)), pl.BlockSpec((es, d, de), lambda i, e: (e, 0, 0)),
            pl.BlockSpec((es, de, d), lambda i, e: (e, 0, 0)), const(g8), const(b8),
        ],
        out_specs=pl.BlockSpec((tm, d), lambda i, e: (i, 0)),
        out_shape=jax.ShapeDtypeStruct((n, d), f32),
        scratch_shapes=[pltpu.VMEM((tm, LANES), f32), pltpu.VMEM((tm, d), f32), pltpu.VMEM((tm, d), bf16)],
        compiler_params=_cparams(("arbitrary", "arbitrary")),
    )(x2, wr_t, wg, wu, wd, g8, b8)


def _pad_rows(a, rows=8):
    return jnp.concatenate([a, jnp.zeros((rows - a.shape[0],) + a.shape[1:], a.dtype)], axis=0)


def _prep_layer(l, w_in, fox_b_f, gmlp_w_s, gmlp_b_s, nsa_cmp_pe, nsa_cmp_w1, nsa_cmp_w2, conv_w, w_out, ln_g, ln_b,
                w_exp_gate, w_exp_up, w_exp_down):
    w = w_in[l]
    d = w.shape[0]
    w_pack = jnp.concatenate([w[:, 0:1280], w[:, 1284:1924], w[:, 1936:2704], w[:, 1280:1284], w[:, 1924:1936],
                              jnp.zeros((d, LANES - N_FOX_F - N_GATES), w.dtype)], axis=1).astype(bf16)
    fb_row = _pad_rows(jnp.concatenate([fox_b_f[l], jnp.zeros((LANES - N_FOX_F,), f32)])[None, :])
    tril = jnp.asarray(np.tril(np.ones((CHUNK, CHUNK), np.float32)))
    wst = (gmlp_w_s[l] * tril).reshape(GROUP_HEADS * CHUNK, CHUNK).astype(bf16)
    bs_full = jnp.repeat(gmlp_b_s[l].T, HEAD_DIM, axis=1)
    cw8 = _pad_rows(conv_w[l])
    half = CMP_LEN // 2
    w1r = nsa_cmp_w1[l].reshape(2, CMP_LEN, HEAD_DIM, CMP_HID)
    per = nsa_cmp_pe[l]
    w2c = jnp.concatenate([nsa_cmp_w2[l, 0], nsa_cmp_w2[l, 1]], axis=1).astype(bf16)
    zed = jnp.zeros((CMP_LEN, HEAD_DIM, CMP_HID), f32)
    w1r = jnp.concatenate([jnp.concatenate([w1r[0], zed], axis=2), jnp.concatenate([zed, w1r[1]], axis=2)],
                          axis=1).astype(bf16)
    pe32 = jnp.concatenate([per[0], per[1]], axis=1)
    w1g = w1r.reshape(2, half * 2 * HEAD_DIM, 2 * CMP_HID)
    pe2 = _pad_rows(pe32.reshape(2, half * 2 * HEAD_DIM))
    w_t = jnp.concatenate([w_pack[:, BK:BK + 2 * GROUP_W], w_pack[:, CKV:CKV + 6 * HEAD_DIM]], axis=1).T
    return dict(
        w_pack=w_pack, w_t=w_t, fb_row=fb_row, wst=wst, bs_full=bs_full, cw8=cw8, w1g=w1g, pe2=pe2, w2c=w2c, w1r=w1r,
        pe32=pe32,
        w_out=w_out[l].astype(bf16), g1=_pad_rows(ln_g[l, 0:1]), b1=_pad_rows(ln_b[l, 0:1]),
        g2=_pad_rows(ln_g[l, 1:2]), b2=_pad_rows(ln_b[l, 1:2]),
        wg=w_exp_gate[l].astype(bf16), wu=w_exp_up[l].astype(bf16), wd=w_exp_down[l].astype(bf16),
    )


def _sel_overlap(t, n_sel):
    n_cmp = (t - CMP_LEN) // CMP_STRIDE + 1
    start = np.arange(n_cmp) * CMP_STRIDE
    sstart = np.arange(n_sel) * SEL_LEN
    ov = np.minimum(start[:, None] + CMP_LEN, sstart[None, :] + SEL_LEN) - np.maximum(start[:, None], sstart[None, :])
    return np.clip(ov, 0, None) / CMP_STRIDE


def _head_avg():
    a = np.kron(np.eye(GROUP_HEADS, dtype=np.float32), np.full((HEAD_DIM, HEAD_DIM), 1.0 / HEAD_DIM, np.float32))
    return jnp.asarray(a, bf16)


def _post_block(x2, ys, p, wr_t, alpha, tm):
    x1 = _outproj(x2, ys, p['w_out'], p['g1'], p['b1'], alpha, tm)
    return _moe(x1, wr_t, p['wg'], p['wu'], p['wd'], p['g2'], p['b2'], alpha, tm)


def _prompt_layer(x, p, wr_t, alpha, tm=512, tk=512, bq=128):
    bsz, t, d = x.shape
    n = bsz * t
    n_sel = t // SEL_LEN
    n_grp = t // CMP_STRIDE
    assert t % tm == 0 and t % tk == 0 and n_sel <= LANES and t >= WINDOW + bq
    tri = jnp.asarray(np.triu(np.ones((tm, tm), np.float32)), bf16)
    expand_t = np.zeros((t, LANES), np.float32)
    expand_t[np.arange(t), np.arange(t) // SEL_LEN] = 1.0
    (ya, fq, fk, fvt, fkvt, misc, ccol, crow, nq, nkvt, wint, kse, featt, kcvc, yd, cst) = _prompt_inproj(
        x, jnp.asarray(expand_t, bf16), p['w_pack'], p['w_t'], p['fb_row'], p['wst'], p['bs_full'], p['cw8'],
        _head_avg(), tri, tm)
    yb = _fox_prompt(fq, fk, fvt, ccol, crow, tk, bq)
    cmp, cmp_t = _compress(kcvc.reshape(bsz, n_grp, CMP_STRIDE * 2 * HEAD_DIM), p['pe2'], p['w1g'], p['w2c'])
    ov = _sel_overlap(t, n_sel)
    selw_t = np.zeros((LANES, n_grp), np.float32)
    selw_t[:n_sel, :ov.shape[0]] = ov.T
    yc = _nsa_prompt(nq, kse, featt, cmp, cmp_t, misc, jnp.asarray(selw_t, bf16), tk, bq)
    ys = [a.reshape(n, GROUP_W) for a in (ya, yb, yc, yd)]
    x_out = _post_block(x.reshape(n, d), ys, p, wr_t, alpha, tm).reshape(bsz, t, d)
    win_keep = min(WINDOW, t)
    states = (fkvt.reshape(bsz, 2, GROUP_HEADS, HEAD_DIM, t).transpose(0, 4, 1, 2, 3), misc[:, :, 0:N_FOX_F],
              nkvt.reshape(bsz, 4, HEAD_DIM, t).transpose(0, 3, 1, 2),
              wint[:, :, t - win_keep:].reshape(bsz, 2, HEAD_DIM, win_keep).transpose(0, 3, 1, 2),
              cst[:, 8 - (CONV_W - 1):])
    return x_out, states, (ya, yb, yc, yd)


def _sample_consts(p, n_pages, page, tn):
    utri = np.triu(np.ones((page, page), np.float32))
    ones = np.ones((page, page), np.float32)
    r = np.arange(GROUP_HEADS * n_pages)
    slt = np.zeros((GROUP_HEADS * n_pages, LANES), np.float32)
    slt[:, :r.size] = ((r[:, None] % GROUP_HEADS) == (r[None, :] % GROUP_HEADS)) & (r[None, :] // GROUP_HEADS < r[:, None] // GROUP_HEADS)
    tri8 = np.zeros((tn, LANES), np.float32)
    tri8[:, :tn] = np.tril(np.ones((tn, tn), np.float32))
    t_all = n_pages * page + tn
    n_sel = -(-t_all // SEL_LEN)
    ov = _sel_overlap(t_all, n_sel)
    n_grp = n_pages * page // CMP_STRIDE
    assert ov.shape[0] <= n_grp and n_sel <= LANES
    selw = np.zeros((n_grp, LANES), np.float32)
    selw[:ov.shape[0], :n_sel] = ov
    n_keys = (n_pages + 1) * page
    expand = np.zeros((LANES, n_keys), np.float32)
    expand[np.arange(n_keys) // SEL_LEN, np.arange(n_keys)] = 1.0
    as_b = lambda a: jnp.asarray(np.asarray(a, np.float32), bf16)
    return [p['fb_row'], p['wst'], p['bs_full'], p['cw8'], _head_avg(), as_b(utri), as_b(ones), as_b(slt),
            as_b(tri8), p['pe32'], p['w1g'], p['w2c'], as_b(selw), as_b(expand)]


def _sample_layer(x, views, page_table, layer, p, wr_t, alpha, tm=512):
    bsz, tn, d = x.shape
    n = bsz * tn
    fox_t, lf_t, nsa_t, win_t, conv4 = views
    n_pages = page_table.shape[1]
    page = fox_t.shape[3]
    assert tn == 8 and page == LANES and win_t.shape[3] == WINDOW and n_pages * page >= WINDOW
    assert GROUP_HEADS * n_pages <= LANES
    h = _sample_inproj(x.reshape(n, d), p['w_pack'], tm).reshape(bsz, tn, P_PACK)
    y, fkv, misc, nrow, wout, cout, gv = _sample_mixer(h, fox_t, lf_t, nsa_t, win_t, conv4, page_table, layer,
                                                       _sample_consts(p, n_pages, page, tn))
    x_out = _post_block(x.reshape(n, d), [y.reshape(n, d)], p, wr_t, alpha, tm).reshape(bsz, tn, d)
    win_state = wout.reshape(bsz, 2, HEAD_DIM, WINDOW).transpose(0, 3, 1, 2)
    states = (fkv.reshape(bsz, tn, 2, GROUP_HEADS, HEAD_DIM), misc[:, :, 0:N_FOX_F], nrow.reshape(bsz, tn, 4, HEAD_DIM),
              win_state, cout[:, tn - (CONV_W - 1):], gv.reshape(bsz, tn, GROUP_HEADS, HEAD_DIM))
    return x_out, states


def kernel(x_prompt, x_sample, cache_fox_kv, cache_fox_logf, cache_nsa_kv, state_nsa_win, state_conv, page_table,
           w_in, fox_b_f, gmlp_w_s, gmlp_b_s, nsa_cmp_pe, nsa_cmp_w1, nsa_cmp_w2, conv_w, w_out, ln_g, ln_b,
           w_router, w_exp_gate, w_exp_up, w_exp_down):
    depth, n_pool, page = cache_fox_kv.shape[0:3]
    dec_b = x_sample.shape[0]
    alpha = (2 * depth) ** 0.25
    wr_t = w_router.T.astype(bf16)
    views = (cache_fox_kv.transpose(0, 1, 3, 4, 5, 2).reshape(depth, n_pool, 2 * GROUP_W, page),
             cache_fox_logf.transpose(0, 1, 3, 2),
             cache_nsa_kv.transpose(0, 1, 3, 4, 2).reshape(depth, n_pool, 4 * HEAD_DIM, page),
             state_nsa_win.transpose(0, 1, 3, 4, 2).reshape(depth, dec_b, 2 * HEAD_DIM, state_nsa_win.shape[2]),
             state_conv)
    xp, xs = x_prompt, x_sample
    st_p = [[] for _ in range(5)]
    st_s = [[] for _ in range(6)]
    for l in range(depth):
        p = _prep_layer(l, w_in, fox_b_f, gmlp_w_s, gmlp_b_s, nsa_cmp_pe, nsa_cmp_w1, nsa_cmp_w2, conv_w, w_out,
                        ln_g, ln_b, w_exp_gate, w_exp_up, w_exp_down)
        xp, sp, _ = _prompt_layer(xp, p, wr_t, alpha)
        xs, ss = _sample_layer(xs, views, page_table, l, p, wr_t, alpha)
        for lst, a in zip(st_p, sp):
            lst.append(a)
        for lst, a in zip(st_s, ss):
            lst.append(a)
    fox_kv_p, fox_lf_p, nsa_kv_p, nsa_win_p, conv_p = [jnp.stack(a, axis=0) for a in st_p]
    fox_kv_s, fox_lf_s, nsa_kv_s, nsa_win_s, conv_s, gmlp_v_s = [jnp.stack(a, axis=0) for a in st_s]
    return (xp, xs, fox_kv_p, fox_kv_s, fox_lf_p, fox_lf_s, nsa_kv_p, nsa_kv_s, nsa_win_p, nsa_win_s, conv_p, conv_s,
            gmlp_v_s)
```

```python
import functools

import numpy as np
import jax
import jax.numpy as jnp
from jax import lax
from jax.experimental import pallas as pl
from jax.experimental.pallas import tpu as pltpu

f32 = jnp.float32
bf16 = jnp.bfloat16

HEAD_DIM = 64
GROUP_W = 256
GROUP_HEADS = 4
CHUNK = 128
Q_BLOCK = 128
CMP_LEN = 32
CMP_STRIDE = 16
CMP_HID = 128
SEL_LEN = 64
SEL_TOPK = 16
WINDOW = 512
CONV_W = 3
N_EXPERTS = 16
N_GROUPS = 4
EXP_PER_GROUP = 4
TOP_K = 2
LN_EPS = 1e-5
SCALE = HEAD_DIM ** -0.5
NEG = -1e30
BIG = 1e30
LANES = 128

AU, AV, BQ, BK, BV, CQ, CKV, DH, DB, DC, MISC = 0, 256, 512, 768, 1024, 1280, 1536, 1920, 2176, 2432, 2688
P_PACK = 2816
N_FOX_F = GROUP_HEADS
N_GATES = 3 * GROUP_HEADS

VMEM_LIMIT = 56 * 1024 * 1024


def _cparams(sem):
    return pltpu.CompilerParams(dimension_semantics=sem, vmem_limit_bytes=VMEM_LIMIT)


def _dot(a, b):
    return jnp.dot(a, b, preferred_element_type=f32)


def _dot_nt(a, b):
    return lax.dot_general(a, b, (((1,), (1,)), ((), ())), preferred_element_type=f32)


def _split3(x):
    hi = x.astype(bf16)
    r = x - hi.astype(f32)
    mid = r.astype(bf16)
    lo = (r - mid.astype(f32)).astype(bf16)
    return hi, mid, lo


def _dot_exact_rhs(a, b3):
    return _dot(a, b3[0]) + _dot(a, b3[1]) + _dot(a, b3[2])


def _dot_exact_lhs(a3, b):
    return _dot(a3[0], b) + _dot(a3[1], b) + _dot(a3[2], b)


def _log_sigmoid(x):
    return jnp.minimum(x, 0.0) - jnp.log1p(jnp.exp(-jnp.abs(x)))


def _gelu(x):
    return jax.nn.gelu(x, approximate=True)


def _layer_norm_rows(x, g, b):
    mu = jnp.mean(x, axis=-1, keepdims=True)
    d = x - mu
    var = jnp.mean(d * d, axis=-1, keepdims=True)
    return d * lax.rsqrt(var + LN_EPS) * g + b


def _head_layer_norm(g, avg):
    h2 = g.astype(bf16)
    mu = _dot(h2, avg) + _dot((g - h2.astype(f32)).astype(bf16), avg)
    d = g - mu
    dd = d * d
    d2 = dd.astype(bf16)
    var = _dot(d2, avg) + _dot((dd - d2.astype(f32)).astype(bf16), avg)
    return d * lax.rsqrt(var + LN_EPS)


def _head_diag(z_all, rows):
    lane_head = lax.broadcasted_iota(jnp.int32, (rows, GROUP_W), 1) // HEAD_DIM
    out = jnp.where(lane_head == 0, z_all[0:rows], 0.0)
    for h in range(1, GROUP_HEADS):
        out = out + jnp.where(lane_head == h, z_all[h * rows:(h + 1) * rows], 0.0)
    return out


def _head_blockdiag_q(q, rows):
    lane_head = lax.broadcasted_iota(jnp.int32, (rows, GROUP_W), 1) // HEAD_DIM
    return jnp.concatenate([jnp.where(lane_head == h, q, jnp.zeros_like(q)) for h in range(GROUP_HEADS)], axis=0)


def _prompt_inproj_kernel(x_ref, exp_ref, w_ref, wt_ref, fb_ref, wst_ref, bs_ref, cw_ref, avg_ref, tri_ref,
                          ya_ref, fq_ref, fk_ref, fvt_ref, fkvt_ref, misc_ref, ccol_ref, crow_ref,
                          nq_ref, nkvt_ref, wint_ref, kse_ref, featt_ref, kcvc_ref, yd_ref, cst_ref,
                          carry_c, carry_z, *, tm):
    ti = pl.program_id(1)

    @pl.when(ti == 0)
    def _():
        carry_c[...] = jnp.zeros_like(carry_c)
        carry_z[...] = jnp.zeros_like(carry_z)

    xb = x_ref[0].astype(bf16)

    u = _gelu(_dot(xb, w_ref[:, AU:AU + GROUP_W]))
    vn = _head_layer_norm(_gelu(_dot(xb, w_ref[:, AV:AV + GROUP_W])), avg_ref[...])
    wst = wst_ref[...]
    bsf = bs_ref[...]
    for c in range(tm // CHUNK):
        r0 = c * CHUNK
        z_all = _dot(wst, vn[r0:r0 + CHUNK].astype(bf16))
        ya_ref[0, r0:r0 + CHUNK, :] = (u[r0:r0 + CHUNK] * (_head_diag(z_all, CHUNK) + bsf)).astype(bf16)

    fq_ref[0] = (_dot(xb, w_ref[:, BQ:BQ + GROUP_W]) * SCALE).astype(bf16)
    res_t = _dot_nt(wt_ref[...], xb)
    fk_ref[0] = jnp.transpose(res_t[0:GROUP_W]).astype(bf16)
    fkvt_ref[0] = res_t[0:2 * GROUP_W]
    fvt_ref[0] = res_t[GROUP_W:2 * GROUP_W].astype(bf16)
    nkvt_ref[0] = res_t[2 * GROUP_W:3 * GROUP_W]
    wint_ref[0] = res_t[3 * GROUP_W:3 * GROUP_W + 2 * HEAD_DIM]
    featt_ref[0] = jnp.concatenate([res_t[3 * GROUP_W - HEAD_DIM:3 * GROUP_W],
                                    res_t[3 * GROUP_W + HEAD_DIM:3 * GROUP_W + 2 * HEAD_DIM]], axis=0).astype(bf16)
    pre =_dot(xb, w_ref[:, MISC:MISC + LANES]) + fb_ref[0:1, :]
    lane = lax.broadcasted_iota(jnp.int32, (tm, LANES), 1)
    lf = jnp.where(lane < N_FOX_F, _log_sigmoid(pre), 0.0)
    misc_ref[0] = jnp.where(lane < N_FOX_F, lf, jnp.where(lane < N_FOX_F + N_GATES, jax.nn.sigmoid(pre), 0.0))
    c_t = _dot_exact_lhs(_split3(jnp.transpose(lf)[0:8, :]), tri_ref[...]) + carry_c[:, 0:1]
    carry_c[...] = jnp.broadcast_to(c_t[:, tm - 1:tm], (8, LANES))
    crow_ref[0] = c_t
    ccol_ref[0] = jnp.transpose(jnp.concatenate([c_t, jnp.zeros((LANES - 8, tm), f32)], axis=0))

    nq_ref[0] = (_dot(xb, w_ref[:, CQ:CQ + GROUP_W]) * SCALE).astype(bf16)
    nsa0 = 2 * GROUP_W
    kcvc_ref[0] = jnp.transpose(res_t[nsa0:nsa0 + 2 * HEAD_DIM])
    ks_kw = jnp.transpose(jnp.concatenate([res_t[nsa0 + 2 * HEAD_DIM:nsa0 + 3 * HEAD_DIM],
                                           res_t[nsa0 + 4 * HEAD_DIM:nsa0 + 5 * HEAD_DIM]], axis=0))
    kse_ref[0] = jnp.concatenate([ks_kw.astype(bf16), exp_ref[...]], axis=-1)

    z = _dot(xb, w_ref[:, DC:DC + GROUP_W]) * _dot(xb, w_ref[:, DH:DH + GROUP_W])
    row = lax.broadcasted_iota(jnp.int32, (tm, GROUP_W), 0)
    zp = carry_z[...]
    z1 = jnp.where(row == 0, zp[7:8, :], pltpu.roll(z, 1, 0))
    z2 = jnp.where(row == 0, zp[6:7, :], jnp.where(row == 1, zp[7:8, :], pltpu.roll(z, 2, 0)))
    cw = cw_ref[...]
    y_conv = cw[0:1, :] * z2 + cw[1:2, :] * z1 + cw[2:3, :] * z
    yd_ref[0] = (_dot(xb, w_ref[:, DB:DB + GROUP_W]) * y_conv).astype(bf16)
    carry_z[...] = z[tm - 8:tm, :]
    cst_ref[0] = z[tm - 8:tm, :]


def _prompt_inproj(x, expand_t, w_pack, w_t, fb_row, wst, bs_full, cw8, avg, tri, tm):
    bsz, t, d = x.shape
    nt = t // tm
    tok = lambda w, dt: (jax.ShapeDtypeStruct((bsz, t, w), dt), pl.BlockSpec((1, tm, w), lambda b, i: (b, i, 0)))
    feat = lambda r, dt: (jax.ShapeDtypeStruct((bsz, r, t), dt), pl.BlockSpec((1, r, tm), lambda b, i: (b, 0, i)))
    const = lambda a: pl.BlockSpec(a.shape, lambda b, i: (0,) * a.ndim)
    outs = [
        tok(GROUP_W, bf16),
        tok(GROUP_W, bf16),
        tok(GROUP_W, bf16),
        feat(GROUP_W, bf16),
        feat(2 * GROUP_W, f32),
        tok(LANES, f32),
        tok(LANES, f32),
        feat(8, f32),
        tok(GROUP_W, bf16),
        feat(4 * HEAD_DIM, f32),
        feat(2 * HEAD_DIM, f32),
        tok(GROUP_W, bf16),
        feat(2 * HEAD_DIM, bf16),
        tok(2 * HEAD_DIM, f32),
        tok(GROUP_W, bf16),
        (jax.ShapeDtypeStruct((bsz, 8, GROUP_W), f32), pl.BlockSpec((1, 8, GROUP_W), lambda b, i: (b, 0, 0))),
    ]
    consts = [w_pack, w_t, fb_row, wst, bs_full, cw8, avg, tri]
    return pl.pallas_call(
        functools.partial(_prompt_inproj_kernel, tm=tm),
        grid=(bsz, nt),
        in_specs=[pl.BlockSpec((1, tm, d), lambda b, i: (b, i, 0)), pl.BlockSpec((tm, LANES), lambda b, i: (i, 0))]
        + [const(a) for a in consts],
        out_specs=[o[1] for o in outs],
        out_shape=[o[0] for o in outs],
        scratch_shapes=[pltpu.VMEM((8, LANES), f32), pltpu.VMEM((8, GROUP_W), f32)],
        compiler_params=_cparams(("arbitrary", "arbitrary")),
    )(x, expand_t, *consts)


def _flash_update_t(chains):
    parts = []
    for s, v_t, m_sc, l_sc, acc_sc in chains:
        m_old = m_sc[...]
        m_new = jnp.maximum(m_old, jnp.max(s, axis=0, keepdims=True))
        alpha = jnp.exp(m_old - m_new)
        p = jnp.exp(s - m_new)
        l_sc[...] = alpha * l_sc[...] + jnp.sum(p, axis=0, keepdims=True)
        m_sc[...] = m_new
        parts.append((alpha, p.astype(bf16)))
    for (s, v_t, m_sc, l_sc, acc_sc), (alpha, pb) in zip(chains, parts):
        acc_sc[...] = alpha * acc_sc[...] + _dot(v_t, pb)


def _fox_prompt_kernel(q_ref, k_ref, vt_ref, ccol_ref, crow_ref, sel_ref, o_ref, crep_sc, m_sc, l_sc, acc_sc, *, tk, bq):
    qi = pl.program_id(1)
    n_pair = GROUP_HEADS // 2
    pw = 2 * HEAD_DIM

    @pl.when(qi == 0)
    def _():
        c3 = _split3(ccol_ref[0])
        for h in range(GROUP_HEADS):
            crep_sc[h] = _dot_exact_lhs(c3, sel_ref[h])

    nq = LANES
    n_sub = bq // nq
    lane_hi = lax.broadcasted_iota(jnp.int32, (nq, pw), 1) >= HEAD_DIM
    qbd, cq, qpos = [], [], []
    for u in range(n_sub):
        q = q_ref[0, u * nq:(u + 1) * nq, :]
        cr = crow_ref[0, :, u * nq:(u + 1) * nq]
        qp = qi * bq + u * nq + lax.broadcasted_iota(jnp.int32, (1, nq), 1)
        qpos.append(jnp.concatenate([qp, qp], axis=1))
        for g in range(n_pair):
            qg = q[:, g * pw:(g + 1) * pw]
            qbd.append(jnp.concatenate([jnp.where(lane_hi, jnp.zeros_like(qg), qg),
                                        jnp.where(lane_hi, qg, jnp.zeros_like(qg))], axis=0))
            cq.append(jnp.concatenate([cr[2 * g:2 * g + 1, :], cr[2 * g + 1:2 * g + 2, :]], axis=1))
    m_sc[...] = jnp.full_like(m_sc, NEG)
    l_sc[...] = jnp.zeros_like(l_sc)
    acc_sc[...] = jnp.zeros_like(acc_sc)

    def tile(j, masked):
        k0 = pl.multiple_of(j * tk, tk)
        chains = []
        for g in range(n_pair):
            k = k_ref[0, pl.ds(k0, tk), g * pw:(g + 1) * pw]
            v_t = vt_ref[0, g * pw:(g + 1) * pw, pl.ds(k0, tk)]
            ck = jnp.concatenate([crep_sc[2 * g, pl.ds(k0, tk), :], crep_sc[2 * g + 1, pl.ds(k0, tk), :]], axis=1)
            for u in range(n_sub):
                c = u * n_pair + g
                s = _dot_nt(k, qbd[c]) + cq[c] - ck
                if masked:
                    kpos = k0 + lax.broadcasted_iota(jnp.int32, (tk, 1), 0)
                    s = jnp.where(kpos <= qpos[u], s, NEG)
                chains.append((s, v_t, m_sc.at[c], l_sc.at[c], acc_sc.at[c]))
        _flash_update_t(chains)

    n_full = (qi * bq) // tk

    def body(j, carry):
        tile(j, False)
        return carry

    lax.fori_loop(0, n_full, body, 0)
    tile(n_full, True)
    for u in range(n_sub):
        outs = []
        for g in range(n_pair):
            c = u * n_pair + g
            o = acc_sc[c] / l_sc[c]
            outs.append(o[0:HEAD_DIM, 0:nq])
            outs.append(o[HEAD_DIM:pw, nq:2 * nq])
        o_ref[0, u * nq:(u + 1) * nq, :] = jnp.transpose(jnp.concatenate(outs, axis=0)).astype(bf16)


def _fox_prompt(fq, fk, fvt, ccol, crow, tk, bq):
    bsz, t, _ = fq.shape
    assert tk % bq == 0 and t % tk == 0 and bq % LANES == 0
    n_chain = (bq // LANES) * (GROUP_HEADS // 2)
    sel = np.zeros((GROUP_HEADS, LANES, LANES), np.float32)
    for h in range(GROUP_HEADS):
        sel[h, h, :] = 1.0
    sel = jnp.asarray(sel, bf16)
    return pl.pallas_call(
        functools.partial(_fox_prompt_kernel, tk=tk, bq=bq),
        grid=(bsz, t // bq),
        in_specs=[
            pl.BlockSpec((1, bq, GROUP_W), lambda b, i: (b, i, 0)),
            pl.BlockSpec((1, t, GROUP_W), lambda b, i: (b, 0, 0)),
            pl.BlockSpec((1, GROUP_W, t), lambda b, i: (b, 0, 0)),
            pl.BlockSpec((1, t, LANES), lambda b, i: (b, 0, 0)),
            pl.BlockSpec((1, 8, bq), lambda b, i: (b, 0, i)),
            pl.BlockSpec(sel.shape, lambda b, i: (0, 0, 0)),
        ],
        out_specs=pl.BlockSpec((1, bq, GROUP_W), lambda b, i: (b, i, 0)),
        out_shape=jax.ShapeDtypeStruct((bsz, t, GROUP_W), bf16),
        scratch_shapes=[pltpu.VMEM((GROUP_HEADS, t, LANES), f32), pltpu.VMEM((n_chain, 1, 2 * LANES), f32),
                        pltpu.VMEM((n_chain, 1, 2 * LANES), f32), pltpu.VMEM((n_chain, 2 * HEAD_DIM, 2 * LANES), f32)],
        compiler_params=_cparams(("arbitrary", "arbitrary")),
    )(fq, fk, fvt, ccol, crow, sel)


def _compress_kernel(x_ref, pe_ref, w1_ref, w2_ref, o_ref, ot_ref):
    x = x_ref[0]
    n_grp = x.shape[0]
    top = _dot((x + pe_ref[0:1, :]).astype(bf16), w1_ref[0])
    bot = _dot((x + pe_ref[1:2, :]).astype(bf16), w1_ref[1])
    g = _gelu(top + pltpu.roll(bot, n_grp - 1, 0)).astype(bf16)
    w2 = w2_ref[...]
    kc = _dot(g[:, 0:CMP_HID], w2[:, 0:HEAD_DIM])
    vc = _dot(g[:, CMP_HID:2 * CMP_HID], w2[:, HEAD_DIM:2 * HEAD_DIM])
    cmp = jnp.concatenate([kc, vc], axis=-1)
    o_ref[0] = cmp.astype(bf16)
    ot_ref[0] = jnp.transpose(cmp).astype(bf16)


def _compress(x_groups, pe2, w1g, w2c):
    bsz, n_grp, w = x_groups.shape
    const = lambda a: pl.BlockSpec(a.shape, lambda b: (0,) * a.ndim)
    return pl.pallas_call(
        _compress_kernel,
        grid=(bsz,),
        in_specs=[pl.BlockSpec((1, n_grp, w), lambda b: (b, 0, 0)), const(pe2), const(w1g), const(w2c)],
        out_specs=[pl.BlockSpec((1, n_grp, 2 * HEAD_DIM), lambda b: (b, 0, 0)),
                   pl.BlockSpec((1, 2 * HEAD_DIM, n_grp), lambda b: (b, 0, 0))],
        out_shape=[jax.ShapeDtypeStruct((bsz, n_grp, 2 * HEAD_DIM), bf16),
                   jax.ShapeDtypeStruct((bsz, 2 * HEAD_DIM, n_grp), bf16)],
        compiler_params=_cparams(("arbitrary",)),
    )(x_groups, pe2, w1g, w2c)


def _stack_heads(q, rows):
    return jnp.concatenate([q[:, h * HEAD_DIM:(h + 1) * HEAD_DIM] for h in range(GROUP_HEADS)], axis=0)


def _unstack_heads(o, rows):
    return jnp.concatenate([o[h * rows:(h + 1) * rows] for h in range(GROUP_HEADS)], axis=-1)


def _gate_col(misc, branch):
    base = N_FOX_F + branch * GROUP_HEADS
    return jnp.concatenate([misc[:, base + h:base + h + 1] for h in range(GROUP_HEADS)], axis=0)


def _nsa_compressed(qs, cmp, qpos_s, selw, nq):
    n_c = cmp.shape[0]
    kc = cmp[:, 0:HEAD_DIM]
    vc = cmp[:, HEAD_DIM:2 * HEAD_DIM]
    s = _dot_nt(qs, kc)
    cend = lax.broadcasted_iota(jnp.int32, (1, n_c), 1) * CMP_STRIDE + (CMP_LEN - 1)
    vis = cend <= qpos_s
    s = jnp.where(vis, s, NEG)
    e = jnp.exp(s - jnp.max(s, axis=-1, keepdims=True))
    p = e / jnp.sum(e, axis=-1, keepdims=True)
    p = p * jnp.where(qpos_s >= CMP_LEN - 1, 1.0, 0.0)
    pb = p.astype(bf16)
    o_c = _dot(pb, vc)
    imp = _dot(pb[0:nq], selw)
    for h in range(1, GROUP_HEADS):
        imp = imp + _dot(pb[h * nq:(h + 1) * nq], selw)
    return o_c, imp


def _nsa_select(imp, qpos_row, n_sel_pad):
    nq = imp.shape[0]
    if nq < LANES:
        imp = jnp.concatenate([imp, jnp.zeros((LANES - nq, LANES), f32)], axis=0)
    sel_t = _nsa_select_t(jnp.transpose(imp)[0:n_sel_pad], qpos_row)
    if n_sel_pad < LANES:
        sel_t = jnp.concatenate([sel_t, jnp.zeros((LANES - n_sel_pad, LANES), f32)], axis=0)
    return jnp.transpose(sel_t)[0:nq]


def _nsa_select_t(imp_t, qpos_row):
    n_sel_pad = imp_t.shape[0]
    blk = lax.broadcasted_iota(jnp.int32, (n_sel_pad, LANES), 0)
    cur = qpos_row // SEL_LEN
    valid = blk * SEL_LEN <= qpos_row
    forced = (blk == 0) | (blk == cur) | (blk == cur - 1)
    val = jnp.where(valid, jnp.where(forced, BIG, imp_t), -BIG)
    n_g = n_sel_pad // 8
    grp = [val[8 * g:8 * g + 8] for g in range(n_g)]
    rank = [jnp.zeros((8, LANES), f32) for _ in range(n_g)]
    row8 = lax.broadcasted_iota(jnp.int32, (8, LANES), 0)
    for mp in range(n_sel_pad):
        vb = jnp.broadcast_to(val[mp:mp + 1, :], (8, LANES))
        for g in range(n_g):
            if 8 * g + 7 < mp:
                hit = jnp.where(vb > grp[g], 1.0, 0.0)
            elif 8 * g > mp:
                hit = jnp.where(vb >= grp[g], 1.0, 0.0)
            else:
                hit = jnp.where(row8 > mp - 8 * g, jnp.where(vb >= grp[g], 1.0, 0.0), jnp.where(vb > grp[g], 1.0, 0.0))
            rank[g] = rank[g] + hit
    rank = jnp.concatenate(rank, axis=0)
    return jnp.where(valid, jnp.where(rank < SEL_TOPK, 1.0, 0.0), 0.0)


def _nsa_prompt_kernel(q_ref, kse_ref, ft_ref, cmp_ref, cmpt_ref, misc_ref, selwt_ref, o_ref, m_sc, l_sc, acc_sc, *,
                       tk, t, n_sel_pad, bq):
    qi = pl.program_id(1)
    nq = LANES
    n_sub = bq // nq
    r_all = GROUP_HEADS * nq
    zq = jnp.zeros((HEAD_DIM, r_all), bf16)
    cmp = cmp_ref[0]
    n_c = cmp.shape[0]
    span = min(WINDOW + nq, t)
    selwt = selwt_ref[...]
    sub = [dict() for _ in range(n_sub)]

    def prepare(u):
        d = sub[u]
        q0 = qi * bq + u * nq
        q_t4 = jnp.transpose(q_ref[0, u * nq:(u + 1) * nq, :].astype(f32))
        q_t = jnp.concatenate([q_t4[h * HEAD_DIM:(h + 1) * HEAD_DIM] for h in range(GROUP_HEADS)], axis=1).astype(bf16)
        qpos1 = q0 + lax.broadcasted_iota(jnp.int32, (1, nq), 1)
        qpos = jnp.concatenate([qpos1] * GROUP_HEADS, axis=1)
        d['qpos'] = qpos
        w0 = pl.multiple_of(jnp.maximum(q0 - WINDOW, 0), LANES)
        q_win = jnp.concatenate([zq, q_t, jnp.zeros((LANES, r_all), bf16)], axis=0)
        s_w = _dot(kse_ref[0, pl.ds(w0, span), :], q_win)
        s_c = _dot(cmp, jnp.concatenate([q_t, zq], axis=0))
        yield
        wpos = w0 + lax.broadcasted_iota(jnp.int32, (span, 1), 0)
        s_w = jnp.where((wpos <= qpos) & (wpos > qpos - WINDOW), s_w, NEG)
        cend = lax.broadcasted_iota(jnp.int32, (n_c, 1), 0) * CMP_STRIDE + (CMP_LEN - 1)
        s_c = jnp.where(cend <= qpos, s_c, NEG)
        e_c = jnp.exp(s_c - jnp.max(s_c, axis=0, keepdims=True))
        p_c = e_c / jnp.sum(e_c, axis=0, keepdims=True)
        pb = (p_c * jnp.where(qpos >= CMP_LEN - 1, 1.0, 0.0)).astype(bf16)
        e_w = jnp.exp(s_w - jnp.max(s_w, axis=0, keepdims=True))
        p_w = (e_w / jnp.sum(e_w, axis=0, keepdims=True)).astype(bf16)
        yield
        d['o_c'] = _dot(cmpt_ref[0, HEAD_DIM:2 * HEAD_DIM, :], pb)
        d['o_w'] = _dot(ft_ref[0, HEAD_DIM:2 * HEAD_DIM, pl.ds(w0, span)], p_w)
        imp_t = _dot(selwt, pb[:, 0:nq])
        for h in range(1, GROUP_HEADS):
            imp_t = imp_t + _dot(selwt, pb[:, h * nq:(h + 1) * nq])
        yield
        sel_t = _nsa_select_t(imp_t[0:n_sel_pad], qpos1)
        unsel = jnp.where(sel_t > 0.5, 0.0, -(2.0 ** 100))
        if n_sel_pad < LANES:
            unsel = jnp.concatenate([unsel, jnp.zeros((LANES - n_sel_pad, nq), f32)], axis=0)
        unsel = jnp.concatenate([unsel.astype(bf16)] * GROUP_HEADS, axis=1)
        d['q_sel'] = jnp.concatenate([q_t, zq, unsel], axis=0)

    runs = [prepare(u) for u in range(n_sub)]
    done = object()
    while runs:
        runs = [run for run in runs if next(run, done) is not done]

    m_sc[...] = jnp.full_like(m_sc, NEG)
    l_sc[...] = jnp.zeros_like(l_sc)
    acc_sc[...] = jnp.zeros_like(acc_sc)

    n_half = 2
    rc = r_all // n_half
    chain_ids = [(u, c) for u in range(n_sub) for c in range(n_half)]

    def tile(j, diagonal):
        k0 = pl.multiple_of(j * tk, tk)
        kse = kse_ref[0, pl.ds(k0, tk), :]
        v_t = ft_ref[0, 0:HEAD_DIM, pl.ds(k0, tk)]
        chains = []
        for i, (u, c) in enumerate(chain_ids):
            sc = _dot(kse, sub[u]['q_sel'][:, c * rc:(c + 1) * rc])
            if diagonal:
                kpos = k0 + lax.broadcasted_iota(jnp.int32, (tk, 1), 0)
                sc = jnp.where(kpos <= sub[u]['qpos'][:, c * rc:(c + 1) * rc], sc, NEG)
            chains.append((sc, v_t, m_sc.at[i], l_sc.at[i], acc_sc.at[i]))
        _flash_update_t(chains)

    n_full = (qi * bq) // tk

    def body(j, carry):
        tile(j, False)
        return carry

    lax.fori_loop(0, n_full, body, 0)
    tile(n_full, True)

    for u in range(n_sub):
        o_s = jnp.concatenate([acc_sc[u * n_half + c] / l_sc[u * n_half + c] for c in range(n_half)], axis=1)
        misc_t = jnp.transpose(misc_ref[0, u * nq:(u + 1) * nq, :])

        def gate(branch):
            base = N_FOX_F + branch * GROUP_HEADS
            return jnp.concatenate([misc_t[base + h:base + h + 1, :] for h in range(GROUP_HEADS)], axis=1)

        o = gate(0) * sub[u]['o_c'] + gate(1) * o_s + gate(2) * sub[u]['o_w']
        o = jnp.concatenate([o[:, h * nq:(h + 1) * nq] for h in range(GROUP_HEADS)], axis=0)
        o_ref[0, u * nq:(u + 1) * nq, :] = jnp.transpose(o).astype(bf16)


def _nsa_prompt(nq_b, kse, feat_t, cmp, cmp_t, misc, selw_t, tk, bq):
    bsz, t, _ = nq_b.shape
    n_chain = 2 * (bq // LANES)
    rc = GROUP_HEADS * LANES // 2
    n_c = cmp.shape[1]
    n_sel_pad = -(-(t // SEL_LEN) // 8) * 8
    assert tk % bq == 0 and bq % LANES == 0
    return pl.pallas_call(
        functools.partial(_nsa_prompt_kernel, tk=tk, t=t, n_sel_pad=n_sel_pad, bq=bq),
        grid=(bsz, t // bq),
        in_specs=[
            pl.BlockSpec((1, bq, GROUP_W), lambda b, i: (b, i, 0)),
            pl.BlockSpec((1, t, GROUP_W), lambda b, i: (b, 0, 0)),
            pl.BlockSpec((1, 2 * HEAD_DIM, t), lambda b, i: (b, 0, 0)),
            pl.BlockSpec((1, n_c, 2 * HEAD_DIM), lambda b, i: (b, 0, 0)),
            pl.BlockSpec((1, 2 * HEAD_DIM, n_c), lambda b, i: (b, 0, 0)),
            pl.BlockSpec((1, bq, LANES), lambda b, i: (b, i, 0)),
            pl.BlockSpec(selw_t.shape, lambda b, i: (0, 0)),
        ],
        out_specs=pl.BlockSpec((1, bq, GROUP_W), lambda b, i: (b, i, 0)),
        out_shape=jax.ShapeDtypeStruct((bsz, t, GROUP_W), bf16),
        scratch_shapes=[pltpu.VMEM((n_chain, 1, rc), f32), pltpu.VMEM((n_chain, 1, rc), f32),
                        pltpu.VMEM((n_chain, HEAD_DIM, rc), f32)],
        compiler_params=_cparams(("arbitrary", "arbitrary")),
    )(nq_b, kse, feat_t, cmp, cmp_t, misc, selw_t)


def _matmul_kernel(x_ref, w_ref, o_ref):
    o_ref[...] = _dot(x_ref[...].astype(bf16), w_ref[...])


def _sample_inproj(x2, w_pack, tm):
    n, d = x2.shape
    pw = w_pack.shape[1]
    return pl.pallas_call(
        _matmul_kernel,
        grid=(n // tm,),
        in_specs=[pl.BlockSpec((tm, d), lambda i: (i, 0)), pl.BlockSpec((d, pw), lambda i: (0, 0))],
        out_specs=pl.BlockSpec((tm, pw), lambda i: (i, 0)),
        out_shape=jax.ShapeDtypeStruct((n, pw), f32),
        compiler_params=_cparams(("arbitrary",)),
    )(x2, w_pack)


def _pad_to(a, rows):
    return jnp.concatenate([a, jnp.zeros((rows - a.shape[0], a.shape[1]), a.dtype)], axis=0)


def _softmax_tiles(tiles):
    m = tiles[0].max(axis=-1, keepdims=True)
    for s in tiles[1:]:
        m = jnp.maximum(m, s.max(axis=-1, keepdims=True))
    ps = [jnp.exp(s - m) for s in tiles]
    tot = ps[0].sum(axis=-1, keepdims=True)
    for p in ps[1:]:
        tot = tot + p.sum(axis=-1, keepdims=True)
    return ps, tot


def _sample_mixer_kernel(pt_ref, h_ref, *refs, n_pages, page, past, n_sel_pad, n_seq):
    del pt_ref
    runs = [_sample_mixer_seq(s, h_ref, refs, n_pages, page, past, n_sel_pad, n_seq) for s in range(n_seq)]
    done = object()
    while runs:
        runs = [run for run in runs if next(run, done) is not done]


def _sample_mixer_seq(s, h_ref, refs, n_pages, page, past, n_sel_pad, n_seq):
    per = n_seq * n_pages
    fox_pages = refs[s * n_pages:(s + 1) * n_pages]
    lf_pages = refs[per + s * n_pages:per + (s + 1) * n_pages]
    nsa_pages = refs[2 * per + s * n_pages:2 * per + (s + 1) * n_pages]
    (win_ref, conv_ref, fb_ref, wst_ref, bs_ref, cw_ref, avg_ref, utri_ref, ones_ref, slt_ref, tri8_ref,
     pe_ref, w1_ref, w2_ref, selw_ref, exp_ref,
     y_ref, fkv_ref, misc_ref, nrow_ref, wout_ref, cout_ref, gv_ref, rows_all) = refs[3 * per:]
    rows_sc = rows_all.at[s]
    tn = h_ref.shape[1]
    h = h_ref[s]
    qpos = past + lax.broadcasted_iota(jnp.int32, (tn, 1), 0)
    qpos_s = jnp.concatenate([qpos] * GROUP_HEADS, axis=0)
    new_pos = past + lax.broadcasted_iota(jnp.int32, (1, LANES), 1)
    new_ok = new_pos <= qpos_s

    ckv = h[:, CKV:CKV + 6 * HEAD_DIM]
    nrow_ref[s] = ckv[:, 0:4 * HEAD_DIM]
    for pg in range(n_pages):
        rows_sc[pg * page:(pg + 1) * page, :] = jnp.transpose(nsa_pages[pg][0, 0, 0:2 * HEAD_DIM, :])
    n_grp = n_pages * page // CMP_STRIDE
    half = CMP_LEN // 2
    tops = []
    bots = []
    for r in range(half):
        a_r = rows_sc[pl.ds(r, n_grp, stride=CMP_STRIDE), :]
        tops.append((a_r + pe_ref[r:r + 1, :]).astype(bf16))
        bots.append((a_r + pe_ref[half + r:half + r + 1, :]).astype(bf16))
    top = _dot(jnp.concatenate(tops, axis=1), w1_ref[0])
    bot = _dot(jnp.concatenate(bots, axis=1), w1_ref[1])
    g = _gelu(top + pltpu.roll(bot, n_grp - 1, 0)).astype(bf16)
    w2 = w2_ref[...]
    cmp = jnp.concatenate([_dot(g[:, 0:CMP_HID], w2[:, 0:HEAD_DIM]),
                           _dot(g[:, CMP_HID:2 * CMP_HID], w2[:, HEAD_DIM:2 * HEAD_DIM])], axis=-1).astype(bf16)
    yield
    qs = _stack_heads((h[:, CQ:CQ + GROUP_W] * SCALE).astype(bf16), tn)
    o_c, imp = _nsa_compressed(qs, cmp, qpos_s, selw_ref[...], tn)
    sel = _nsa_select(imp, new_pos, n_sel_pad).astype(bf16)

    pre = h[:, MISC:MISC + LANES] + fb_ref[0:1, :]
    lane = lax.broadcasted_iota(jnp.int32, (tn, LANES), 1)
    lf_new = jnp.where(lane < N_FOX_F, _log_sigmoid(pre), 0.0)
    misc = jnp.where(lane < N_FOX_F, lf_new, jnp.where(lane < N_FOX_F + N_GATES, jax.nn.sigmoid(pre), 0.0))
    misc_ref[s] = misc

    lf3 = _split3(jnp.concatenate([r[0, 0] for r in lf_pages], axis=0))
    tot = _dot_exact_lhs(lf3, ones_ref[...])
    off = _dot_exact_rhs(slt_ref[...], _split3(_pad_to(tot, LANES)))
    c_past = _dot_exact_lhs(lf3, utri_ref[...]) + off
    last = GROUP_HEADS * (n_pages - 1)
    total_past = off[last:last + GROUP_HEADS] + tot[last:last + GROUP_HEADS]
    c_new = jnp.transpose(_pad_to(total_past, LANES))[0:1, :] + _dot_exact_rhs(
        tri8_ref[...], _split3(_pad_to(lf_new, LANES)))
    cq = jnp.concatenate([c_new[:, hh:hh + 1] for hh in range(GROUP_HEADS)], axis=0)
    c_new_t = jnp.transpose(_pad_to(c_new, LANES))

    qbd =_head_blockdiag_q((h[:, BQ:BQ + GROUP_W] * SCALE).astype(bf16), tn)
    kv_new = h[:, BK:BK + 2 * GROUP_W]
    fkv_ref[s] = kv_new
    kvb_new = _pad_to(kv_new, LANES).astype(bf16)

    def head_rows(c4):
        return jnp.concatenate([jnp.broadcast_to(c4[hh:hh + 1, :], (tn, c4.shape[1])) for hh in range(GROUP_HEADS)],
                               axis=0)

    tiles = []
    for pg in range(n_pages):
        k_t = fox_pages[pg][0, 0, 0:GROUP_W, :].astype(bf16)
        tiles.append(_dot(qbd, k_t) + cq - head_rows(c_past[GROUP_HEADS * pg:GROUP_HEADS * (pg + 1)]))
    tiles.append(jnp.where(new_ok, _dot_nt(qbd, kvb_new[:, 0:GROUP_W]) + cq - head_rows(c_new_t[0:GROUP_HEADS]), NEG))
    yield
    ps, tot_p = _softmax_tiles(tiles)
    acc = _dot(ps[n_pages].astype(bf16), kvb_new[:, GROUP_W:2 * GROUP_W])
    for pg in range(n_pages):
        acc = acc + _dot_nt(ps[pg].astype(bf16), fox_pages[pg][0, 0, GROUP_W:2 * GROUP_W, :].astype(bf16))
    y_b = _head_diag(acc / tot_p, tn)

    u = _gelu(h[:, AU:AU + GROUP_W])
    vn = _head_layer_norm(_gelu(h[:, AV:AV + GROUP_W]), avg_ref[...])
    gv_ref[s] = vn
    w8 = jnp.concatenate([wst_ref[hh * CHUNK:hh * CHUNK + tn, :] for hh in range(GROUP_HEADS)], axis=0)
    z_all = _dot(w8, _pad_to(vn, CHUNK).astype(bf16))
    y_a = u * (_head_diag(z_all, tn) + bs_ref[0:tn, :])

    z = h[:, DC:DC + GROUP_W] * h[:, DH:DH + GROUP_W]
    zp = conv_ref[0, s]
    row = lax.broadcasted_iota(jnp.int32, (tn, GROUP_W), 0)
    z1 = jnp.where(row == 0, zp[1:2, :], pltpu.roll(z, 1, 0))
    z2 = jnp.where(row == 0, zp[0:1, :], jnp.where(row == 1, zp[1:2, :], pltpu.roll(z, 2, 0)))
    cw = cw_ref[...]
    y_d = h[:, DB:DB + GROUP_W] * (cw[0:1, :] * z2 + cw[1:2, :] * z1 + cw[2:3, :] * z)
    cout_ref[s] = z

    yield
    sel_new =_pad_to(ckv[:, 2 * HEAD_DIM:4 * HEAD_DIM], LANES).astype(bf16)
    tiles = []
    for pg in range(n_pages + 1):
        picked = _dot(sel, exp_ref[:, pg * page:(pg + 1) * page])
        if pg == n_pages:
            picked = jnp.where(new_pos <= qpos, picked, 0.0)
            sc = _dot_nt(qs, sel_new[:, 0:HEAD_DIM])
        else:
            sc = _dot(qs, nsa_pages[pg][0, 0, 2 * HEAD_DIM:3 * HEAD_DIM, :].astype(bf16))
        keep = jnp.concatenate([picked] * GROUP_HEADS, axis=0) > 0.5
        tiles.append(jnp.where(keep, sc, NEG))
    yield
    ps, tot_p = _softmax_tiles(tiles)
    acc = _dot(ps[n_pages].astype(bf16), sel_new[:, HEAD_DIM:2 * HEAD_DIM])
    for pg in range(n_pages):
        acc = acc + _dot_nt(ps[pg].astype(bf16), nsa_pages[pg][0, 0, 3 * HEAD_DIM:4 * HEAD_DIM, :].astype(bf16))
    o_s = acc / tot_p

    yield
    win_old = win_ref[0, s]
    win_new = ckv[:, 4 * HEAD_DIM:6 * HEAD_DIM]
    new_t = pltpu.roll(jnp.transpose(_pad_to(win_new, LANES)), LANES - tn, 1)
    new_wide = jnp.concatenate([jnp.zeros((2 * HEAD_DIM, WINDOW - LANES), f32), new_t], axis=1)
    w_idx = lax.broadcasted_iota(jnp.int32, (1, WINDOW), 1)
    wout_ref[s] = jnp.where(w_idx >= WINDOW - tn, new_wide, pltpu.roll(win_old, WINDOW - tn, 1))
    wob = win_old.astype(bf16)
    wnb = _pad_to(win_new, LANES).astype(bf16)
    s_old = jnp.where(w_idx > qpos_s - past, _dot(qs, wob[0:HEAD_DIM, :]), NEG)
    s_new = jnp.where(new_ok, _dot_nt(qs, wnb[:, 0:HEAD_DIM]), NEG)
    ps, tot_p = _softmax_tiles([s_old, s_new])
    o_w = (_dot_nt(ps[0].astype(bf16), wob[HEAD_DIM:2 * HEAD_DIM, :])
           + _dot(ps[1].astype(bf16), wnb[:, HEAD_DIM:2 * HEAD_DIM])) / tot_p

    o = _gate_col(misc, 0) * o_c + _gate_col(misc, 1) * o_s + _gate_col(misc, 2) * o_w
    y_c = _unstack_heads(o, tn)
    y_ref[s] = jnp.concatenate([y_a, y_b, y_c, y_d], axis=-1)


def _sample_mixer(h3, fox_t, lf_t, nsa_t, win_t, conv4, page_table, layer, consts, n_seq=2):
    n_all, tn, pw = h3.shape
    assert n_all % n_seq == 0
    bsz = n_all // n_seq
    n_pages = page_table.shape[1]
    page = fox_t.shape[3]
    past = n_pages * page
    n_sel_pad = -(-(-(-(past + tn) // SEL_LEN)) // 8) * 8

    def page_spec(arr, s, pg):
        blk = (1, 1) + arr.shape[2:]
        return pl.BlockSpec(blk, lambda b, pt: (layer, pt[n_seq * b + s, pg], 0, 0))

    in_specs = [pl.BlockSpec((n_seq, tn, pw), lambda b, pt: (b, 0, 0))]
    operands = [h3]
    for arr in (fox_t, lf_t, nsa_t):
        for s in range(n_seq):
            for pg in range(n_pages):
                in_specs.append(page_spec(arr, s, pg))
                operands.append(arr)
    in_specs.append(pl.BlockSpec((1, n_seq) + win_t.shape[2:], lambda b, pt: (layer, b, 0, 0)))
    operands.append(win_t)
    in_specs.append(pl.BlockSpec((1, n_seq) + conv4.shape[2:], lambda b, pt: (layer, b, 0, 0)))
    operands.append(conv4)
    for a in consts:
        in_specs.append(pl.BlockSpec(a.shape, lambda b, pt, nd=a.ndim: (0,) * nd))
        operands.append(a)
    out_dims = [(tn, 4 * GROUP_W), (tn, 2 * GROUP_W), (tn, LANES), (tn, 4 * HEAD_DIM), (2 * HEAD_DIM, WINDOW),
                (tn, GROUP_W), (tn, GROUP_W)]
    out_shape = [jax.ShapeDtypeStruct((n_all, r, w), f32) for r, w in out_dims]
    out_specs = [pl.BlockSpec((n_seq, r, w), lambda b, pt: (b, 0, 0)) for r, w in out_dims]
    return pl.pallas_call(
        functools.partial(_sample_mixer_kernel, n_pages=n_pages, page=page, past=past, n_sel_pad=n_sel_pad,
                          n_seq=n_seq),
        grid_spec=pltpu.PrefetchScalarGridSpec(num_scalar_prefetch=1, grid=(bsz,), in_specs=in_specs,
                                               out_specs=out_specs,
                                               scratch_shapes=[pltpu.VMEM((n_seq, past, 2 * HEAD_DIM), f32)]),
        out_shape=out_shape,
        compiler_params=_cparams(("arbitrary",)),
    )(page_table, *operands)


def _outproj_kernel(x_ref, *refs, alpha):
    y_refs = refs[:-4]
    w_ref, g_ref, b_ref, o_ref = refs[-4:]
    acc = None
    c0 = 0
    for y_ref in y_refs:
        wy = y_ref.shape[1]
        part = _dot(y_ref[...].astype(bf16), w_ref[c0:c0 + wy, :])
        acc = part if acc is None else acc + part
        c0 += wy
    o_ref[...] = _layer_norm_rows(alpha * x_ref[...] + acc, g_ref[0:1, :], b_ref[0:1, :])


def _outproj(x2, ys, w_out_b, g8, b8, alpha, tm):
    n, d = x2.shape
    assert sum(y.shape[1] for y in ys) == d
    const = lambda a: pl.BlockSpec(a.shape, lambda i: (0,) * a.ndim)
    return pl.pallas_call(
        functools.partial(_outproj_kernel, alpha=alpha),
        grid=(n // tm,),
        in_specs=[pl.BlockSpec((tm, d), lambda i: (i, 0))] + [pl.BlockSpec((tm, y.shape[1]), lambda i: (i, 0)) for y in ys]
        + [const(w_out_b), const(g8), const(b8)],
        out_specs=pl.BlockSpec((tm, d), lambda i: (i, 0)),
        out_shape=jax.ShapeDtypeStruct((n, d), f32),
        compiler_params=_cparams(("arbitrary",)),
    )(x2, *ys, w_out_b, g8, b8)


def _route(xb, wr_t, tm):
    logits = _dot_nt(wr_t, xb)
    e = jnp.exp(logits - jnp.max(logits, axis=0, keepdims=True))
    probs = e / jnp.sum(e, axis=0, keepdims=True)
    p = [probs[i:i + 1, :] for i in range(N_EXPERTS)]
    one = jnp.ones((1, tm), f32)
    zero = jnp.zeros((1, tm), f32)
    in_top = []
    gscore = []
    for g in range(N_GROUPS):
        members = range(g * EXP_PER_GROUP, (g + 1) * EXP_PER_GROUP)
        score = zero
        for i in members:
            rank = zero
            for j in members:
                if j < i:
                    rank = rank + jnp.where(p[j] >= p[i], one, zero)
                elif j > i:
                    rank = rank + jnp.where(p[j] > p[i], one, zero)
            keep = jnp.where(rank < TOP_K, one, zero)
            in_top.append(keep)
            score = score + keep * p[i]
        gscore.append(score)
    gate_rows = []
    wsum = zero
    chosen = []
    for g in range(N_GROUPS):
        win = one
        for g2 in range(N_GROUPS):
            if g2 < g:
                win = win * jnp.where(gscore[g] > gscore[g2], one, zero)
            elif g2 > g:
                win = win * jnp.where(gscore[g] >= gscore[g2], one, zero)
        for i in range(g * EXP_PER_GROUP, (g + 1) * EXP_PER_GROUP):
            c = in_top[i] * win
            chosen.append(c)
            wsum = wsum + c * p[i]
    for i in range(N_EXPERTS):
        gate_rows.append(jnp.where(chosen[i] > 0.5, p[i] / wsum, zero))
    gate_t = jnp.concatenate(gate_rows + [jnp.zeros((LANES - N_EXPERTS, tm), f32)], axis=0)
    return jnp.concatenate([jnp.transpose(gate_t[:, c * LANES:(c + 1) * LANES]) for c in range(tm // LANES)], axis=0)


def _moe_kernel(x_ref, wr_ref, wg_ref, wu_ref, wd_ref, g_ref, b_ref, o_ref, gate_sc, acc_sc, xb_sc, *, alpha, tm):
    grp = pl.program_id(1)
    n_e = wg_ref.shape[0]

    @pl.when(grp == 0)
    def _():
        xb = x_ref[...].astype(bf16)
        xb_sc[...] = xb
        gate_sc[...] = _route(xb, wr_ref[...], tm)

    xb = xb_sc[...]
    lane = lax.broadcasted_iota(jnp.int32, (tm, LANES), 1)
    gate = gate_sc[...]
    gates = [_dot(xb, wg_ref[j]) for j in range(n_e)]
    ups = [_dot(xb, wu_ref[j]) for j in range(n_e)]
    hs = []
    for j in range(n_e):
        gcol = jnp.sum(jnp.where(lane == grp * n_e + j, gate, 0.0), axis=-1, keepdims=True)
        hs.append((jax.nn.silu(gates[j]) * ups[j] * gcol).astype(bf16))
    y = _dot(jnp.concatenate(hs, axis=-1), wd_ref[...].reshape(n_e * wd_ref.shape[1], wd_ref.shape[2]))

    @pl.when(grp == 0)
    def _():
        acc_sc[...] = y

    @pl.when(grp > 0)
    def _():
        acc_sc[...] += y

    @pl.when(grp == pl.num_programs(1) - 1)
    def _():
        o_ref[...] = _layer_norm_rows(alpha * x_ref[...] + acc_sc[...], g_ref[0:1, :], b_ref[0:1, :])


def _moe(x2, wr_t, wg, wu, wd, g8, b8, alpha, tm):
    n, d = x2.shape
    de = wg.shape[-1]
    es = EXP_PER_GROUP
    const = lambda a: pl.BlockSpec(a.shape, lambda i, e: (0,) * a.ndim)
    return pl.pallas_call(
        functools.partial(_moe_kernel, alpha=alpha, tm=tm),
        grid=(n // tm, N_EXPERTS // es),
        in_specs=[
            pl.BlockSpec((tm, d), lambda i, e: (i, 0)), const(wr_t),
            pl.BlockSpec((es, d, de), lambda i, e: (e, 0, 0)), pl.BlockSpec((es, d, de), lambda i, e: (e, 0, 0)),
            pl.BlockSpec((es, de, d), lambda i, e: (e, 0, 0)), const(g8), const(b8),
        ],
        out_specs=pl.BlockSpec((tm, d), lambda i, e: (i, 0)),
        out_shape=jax.ShapeDtypeStruct((n, d), f32),
        scratch_shapes=[pltpu.VMEM((tm, LANES), f32), pltpu.VMEM((tm, d), f32), pltpu.VMEM((tm, d), bf16)],
        compiler_params=_cparams(("arbitrary", "arbitrary")),
    )(x2, wr_t, wg, wu, wd, g8, b8)


def _pad_rows(a, rows=8):
    return jnp.concatenate([a, jnp.zeros((rows - a.shape[0],) + a.shape[1:], a.dtype)], axis=0)


def _prep_layer(l, w_in, fox_b_f, gmlp_w_s, gmlp_b_s, nsa_cmp_pe, nsa_cmp_w1, nsa_cmp_w2, conv_w, w_out, ln_g, ln_b,
                w_exp_gate, w_exp_up, w_exp_down):
    w = w_in[l]
    d = w.shape[0]
    w_pack = jnp.concatenate([w[:, 0:1280], w[:, 1284:1924], w[:, 1936:2704], w[:, 1280:1284], w[:, 1924:1936],
                              jnp.zeros((d, LANES - N_FOX_F - N_GATES), w.dtype)], axis=1).astype(bf16)
    fb_row = _pad_rows(jnp.concatenate([fox_b_f[l], jnp.zeros((LANES - N_FOX_F,), f32)])[None, :])
    tril = jnp.asarray(np.tril(np.ones((CHUNK, CHUNK), np.float32)))
    wst = (gmlp_w_s[l] * tril).reshape(GROUP_HEADS * CHUNK, CHUNK).astype(bf16)
    bs_full = jnp.repeat(gmlp_b_s[l].T, HEAD_DIM, axis=1)
    cw8 = _pad_rows(conv_w[l])
    half = CMP_LEN // 2
    w1r = nsa_cmp_w1[l].reshape(2, CMP_LEN, HEAD_DIM, CMP_HID)
    per = nsa_cmp_pe[l]
    w2c = jnp.concatenate([nsa_cmp_w2[l, 0], nsa_cmp_w2[l, 1]], axis=1).astype(bf16)
    zed = jnp.zeros((CMP_LEN, HEAD_DIM, CMP_HID), f32)
    w1r = jnp.concatenate([jnp.concatenate([w1r[0], zed], axis=2), jnp.concatenate([zed, w1r[1]], axis=2)],
                          axis=1).astype(bf16)
    pe32 = jnp.concatenate([per[0], per[1]], axis=1)
    w1g = w1r.reshape(2, half * 2 * HEAD_DIM, 2 * CMP_HID)
    pe2 = _pad_rows(pe32.reshape(2, half * 2 * HEAD_DIM))
    w_t = jnp.concatenate([w_pack[:, BK:BK + 2 * GROUP_W], w_pack[:, CKV:CKV + 6 * HEAD_DIM]], axis=1).T
    return dict(
        w_pack=w_pack, w_t=w_t, fb_row=fb_row, wst=wst, bs_full=bs_full, cw8=cw8, w1g=w1g, pe2=pe2, w2c=w2c, w1r=w1r,
        pe32=pe32,
        w_out=w_out[l].astype(bf16), g1=_pad_rows(ln_g[l, 0:1]), b1=_pad_rows(ln_b[l, 0:1]),
        g2=_pad_rows(ln_g[l, 1:2]), b2=_pad_rows(ln_b[l, 1:2]),
        wg=w_exp_gate[l].astype(bf16), wu=w_exp_up[l].astype(bf16), wd=w_exp_down[l].astype(bf16),
    )


def _sel_overlap(t, n_sel):
    n_cmp = (t - CMP_LEN) // CMP_STRIDE + 1
    start = np.arange(n_cmp) * CMP_STRIDE
    sstart = np.arange(n_sel) * SEL_LEN
    ov = np.minimum(start[:, None] + CMP_LEN, sstart[None, :] + SEL_LEN) - np.maximum(start[:, None], sstart[None, :])
    return np.clip(ov, 0, None) / CMP_STRIDE


def _head_avg():
    a = np.kron(np.eye(GROUP_HEADS, dtype=np.float32), np.full((HEAD_DIM, HEAD_DIM), 1.0 / HEAD_DIM, np.float32))
    return jnp.asarray(a, bf16)


def _post_block(x2, ys, p, wr_t, alpha, tm):
    x1 = _outproj(x2, ys, p['w_out'], p['g1'], p['b1'], alpha, tm)
    return _moe(x1, wr_t, p['wg'], p['wu'], p['wd'], p['g2'], p['b2'], alpha, tm)


def _prompt_layer(x, p, wr_t, alpha, tm=512, tk=512, bq=512, bq_fox=512):
    bsz, t, d = x.shape
    n = bsz * t
    n_sel = t // SEL_LEN
    n_grp = t // CMP_STRIDE
    assert t % tm == 0 and t % tk == 0 and n_sel <= LANES and t >= WINDOW + bq
    tri = jnp.asarray(np.triu(np.ones((tm, tm), np.float32)), bf16)
    expand_t = np.zeros((t, LANES), np.float32)
    expand_t[np.arange(t), np.arange(t) // SEL_LEN] = 1.0
    (ya, fq, fk, fvt, fkvt, misc, ccol, crow, nq, nkvt, wint, kse, featt, kcvc, yd, cst) = _prompt_inproj(
        x, jnp.asarray(expand_t, bf16), p['w_pack'], p['w_t'], p['fb_row'], p['wst'], p['bs_full'], p['cw8'],
        _head_avg(), tri, tm)
    yb = _fox_prompt(fq, fk, fvt, ccol, crow, tk, bq_fox)
    cmp, cmp_t = _compress(kcvc.reshape(bsz, n_grp, CMP_STRIDE * 2 * HEAD_DIM), p['pe2'], p['w1g'], p['w2c'])
    ov = _sel_overlap(t, n_sel)
    selw_t = np.zeros((LANES, n_grp), np.float32)
    selw_t[:n_sel, :ov.shape[0]] = ov.T
    yc = _nsa_prompt(nq, kse, featt, cmp, cmp_t, misc, jnp.asarray(selw_t, bf16), tk, bq)
    ys = [a.reshape(n, GROUP_W) for a in (ya, yb, yc, yd)]
    x_out = _post_block(x.reshape(n, d), ys, p, wr_t, alpha, tm).reshape(bsz, t, d)
    win_keep = min(WINDOW, t)
    states = (fkvt.reshape(bsz, 2, GROUP_HEADS, HEAD_DIM, t).transpose(0, 4, 1, 2, 3), misc[:, :, 0:N_FOX_F],
              nkvt.reshape(bsz, 4, HEAD_DIM, t).transpose(0, 3, 1, 2),
              wint[:, :, t - win_keep:].reshape(bsz, 2, HEAD_DIM, win_keep).transpose(0, 3, 1, 2),
              cst[:, 8 - (CONV_W - 1):])
    return x_out, states, (ya, yb, yc, yd)


def _sample_consts(p, n_pages, page, tn):
    utri = np.triu(np.ones((page, page), np.float32))
    ones = np.ones((page, page), np.float32)
    r = np.arange(GROUP_HEADS * n_pages)
    slt = np.zeros((GROUP_HEADS * n_pages, LANES), np.float32)
    slt[:, :r.size] = ((r[:, None] % GROUP_HEADS) == (r[None, :] % GROUP_HEADS)) & (r[None, :] // GROUP_HEADS < r[:, None] // GROUP_HEADS)
    tri8 = np.zeros((tn, LANES), np.float32)
    tri8[:, :tn] = np.tril(np.ones((tn, tn), np.float32))
    t_all = n_pages * page + tn
    n_sel = -(-t_all // SEL_LEN)
    ov = _sel_overlap(t_all, n_sel)
    n_grp = n_pages * page // CMP_STRIDE
    assert ov.shape[0] <= n_grp and n_sel <= LANES
    selw = np.zeros((n_grp, LANES), np.float32)
    selw[:ov.shape[0], :n_sel] = ov
    n_keys = (n_pages + 1) * page
    expand = np.zeros((LANES, n_keys), np.float32)
    expand[np.arange(n_keys) // SEL_LEN, np.arange(n_keys)] = 1.0
    as_b = lambda a: jnp.asarray(np.asarray(a, np.float32), bf16)
    return [p['fb_row'], p['wst'], p['bs_full'], p['cw8'], _head_avg(), as_b(utri), as_b(ones), as_b(slt),
            as_b(tri8), p['pe32'], p['w1g'], p['w2c'], as_b(selw), as_b(expand)]


def _sample_layer(x, views, page_table, layer, p, wr_t, alpha, tm=512):
    bsz, tn, d = x.shape
    n = bsz * tn
    fox_t, lf_t, nsa_t, win_t, conv4 = views
    n_pages = page_table.shape[1]
    page = fox_t.shape[3]
    assert tn == 8 and page == LANES and win_t.shape[3] == WINDOW and n_pages * page >= WINDOW
    assert GROUP_HEADS * n_pages <= LANES
    h = _sample_inproj(x.reshape(n, d), p['w_pack'], tm).reshape(bsz, tn, P_PACK)
    y, fkv, misc, nrow, wout, cout, gv = _sample_mixer(h, fox_t, lf_t, nsa_t, win_t, conv4, page_table, layer,
                                                       _sample_consts(p, n_pages, page, tn))
    x_out = _post_block(x.reshape(n, d), [y.reshape(n, d)], p, wr_t, alpha, tm).reshape(bsz, tn, d)
    win_state = wout.reshape(bsz, 2, HEAD_DIM, WINDOW).transpose(0, 3, 1, 2)
    states = (fkv.reshape(bsz, tn, 2, GROUP_HEADS, HEAD_DIM), misc[:, :, 0:N_FOX_F], nrow.reshape(bsz, tn, 4, HEAD_DIM),
              win_state, cout[:, tn - (CONV_W - 1):], gv.reshape(bsz, tn, GROUP_HEADS, HEAD_DIM))
    return x_out, states


def kernel(x_prompt, x_sample, cache_fox_kv, cache_fox_logf, cache_nsa_kv, state_nsa_win, state_conv, page_table,
           w_in, fox_b_f, gmlp_w_s, gmlp_b_s, nsa_cmp_pe, nsa_cmp_w1, nsa_cmp_w2, conv_w, w_out, ln_g, ln_b,
           w_router, w_exp_gate, w_exp_up, w_exp_down):
    depth, n_pool, page = cache_fox_kv.shape[0:3]
    dec_b = x_sample.shape[0]
    alpha = (2 * depth) ** 0.25
    wr_t = w_router.T.astype(bf16)
    views = (cache_fox_kv.transpose(0, 1, 3, 4, 5, 2).reshape(depth, n_pool, 2 * GROUP_W, page),
             cache_fox_logf.transpose(0, 1, 3, 2),
             cache_nsa_kv.transpose(0, 1, 3, 4, 2).reshape(depth, n_pool, 4 * HEAD_DIM, page),
             state_nsa_win.transpose(0, 1, 3, 4, 2).reshape(depth, dec_b, 2 * HEAD_DIM, state_nsa_win.shape[2]),
             state_conv)
    xp, xs = x_prompt, x_sample
    st_p = [[] for _ in range(5)]
    st_s = [[] for _ in range(6)]
    for l in range(depth):
        p = _prep_layer(l, w_in, fox_b_f, gmlp_w_s, gmlp_b_s, nsa_cmp_pe, nsa_cmp_w1, nsa_cmp_w2, conv_w, w_out,
                        ln_g, ln_b, w_exp_gate, w_exp_up, w_exp_down)
        xp, sp, _ = _prompt_layer(xp, p, wr_t, alpha)
        xs, ss = _sample_layer(xs, views, page_table, l, p, wr_t, alpha)
        for lst, a in zip(st_p, sp):
            lst.append(a)
        for lst, a in zip(st_s, ss):
            lst.append(a)
    fox_kv_p, fox_lf_p, nsa_kv_p, nsa_win_p, conv_p = [jnp.stack(a, axis=0) for a in st_p]
    fox_kv_s, fox_lf_s, nsa_kv_s, nsa_win_s, conv_s, gmlp_v_s = [jnp.stack(a, axis=0) for a in st_s]
    return (xp, xs, fox_kv_p, fox_kv_s, fox_lf_p, fox_lf_s, nsa_kv_p, nsa_kv_s, nsa_win_p, nsa_win_s, conv_p, conv_s,
            gmlp_v_s)
```

```python
import functools

import numpy as np
import jax
import jax.numpy as jnp
from jax import lax
from jax.experimental import pallas as pl
from jax.experimental.pallas import tpu as pltpu

f32 = jnp.float32
bf16 = jnp.bfloat16

HEAD_DIM = 64
GROUP_W = 256
GROUP_HEADS = 4
CHUNK = 128
CMP_LEN = 32
CMP_STRIDE = 16
CMP_HID = 128
SEL_LEN = 64
SEL_TOPK = 16
WINDOW = 512
CONV_W = 3
N_EXPERTS = 16
N_GROUPS = 4
EXP_PER_GROUP = 4
TOP_K = 2
LN_EPS = 1e-5
SCALE = HEAD_DIM ** -0.5
NEG = -1e30
BIG = 1e30
LANES = 128

AU, AV, BQ, BK, BV, CQ, CKV, DH, DB, DC, MISC = 0, 256, 512, 768, 1024, 1280, 1536, 1920, 2176, 2432, 2688
P_PACK = 2816
N_FOX_F = GROUP_HEADS
N_GATES = 3 * GROUP_HEADS

VMEM_LIMIT = 56 * 1024 * 1024


def _cparams(sem):
    return pltpu.CompilerParams(dimension_semantics=sem, vmem_limit_bytes=VMEM_LIMIT)


def _dot(a, b):
    return jnp.dot(a, b, preferred_element_type=f32)


def _dot_nt(a, b):
    return lax.dot_general(a, b, (((1,), (1,)), ((), ())), preferred_element_type=f32)


def _split3(x):
    hi = x.astype(bf16)
    r = x - hi.astype(f32)
    mid = r.astype(bf16)
    lo = (r - mid.astype(f32)).astype(bf16)
    return hi, mid, lo


def _dot_exact_rhs(a, b3):
    return _dot(a, b3[0]) + _dot(a, b3[1]) + _dot(a, b3[2])


def _dot_exact_lhs(a3, b):
    return _dot(a3[0], b) + _dot(a3[1], b) + _dot(a3[2], b)


def _log_sigmoid(x):
    return jnp.minimum(x, 0.0) - jnp.log1p(jnp.exp(-jnp.abs(x)))


def _gelu(x):
    return jax.nn.gelu(x, approximate=True)


def _layer_norm_rows(x, g, b):
    mu = jnp.mean(x, axis=-1, keepdims=True)
    d = x - mu
    var = jnp.mean(d * d, axis=-1, keepdims=True)
    return d * lax.rsqrt(var + LN_EPS) * g + b


def _head_layer_norm(g, avg):
    h2 = g.astype(bf16)
    mu = _dot(h2, avg) + _dot((g - h2.astype(f32)).astype(bf16), avg)
    d = g - mu
    dd = d * d
    d2 = dd.astype(bf16)
    var = _dot(d2, avg) + _dot((dd - d2.astype(f32)).astype(bf16), avg)
    return d * lax.rsqrt(var + LN_EPS)


def _head_diag(z_all, rows):
    lane_head = lax.broadcasted_iota(jnp.int32, (rows, GROUP_W), 1) // HEAD_DIM
    out = jnp.where(lane_head == 0, z_all[0:rows], 0.0)
    for h in range(1, GROUP_HEADS):
        out = out + jnp.where(lane_head == h, z_all[h * rows:(h + 1) * rows], 0.0)
    return out


def _head_blockdiag_q(q, rows):
    lane_head = lax.broadcasted_iota(jnp.int32, (rows, GROUP_W), 1) // HEAD_DIM
    return jnp.concatenate([jnp.where(lane_head == h, q, jnp.zeros_like(q)) for h in range(GROUP_HEADS)], axis=0)


def _prompt_inproj_kernel(x_ref, exp_ref, w_ref, wt_ref, fb_ref, wst_ref, bs_ref, cw_ref, avg_ref, tri_ref,
                          ya_ref, fq_ref, fk_ref, fvt_ref, fkvt_ref, misc_ref, ccol_ref,
                          nq_ref, nkvt_ref, wint_ref, kse_ref, featt_ref, kcvc_ref, yd_ref, cst_ref,
                          carry_c, carry_z, *, tm):
    ti = pl.program_id(1)

    @pl.when(ti == 0)
    def _():
        carry_c[...] = jnp.zeros_like(carry_c)
        carry_z[...] = jnp.zeros_like(carry_z)

    xb = x_ref[0].astype(bf16)

    u = _gelu(_dot(xb, w_ref[:, AU:AU + GROUP_W]))
    vn = _head_layer_norm(_gelu(_dot(xb, w_ref[:, AV:AV + GROUP_W])), avg_ref[...])
    wst = wst_ref[...]
    bsf = bs_ref[...]
    for c in range(tm // CHUNK):
        r0 = c * CHUNK
        z_all = _dot(wst, vn[r0:r0 + CHUNK].astype(bf16))
        ya_ref[0, r0:r0 + CHUNK, :] = (u[r0:r0 + CHUNK] * (_head_diag(z_all, CHUNK) + bsf)).astype(bf16)

    fq_ref[0] = (_dot(xb, w_ref[:, BQ:BQ + GROUP_W]) * SCALE).astype(bf16)
    res_t = _dot_nt(wt_ref[...], xb)
    fkvt_ref[0] = res_t[0:2 * GROUP_W]
    fvt_ref[0] = res_t[GROUP_W:2 * GROUP_W].astype(bf16)
    nkvt_ref[0] = res_t[2 * GROUP_W:3 * GROUP_W]
    wint_ref[0] = res_t[3 * GROUP_W:3 * GROUP_W + 2 * HEAD_DIM]
    featt_ref[0] = jnp.concatenate([res_t[3 * GROUP_W - HEAD_DIM:3 * GROUP_W],
                                    res_t[3 * GROUP_W + HEAD_DIM:3 * GROUP_W + 2 * HEAD_DIM]], axis=0).astype(bf16)
    pre =_dot(xb, w_ref[:, MISC:MISC + LANES]) + fb_ref[0:1, :]
    lane = lax.broadcasted_iota(jnp.int32, (tm, LANES), 1)
    lf = jnp.where(lane < N_FOX_F, _log_sigmoid(pre), 0.0)
    misc_ref[0] = jnp.where(lane < N_FOX_F, lf, jnp.where(lane < N_FOX_F + N_GATES, jax.nn.sigmoid(pre), 0.0))
    c_t = _dot_exact_lhs(_split3(jnp.transpose(lf)[0:8, :]), tri_ref[...]) + carry_c[:, 0:1]
    carry_c[...] = jnp.broadcast_to(c_t[:, tm - 1:tm], (8, LANES))
    ccol_ref[0] = jnp.transpose(jnp.concatenate([c_t, jnp.zeros((LANES - 8, tm), f32)], axis=0))
    c3 = [piece.astype(f32) for piece in _split3(c_t)]
    k_tok = jnp.transpose(res_t[0:GROUP_W])
    parts = []
    for g in range(GROUP_HEADS // 2):
        rows = [-c3[i][h:h + 1, :] for h in (2 * g, 2 * g + 1) for i in range(3)]
        aux_t = jnp.concatenate(rows + [jnp.ones((3, tm), f32), jnp.zeros((LANES - 9, tm), f32)], axis=0)
        parts += [k_tok[:, g * LANES:(g + 1) * LANES].astype(bf16), jnp.transpose(aux_t).astype(bf16)]
    fk_ref[0] = jnp.concatenate(parts, axis=-1)

    nq_ref[0] = (_dot(xb, w_ref[:, CQ:CQ + GROUP_W]) * SCALE).astype(bf16)
    nsa0 = 2 * GROUP_W
    kcvc_ref[0] = jnp.transpose(res_t[nsa0:nsa0 + 2 * HEAD_DIM])
    ks_kw = jnp.transpose(jnp.concatenate([res_t[nsa0 + 2 * HEAD_DIM:nsa0 + 3 * HEAD_DIM],
                                           res_t[nsa0 + 4 * HEAD_DIM:nsa0 + 5 * HEAD_DIM]], axis=0))
    kse_ref[0] = jnp.concatenate([ks_kw.astype(bf16), exp_ref[...]], axis=-1)

    z = _dot(xb, w_ref[:, DC:DC + GROUP_W]) * _dot(xb, w_ref[:, DH:DH + GROUP_W])
    row = lax.broadcasted_iota(jnp.int32, (tm, GROUP_W), 0)
    zp = carry_z[...]
    z1 = jnp.where(row == 0, zp[7:8, :], pltpu.roll(z, 1, 0))
    z2 = jnp.where(row == 0, zp[6:7, :], jnp.where(row == 1, zp[7:8, :], pltpu.roll(z, 2, 0)))
    cw = cw_ref[...]
    y_conv = cw[0:1, :] * z2 + cw[1:2, :] * z1 + cw[2:3, :] * z
    yd_ref[0] = (_dot(xb, w_ref[:, DB:DB + GROUP_W]) * y_conv).astype(bf16)
    carry_z[...] = z[tm - 8:tm, :]
    cst_ref[0] = z[tm - 8:tm, :]


def _prompt_inproj(x, expand_t, w_pack, w_t, fb_row, wst, bs_full, cw8, avg, tri, tm):
    bsz, t, d = x.shape
    nt = t // tm
    tok = lambda w, dt: (jax.ShapeDtypeStruct((bsz, t, w), dt), pl.BlockSpec((1, tm, w), lambda b, i: (b, i, 0)))
    feat = lambda r, dt: (jax.ShapeDtypeStruct((bsz, r, t), dt), pl.BlockSpec((1, r, tm), lambda b, i: (b, 0, i)))
    const = lambda a: pl.BlockSpec(a.shape, lambda b, i: (0,) * a.ndim)
    outs = [
        tok(GROUP_W, bf16),
        tok(GROUP_W, bf16),
        tok(2 * GROUP_W, bf16),
        feat(GROUP_W, bf16),
        feat(2 * GROUP_W, f32),
        tok(LANES, f32),
        tok(LANES, f32),
        tok(GROUP_W, bf16),
        feat(4 * HEAD_DIM, f32),
        feat(2 * HEAD_DIM, f32),
        tok(GROUP_W, bf16),
        feat(2 * HEAD_DIM, bf16),
        tok(2 * HEAD_DIM, f32),
        tok(GROUP_W, bf16),
        (jax.ShapeDtypeStruct((bsz, 8, GROUP_W), f32), pl.BlockSpec((1, 8, GROUP_W), lambda b, i: (b, 0, 0))),
    ]
    consts = [w_pack, w_t, fb_row, wst, bs_full, cw8, avg, tri]
    return pl.pallas_call(
        functools.partial(_prompt_inproj_kernel, tm=tm),
        grid=(bsz, nt),
        in_specs=[pl.BlockSpec((1, tm, d), lambda b, i: (b, i, 0)), pl.BlockSpec((tm, LANES), lambda b, i: (i, 0))]
        + [const(a) for a in consts],
        out_specs=[o[1] for o in outs],
        out_shape=[o[0] for o in outs],
        scratch_shapes=[pltpu.VMEM((8, LANES), f32), pltpu.VMEM((8, GROUP_W), f32)],
        compiler_params=_cparams(("arbitrary", "arbitrary")),
    )(x, expand_t, *consts)


def _flash_update_t(chains):
    parts = []
    for s, v_t, m_sc, l_sc, acc_sc in chains:
        m_old = m_sc[...]
        m_new = jnp.maximum(m_old, jnp.max(s, axis=0, keepdims=True))
        alpha = jnp.exp(m_old - m_new)
        p = jnp.exp(s - m_new)
        l_sc[...] = alpha * l_sc[...] + jnp.sum(p, axis=0, keepdims=True)
        m_sc[...] = m_new
        parts.append((alpha, p.astype(bf16)))
    for (s, v_t, m_sc, l_sc, acc_sc), (alpha, pb) in zip(chains, parts):
        acc_sc[...] = alpha * acc_sc[...] + _dot(v_t, pb)


def _fox_prompt_kernel(q_ref, k_ref, vt_ref, ccol_ref, o_ref, m_sc, l_sc, acc_sc, *, tk, bq):
    qi = pl.program_id(1)
    n_pair = GROUP_HEADS // 2
    pw = 2 * HEAD_DIM
    nq = LANES
    n_sub = bq // nq
    lane = lax.broadcasted_iota(jnp.int32, (nq, pw), 1)
    lane_hi = lane >= HEAD_DIM
    qbd, qpos = [], []
    for u in range(n_sub):
        q = q_ref[0, u * nq:(u + 1) * nq, :]
        cc = ccol_ref[0, u * nq:(u + 1) * nq, :]
        qp = qi * bq + u * nq + lax.broadcasted_iota(jnp.int32, (1, nq), 1)
        qpos.append(jnp.concatenate([qp, qp], axis=1))
        for g in range(n_pair):
            qg = q[:, g * pw:(g + 1) * pw]
            halves = []
            for i, h in enumerate((2 * g, 2 * g + 1)):
                hi, mid, lo = [piece.astype(f32) for piece in _split3(cc[:, h:h + 1])]
                aux = jnp.where((lane >= 3 * i) & (lane < 3 * i + 3), 1.0, 0.0)
                aux = jnp.where(lane == 6, hi, jnp.where(lane == 7, mid, jnp.where(lane == 8, lo, aux)))
                own = jnp.where(lane_hi, qg, jnp.zeros_like(qg)) if i else jnp.where(lane_hi, jnp.zeros_like(qg), qg)
                halves.append(jnp.concatenate([own, aux.astype(bf16)], axis=1))
            qbd.append(jnp.concatenate(halves, axis=0))
    m_sc[...] = jnp.full_like(m_sc, NEG)
    l_sc[...] = jnp.zeros_like(l_sc)
    acc_sc[...] = jnp.zeros_like(acc_sc)

    def tile(j, masked):
        k0 = pl.multiple_of(j * tk, tk)
        chains = []
        for g in range(n_pair):
            k = k_ref[0, pl.ds(k0, tk), 2 * g * pw:2 * (g + 1) * pw]
            v_t = vt_ref[0, g * pw:(g + 1) * pw, pl.ds(k0, tk)]
            for u in range(n_sub):
                c = u * n_pair + g
                s = _dot_nt(k, qbd[c])
                if masked:
                    kpos = k0 + lax.broadcasted_iota(jnp.int32, (tk, 1), 0)
                    s = jnp.where(kpos <= qpos[u], s, NEG)
                chains.append((s, v_t, m_sc.at[c], l_sc.at[c], acc_sc.at[c]))
        _flash_update_t(chains)

    n_full = (qi * bq) // tk

    def body(j, carry):
        tile(j, False)
        return carry

    lax.fori_loop(0, n_full, body, 0)
    tile(n_full, True)
    for u in range(n_sub):
        outs = []
        for g in range(n_pair):
            c = u * n_pair + g
            o = acc_sc[c] / l_sc[c]
            outs.append(o[0:HEAD_DIM, 0:nq])
            outs.append(o[HEAD_DIM:pw, nq:2 * nq])
        o_ref[0, u * nq:(u + 1) * nq, :] = jnp.transpose(jnp.concatenate(outs, axis=0)).astype(bf16)


def _fox_prompt(fq, fk, fvt, ccol, tk, bq):
    bsz, t, _ = fq.shape
    assert tk % bq == 0 and t % tk == 0 and bq % LANES == 0
    n_chain = (bq // LANES) * (GROUP_HEADS // 2)
    return pl.pallas_call(
        functools.partial(_fox_prompt_kernel, tk=tk, bq=bq),
        grid=(bsz, t // bq),
        in_specs=[
            pl.BlockSpec((1, bq, GROUP_W), lambda b, i: (b, i, 0)),
            pl.BlockSpec((1, t, 2 * GROUP_W), lambda b, i: (b, 0, 0)),
            pl.BlockSpec((1, GROUP_W, t), lambda b, i: (b, 0, 0)),
            pl.BlockSpec((1, bq, LANES), lambda b, i: (b, i, 0)),
        ],
        out_specs=pl.BlockSpec((1, bq, GROUP_W), lambda b, i: (b, i, 0)),
        out_shape=jax.ShapeDtypeStruct((bsz, t, GROUP_W), bf16),
        scratch_shapes=[pltpu.VMEM((n_chain, 1, 2 * LANES), f32), pltpu.VMEM((n_chain, 1, 2 * LANES), f32),
                        pltpu.VMEM((n_chain, 2 * HEAD_DIM, 2 * LANES), f32)],
        compiler_params=_cparams(("arbitrary", "arbitrary")),
    )(fq, fk, fvt, ccol)


def _compress_kernel(x_ref, pe_ref, w1_ref, w2_ref, o_ref, ot_ref):
    x = x_ref[0]
    n_grp = x.shape[0]
    top = _dot((x + pe_ref[0:1, :]).astype(bf16), w1_ref[0])
    bot = _dot((x + pe_ref[1:2, :]).astype(bf16), w1_ref[1])
    g = _gelu(top + pltpu.roll(bot, n_grp - 1, 0)).astype(bf16)
    w2 = w2_ref[...]
    kc = _dot(g[:, 0:CMP_HID], w2[:, 0:HEAD_DIM])
    vc = _dot(g[:, CMP_HID:2 * CMP_HID], w2[:, HEAD_DIM:2 * HEAD_DIM])
    cmp = jnp.concatenate([kc, vc], axis=-1)
    o_ref[0] = cmp.astype(bf16)
    ot_ref[0] = jnp.transpose(cmp).astype(bf16)


def _compress(x_groups, pe2, w1g, w2c):
    bsz, n_grp, w = x_groups.shape
    const = lambda a: pl.BlockSpec(a.shape, lambda b: (0,) * a.ndim)
    return pl.pallas_call(
        _compress_kernel,
        grid=(bsz,),
        in_specs=[pl.BlockSpec((1, n_grp, w), lambda b: (b, 0, 0)), const(pe2), const(w1g), const(w2c)],
        out_specs=[pl.BlockSpec((1, n_grp, 2 * HEAD_DIM), lambda b: (b, 0, 0)),
                   pl.BlockSpec((1, 2 * HEAD_DIM, n_grp), lambda b: (b, 0, 0))],
        out_shape=[jax.ShapeDtypeStruct((bsz, n_grp, 2 * HEAD_DIM), bf16),
                   jax.ShapeDtypeStruct((bsz, 2 * HEAD_DIM, n_grp), bf16)],
        compiler_params=_cparams(("arbitrary",)),
    )(x_groups, pe2, w1g, w2c)


def _stack_heads(q, rows):
    return jnp.concatenate([q[:, h * HEAD_DIM:(h + 1) * HEAD_DIM] for h in range(GROUP_HEADS)], axis=0)


def _unstack_heads(o, rows):
    return jnp.concatenate([o[h * rows:(h + 1) * rows] for h in range(GROUP_HEADS)], axis=-1)


def _gate_col(misc, branch):
    base = N_FOX_F + branch * GROUP_HEADS
    return jnp.concatenate([misc[:, base + h:base + h + 1] for h in range(GROUP_HEADS)], axis=0)


def _nsa_compressed(qs, cmp, qpos_s, selw, nq):
    n_c = cmp.shape[0]
    kc = cmp[:, 0:HEAD_DIM]
    vc = cmp[:, HEAD_DIM:2 * HEAD_DIM]
    s = _dot_nt(qs, kc)
    cend = lax.broadcasted_iota(jnp.int32, (1, n_c), 1) * CMP_STRIDE + (CMP_LEN - 1)
    vis = cend <= qpos_s
    s = jnp.where(vis, s, NEG)
    e = jnp.exp(s - jnp.max(s, axis=-1, keepdims=True))
    p = e / jnp.sum(e, axis=-1, keepdims=True)
    p = p * jnp.where(qpos_s >= CMP_LEN - 1, 1.0, 0.0)
    pb = p.astype(bf16)
    o_c = _dot(pb, vc)
    imp = _dot(pb[0:nq], selw)
    for h in range(1, GROUP_HEADS):
        imp = imp + _dot(pb[h * nq:(h + 1) * nq], selw)
    return o_c, imp


def _nsa_select(imp, qpos_row, n_sel_pad):
    nq = imp.shape[0]
    if nq < LANES:
        imp = jnp.concatenate([imp, jnp.zeros((LANES - nq, LANES), f32)], axis=0)
    sel_t = _nsa_select_t(jnp.transpose(imp)[0:n_sel_pad], qpos_row)
    if n_sel_pad < LANES:
        sel_t = jnp.concatenate([sel_t, jnp.zeros((LANES - n_sel_pad, LANES), f32)], axis=0)
    return jnp.transpose(sel_t)[0:nq]


def _nsa_select_t(imp_t, qpos_row):
    n_sel_pad = imp_t.shape[0]
    blk = lax.broadcasted_iota(jnp.int32, (n_sel_pad, LANES), 0)
    cur = qpos_row // SEL_LEN
    valid = blk * SEL_LEN <= qpos_row
    forced = (blk == 0) | (blk == cur) | (blk == cur - 1)
    val = jnp.where(valid, jnp.where(forced, BIG, imp_t), -BIG)
    n_g = n_sel_pad // 8
    grp = [val[8 * g:8 * g + 8] for g in range(n_g)]
    rank = [jnp.zeros((8, LANES), f32) for _ in range(n_g)]
    row8 = lax.broadcasted_iota(jnp.int32, (8, LANES), 0)
    for mp in range(n_sel_pad):
        vb = jnp.broadcast_to(val[mp:mp + 1, :], (8, LANES))
        for g in range(n_g):
            if 8 * g + 7 < mp:
                hit = jnp.where(vb > grp[g], 1.0, 0.0)
            elif 8 * g > mp:
                hit = jnp.where(vb >= grp[g], 1.0, 0.0)
            else:
                hit = jnp.where(row8 > mp - 8 * g, jnp.where(vb >= grp[g], 1.0, 0.0), jnp.where(vb > grp[g], 1.0, 0.0))
            rank[g] = rank[g] + hit
    rank = jnp.concatenate(rank, axis=0)
    return jnp.where(valid, jnp.where(rank < SEL_TOPK, 1.0, 0.0), 0.0)


def _nsa_prompt_kernel(q_ref, kse_ref, ft_ref, cmp_ref, cmpt_ref, misc_ref, selwt_ref, o_ref, m_sc, l_sc, acc_sc, *,
                       tk, t, n_sel_pad, bq):
    qi = pl.program_id(1)
    nq = LANES
    n_sub = bq // nq
    r_all = GROUP_HEADS * nq
    zq = jnp.zeros((HEAD_DIM, r_all), bf16)
    cmp = cmp_ref[0]
    n_c = cmp.shape[0]
    span = min(WINDOW + nq, t)
    selwt = selwt_ref[...]
    sub = [dict() for _ in range(n_sub)]

    def prepare(u):
        d = sub[u]
        q0 = qi * bq + u * nq
        q_t4 = jnp.transpose(q_ref[0, u * nq:(u + 1) * nq, :].astype(f32))
        q_t = jnp.concatenate([q_t4[h * HEAD_DIM:(h + 1) * HEAD_DIM] for h in range(GROUP_HEADS)], axis=1).astype(bf16)
        qpos1 = q0 + lax.broadcasted_iota(jnp.int32, (1, nq), 1)
        qpos = jnp.concatenate([qpos1] * GROUP_HEADS, axis=1)
        d['qpos'] = qpos
        w0 = pl.multiple_of(jnp.maximum(q0 - WINDOW, 0), LANES)
        q_win = jnp.concatenate([zq, q_t, jnp.zeros((LANES, r_all), bf16)], axis=0)
        s_w = _dot(kse_ref[0, pl.ds(w0, span), :], q_win)
        s_c = _dot(cmp, jnp.concatenate([q_t, zq], axis=0))
        yield
        wpos = w0 + lax.broadcasted_iota(jnp.int32, (span, 1), 0)
        s_w = jnp.where((wpos <= qpos) & (wpos > qpos - WINDOW), s_w, NEG)
        cend = lax.broadcasted_iota(jnp.int32, (n_c, 1), 0) * CMP_STRIDE + (CMP_LEN - 1)
        s_c = jnp.where(cend <= qpos, s_c, NEG)
        e_c = jnp.exp(s_c - jnp.max(s_c, axis=0, keepdims=True))
        p_c = e_c / jnp.sum(e_c, axis=0, keepdims=True)
        pb = (p_c * jnp.where(qpos >= CMP_LEN - 1, 1.0, 0.0)).astype(bf16)
        e_w = jnp.exp(s_w - jnp.max(s_w, axis=0, keepdims=True))
        p_w = (e_w / jnp.sum(e_w, axis=0, keepdims=True)).astype(bf16)
        yield
        d['o_c'] = _dot(cmpt_ref[0, HEAD_DIM:2 * HEAD_DIM, :], pb)
        d['o_w'] = _dot(ft_ref[0, HEAD_DIM:2 * HEAD_DIM, pl.ds(w0, span)], p_w)
        imp_t = _dot(selwt, pb[:, 0:nq])
        for h in range(1, GROUP_HEADS):
            imp_t = imp_t + _dot(selwt, pb[:, h * nq:(h + 1) * nq])
        yield
        sel_t = _nsa_select_t(imp_t[0:n_sel_pad], qpos1)
        unsel = jnp.where(sel_t > 0.5, 0.0, -(2.0 ** 100))
        if n_sel_pad < LANES:
            unsel = jnp.concatenate([unsel, jnp.zeros((LANES - n_sel_pad, nq), f32)], axis=0)
        unsel = jnp.concatenate([unsel.astype(bf16)] * GROUP_HEADS, axis=1)
        d['q_sel'] = jnp.concatenate([q_t, zq, unsel], axis=0)

    runs = [prepare(u) for u in range(n_sub)]
    done = object()
    while runs:
        runs = [run for run in runs if next(run, done) is not done]

    m_sc[...] = jnp.full_like(m_sc, NEG)
    l_sc[...] = jnp.zeros_like(l_sc)
    acc_sc[...] = jnp.zeros_like(acc_sc)

    n_half = 2
    rc = r_all // n_half
    chain_ids = [(u, c) for u in range(n_sub) for c in range(n_half)]

    def tile(j, diagonal):
        k0 = pl.multiple_of(j * tk, tk)
        kse = kse_ref[0, pl.ds(k0, tk), :]
        v_t = ft_ref[0, 0:HEAD_DIM, pl.ds(k0, tk)]
        chains = []
        for i, (u, c) in enumerate(chain_ids):
            sc = _dot(kse, sub[u]['q_sel'][:, c * rc:(c + 1) * rc])
            if diagonal:
                kpos = k0 + lax.broadcasted_iota(jnp.int32, (tk, 1), 0)
                sc = jnp.where(kpos <= sub[u]['qpos'][:, c * rc:(c + 1) * rc], sc, NEG)
            chains.append((sc, v_t, m_sc.at[i], l_sc.at[i], acc_sc.at[i]))
        _flash_update_t(chains)

    n_full = (qi * bq) // tk

    def body(j, carry):
        tile(j, False)
        return carry

    lax.fori_loop(0, n_full, body, 0)
    tile(n_full, True)

    for u in range(n_sub):
        o_s = jnp.concatenate([acc_sc[u * n_half + c] / l_sc[u * n_half + c] for c in range(n_half)], axis=1)
        misc_t = jnp.transpose(misc_ref[0, u * nq:(u + 1) * nq, :])

        def gate(branch):
            base = N_FOX_F + branch * GROUP_HEADS
            return jnp.concatenate([misc_t[base + h:base + h + 1, :] for h in range(GROUP_HEADS)], axis=1)

        o = gate(0) * sub[u]['o_c'] + gate(1) * o_s + gate(2) * sub[u]['o_w']
        o = jnp.concatenate([o[:, h * nq:(h + 1) * nq] for h in range(GROUP_HEADS)], axis=0)
        o_ref[0, u * nq:(u + 1) * nq, :] = jnp.transpose(o).astype(bf16)


def _nsa_prompt(nq_b, kse, feat_t, cmp, cmp_t, misc, selw_t, tk, bq):
    bsz, t, _ = nq_b.shape
    n_chain = 2 * (bq // LANES)
    rc = GROUP_HEADS * LANES // 2
    n_c = cmp.shape[1]
    n_sel_pad = -(-(t // SEL_LEN) // 8) * 8
    assert tk % bq == 0 and bq % LANES == 0
    return pl.pallas_call(
        functools.partial(_nsa_prompt_kernel, tk=tk, t=t, n_sel_pad=n_sel_pad, bq=bq),
        grid=(bsz, t // bq),
        in_specs=[
            pl.BlockSpec((1, bq, GROUP_W), lambda b, i: (b, i, 0)),
            pl.BlockSpec((1, t, GROUP_W), lambda b, i: (b, 0, 0)),
            pl.BlockSpec((1, 2 * HEAD_DIM, t), lambda b, i: (b, 0, 0)),
            pl.BlockSpec((1, n_c, 2 * HEAD_DIM), lambda b, i: (b, 0, 0)),
            pl.BlockSpec((1, 2 * HEAD_DIM, n_c), lambda b, i: (b, 0, 0)),
            pl.BlockSpec((1, bq, LANES), lambda b, i: (b, i, 0)),
            pl.BlockSpec(selw_t.shape, lambda b, i: (0, 0)),
        ],
        out_specs=pl.BlockSpec((1, bq, GROUP_W), lambda b, i: (b, i, 0)),
        out_shape=jax.ShapeDtypeStruct((bsz, t, GROUP_W), bf16),
        scratch_shapes=[pltpu.VMEM((n_chain, 1, rc), f32), pltpu.VMEM((n_chain, 1, rc), f32),
                        pltpu.VMEM((n_chain, HEAD_DIM, rc), f32)],
        compiler_params=_cparams(("arbitrary", "arbitrary")),
    )(nq_b, kse, feat_t, cmp, cmp_t, misc, selw_t)


def _matmul_kernel(x_ref, w_ref, o_ref):
    o_ref[...] = _dot(x_ref[...].astype(bf16), w_ref[...])


def _sample_inproj(x2, w_pack, tm):
    n, d = x2.shape
    pw = w_pack.shape[1]
    return pl.pallas_call(
        _matmul_kernel,
        grid=(n // tm,),
        in_specs=[pl.BlockSpec((tm, d), lambda i: (i, 0)), pl.BlockSpec((d, pw), lambda i: (0, 0))],
        out_specs=pl.BlockSpec((tm, pw), lambda i: (i, 0)),
        out_shape=jax.ShapeDtypeStruct((n, pw), f32),
        compiler_params=_cparams(("arbitrary",)),
    )(x2, w_pack)


def _pad_to(a, rows):
    return jnp.concatenate([a, jnp.zeros((rows - a.shape[0], a.shape[1]), a.dtype)], axis=0)


def _softmax_tiles(tiles):
    m = tiles[0].max(axis=-1, keepdims=True)
    for s in tiles[1:]:
        m = jnp.maximum(m, s.max(axis=-1, keepdims=True))
    ps = [jnp.exp(s - m) for s in tiles]
    tot = ps[0].sum(axis=-1, keepdims=True)
    for p in ps[1:]:
        tot = tot + p.sum(axis=-1, keepdims=True)
    return ps, tot


def _sample_mixer_kernel(pt_ref, h_ref, *refs, n_pages, page, past, n_sel_pad, n_seq):
    del pt_ref
    runs = [_sample_mixer_seq(s, h_ref, refs, n_pages, page, past, n_sel_pad, n_seq) for s in range(n_seq)]
    done = object()
    while runs:
        runs = [run for run in runs if next(run, done) is not done]


def _sample_mixer_seq(s, h_ref, refs, n_pages, page, past, n_sel_pad, n_seq):
    per = n_seq * n_pages
    fox_pages = refs[s * n_pages:(s + 1) * n_pages]
    lf_pages = refs[per + s * n_pages:per + (s + 1) * n_pages]
    nsa_pages = refs[2 * per + s * n_pages:2 * per + (s + 1) * n_pages]
    (win_ref, conv_ref, fb_ref, wst_ref, bs_ref, cw_ref, avg_ref, utri_ref, ones_ref, slt_ref, tri8_ref,
     pe_ref, w1_ref, w2_ref, selw_ref, exp_ref,
     y_ref, fkv_ref, misc_ref, nrow_ref, wout_ref, cout_ref, gv_ref, rows_all) = refs[3 * per:]
    rows_sc = rows_all.at[s]
    tn = h_ref.shape[1]
    h = h_ref[s]
    qpos = past + lax.broadcasted_iota(jnp.int32, (tn, 1), 0)
    qpos_s = jnp.concatenate([qpos] * GROUP_HEADS, axis=0)
    new_pos = past + lax.broadcasted_iota(jnp.int32, (1, LANES), 1)
    new_ok = new_pos <= qpos_s

    ckv = h[:, CKV:CKV + 6 * HEAD_DIM]
    nrow_ref[s] = ckv[:, 0:4 * HEAD_DIM]
    for pg in range(n_pages):
        rows_sc[pg * page:(pg + 1) * page, :] = jnp.transpose(nsa_pages[pg][0, 0, 0:2 * HEAD_DIM, :])
    n_grp = n_pages * page // CMP_STRIDE
    half = CMP_LEN // 2
    tops = []
    bots = []
    for r in range(half):
        a_r = rows_sc[pl.ds(r, n_grp, stride=CMP_STRIDE), :]
        tops.append((a_r + pe_ref[r:r + 1, :]).astype(bf16))
        bots.append((a_r + pe_ref[half + r:half + r + 1, :]).astype(bf16))
    top = _dot(jnp.concatenate(tops, axis=1), w1_ref[0])
    bot = _dot(jnp.concatenate(bots, axis=1), w1_ref[1])
    g = _gelu(top + pltpu.roll(bot, n_grp - 1, 0)).astype(bf16)
    w2 = w2_ref[...]
    cmp = jnp.concatenate([_dot(g[:, 0:CMP_HID], w2[:, 0:HEAD_DIM]),
                           _dot(g[:, CMP_HID:2 * CMP_HID], w2[:, HEAD_DIM:2 * HEAD_DIM])], axis=-1).astype(bf16)
    yield
    qs = _stack_heads((h[:, CQ:CQ + GROUP_W] * SCALE).astype(bf16), tn)
    o_c, imp = _nsa_compressed(qs, cmp, qpos_s, selw_ref[...], tn)
    sel = _nsa_select(imp, new_pos, n_sel_pad).astype(bf16)

    pre = h[:, MISC:MISC + LANES] + fb_ref[0:1, :]
    lane = lax.broadcasted_iota(jnp.int32, (tn, LANES), 1)
    lf_new = jnp.where(lane < N_FOX_F, _log_sigmoid(pre), 0.0)
    misc = jnp.where(lane < N_FOX_F, lf_new, jnp.where(lane < N_FOX_F + N_GATES, jax.nn.sigmoid(pre), 0.0))
    misc_ref[s] = misc

    lf3 = _split3(jnp.concatenate([r[0, 0] for r in lf_pages], axis=0))
    tot = _dot_exact_lhs(lf3, ones_ref[...])
    off = _dot_exact_rhs(slt_ref[...], _split3(_pad_to(tot, LANES)))
    c_past = _dot_exact_lhs(lf3, utri_ref[...]) + off
    last = GROUP_HEADS * (n_pages - 1)
    total_past = off[last:last + GROUP_HEADS] + tot[last:last + GROUP_HEADS]
    c_new = jnp.transpose(_pad_to(total_past, LANES))[0:1, :] + _dot_exact_rhs(
        tri8_ref[...], _split3(_pad_to(lf_new, LANES)))
    cq = jnp.concatenate([c_new[:, hh:hh + 1] for hh in range(GROUP_HEADS)], axis=0)
    c_new_t = jnp.transpose(_pad_to(c_new, LANES))

    qbd =_head_blockdiag_q((h[:, BQ:BQ + GROUP_W] * SCALE).astype(bf16), tn)
    kv_new = h[:, BK:BK + 2 * GROUP_W]
    fkv_ref[s] = kv_new
    kvb_new = _pad_to(kv_new, LANES).astype(bf16)

    def head_rows(c4):
        return jnp.concatenate([jnp.broadcast_to(c4[hh:hh + 1, :], (tn, c4.shape[1])) for hh in range(GROUP_HEADS)],
                               axis=0)

    tiles = []
    for pg in range(n_pages):
        k_t = fox_pages[pg][0, 0, 0:GROUP_W, :].astype(bf16)
        tiles.append(_dot(qbd, k_t) + cq - head_rows(c_past[GROUP_HEADS * pg:GROUP_HEADS * (pg + 1)]))
    tiles.append(jnp.where(new_ok, _dot_nt(qbd, kvb_new[:, 0:GROUP_W]) + cq - head_rows(c_new_t[0:GROUP_HEADS]), NEG))
    yield
    ps, tot_p = _softmax_tiles(tiles)
    acc = _dot(ps[n_pages].astype(bf16), kvb_new[:, GROUP_W:2 * GROUP_W])
    for pg in range(n_pages):
        acc = acc + _dot_nt(ps[pg].astype(bf16), fox_pages[pg][0, 0, GROUP_W:2 * GROUP_W, :].astype(bf16))
    y_b = _head_diag(acc / tot_p, tn)

    u = _gelu(h[:, AU:AU + GROUP_W])
    vn = _head_layer_norm(_gelu(h[:, AV:AV + GROUP_W]), avg_ref[...])
    gv_ref[s] = vn
    w8 = jnp.concatenate([wst_ref[hh * CHUNK:hh * CHUNK + tn, :] for hh in range(GROUP_HEADS)], axis=0)
    z_all = _dot(w8, _pad_to(vn, CHUNK).astype(bf16))
    y_a = u * (_head_diag(z_all, tn) + bs_ref[0:tn, :])

    z = h[:, DC:DC + GROUP_W] * h[:, DH:DH + GROUP_W]
    zp = conv_ref[0, s]
    row = lax.broadcasted_iota(jnp.int32, (tn, GROUP_W), 0)
    z1 = jnp.where(row == 0, zp[1:2, :], pltpu.roll(z, 1, 0))
    z2 = jnp.where(row == 0, zp[0:1, :], jnp.where(row == 1, zp[1:2, :], pltpu.roll(z, 2, 0)))
    cw = cw_ref[...]
    y_d = h[:, DB:DB + GROUP_W] * (cw[0:1, :] * z2 + cw[1:2, :] * z1 + cw[2:3, :] * z)
    cout_ref[s] = z

    yield
    sel_new =_pad_to(ckv[:, 2 * HEAD_DIM:4 * HEAD_DIM], LANES).astype(bf16)
    tiles = []
    for pg in range(n_pages + 1):
        picked = _dot(sel, exp_ref[:, pg * page:(pg + 1) * page])
        if pg == n_pages:
            picked = jnp.where(new_pos <= qpos, picked, 0.0)
            sc = _dot_nt(qs, sel_new[:, 0:HEAD_DIM])
        else:
            sc = _dot(qs, nsa_pages[pg][0, 0, 2 * HEAD_DIM:3 * HEAD_DIM, :].astype(bf16))
        keep = jnp.concatenate([picked] * GROUP_HEADS, axis=0) > 0.5
        tiles.append(jnp.where(keep, sc, NEG))
    yield
    ps, tot_p = _softmax_tiles(tiles)
    acc = _dot(ps[n_pages].astype(bf16), sel_new[:, HEAD_DIM:2 * HEAD_DIM])
    for pg in range(n_pages):
        acc = acc + _dot_nt(ps[pg].astype(bf16), nsa_pages[pg][0, 0, 3 * HEAD_DIM:4 * HEAD_DIM, :].astype(bf16))
    o_s = acc / tot_p

    yield
    win_old = win_ref[0, s]
    win_new = ckv[:, 4 * HEAD_DIM:6 * HEAD_DIM]
    new_t = pltpu.roll(jnp.transpose(_pad_to(win_new, LANES)), LANES - tn, 1)
    new_wide = jnp.concatenate([jnp.zeros((2 * HEAD_DIM, WINDOW - LANES), f32), new_t], axis=1)
    w_idx = lax.broadcasted_iota(jnp.int32, (1, WINDOW), 1)
    wout_ref[s] = jnp.where(w_idx >= WINDOW - tn, new_wide, pltpu.roll(win_old, WINDOW - tn, 1))
    wob = win_old.astype(bf16)
    wnb = _pad_to(win_new, LANES).astype(bf16)
    s_old = jnp.where(w_idx > qpos_s - past, _dot(qs, wob[0:HEAD_DIM, :]), NEG)
    s_new = jnp.where(new_ok, _dot_nt(qs, wnb[:, 0:HEAD_DIM]), NEG)
    ps, tot_p = _softmax_tiles([s_old, s_new])
    o_w = (_dot_nt(ps[0].astype(bf16), wob[HEAD_DIM:2 * HEAD_DIM, :])
           + _dot(ps[1].astype(bf16), wnb[:, HEAD_DIM:2 * HEAD_DIM])) / tot_p

    o = _gate_col(misc, 0) * o_c + _gate_col(misc, 1) * o_s + _gate_col(misc, 2) * o_w
    y_c = _unstack_heads(o, tn)
    y_ref[s] = jnp.concatenate([y_a, y_b, y_c, y_d], axis=-1)


def _sample_mixer(h3, fox_t, lf_t, nsa_t, win_t, conv4, page_table, layer, consts, n_seq=2):
    n_all, tn, pw = h3.shape
    assert n_all % n_seq == 0
    bsz = n_all // n_seq
    n_pages = page_table.shape[1]
    page = fox_t.shape[3]
    past = n_pages * page
    n_sel_pad = -(-(-(-(past + tn) // SEL_LEN)) // 8) * 8

    def page_spec(arr, s, pg):
        blk = (1, 1) + arr.shape[2:]
        return pl.BlockSpec(blk, lambda b, pt: (layer, pt[n_seq * b + s, pg], 0, 0))

    in_specs = [pl.BlockSpec((n_seq, tn, pw), lambda b, pt: (b, 0, 0))]
    operands = [h3]
    for arr in (fox_t, lf_t, nsa_t):
        for s in range(n_seq):
            for pg in range(n_pages):
                in_specs.append(page_spec(arr, s, pg))
                operands.append(arr)
    in_specs.append(pl.BlockSpec((1, n_seq) + win_t.shape[2:], lambda b, pt: (layer, b, 0, 0)))
    operands.append(win_t)
    in_specs.append(pl.BlockSpec((1, n_seq) + conv4.shape[2:], lambda b, pt: (layer, b, 0, 0)))
    operands.append(conv4)
    for a in consts:
        in_specs.append(pl.BlockSpec(a.shape, lambda b, pt, nd=a.ndim: (0,) * nd))
        operands.append(a)
    out_dims = [(tn, 4 * GROUP_W), (tn, 2 * GROUP_W), (tn, LANES), (tn, 4 * HEAD_DIM), (2 * HEAD_DIM, WINDOW),
                (tn, GROUP_W), (tn, GROUP_W)]
    out_shape = [jax.ShapeDtypeStruct((n_all, r, w), f32) for r, w in out_dims]
    out_specs = [pl.BlockSpec((n_seq, r, w), lambda b, pt: (b, 0, 0)) for r, w in out_dims]
    return pl.pallas_call(
        functools.partial(_sample_mixer_kernel, n_pages=n_pages, page=page, past=past, n_sel_pad=n_sel_pad,
                          n_seq=n_seq),
        grid_spec=pltpu.PrefetchScalarGridSpec(num_scalar_prefetch=1, grid=(bsz,), in_specs=in_specs,
                                               out_specs=out_specs,
                                               scratch_shapes=[pltpu.VMEM((n_seq, past, 2 * HEAD_DIM), f32)]),
        out_shape=out_shape,
        compiler_params=_cparams(("arbitrary",)),
    )(page_table, *operands)


def _outproj_kernel(x_ref, *refs, alpha):
    y_refs = refs[:-4]
    w_ref, g_ref, b_ref, o_ref = refs[-4:]
    acc = None
    c0 = 0
    for y_ref in y_refs:
        wy = y_ref.shape[1]
        part = _dot(y_ref[...].astype(bf16), w_ref[c0:c0 + wy, :])
        acc = part if acc is None else acc + part
        c0 += wy
    o_ref[...] = _layer_norm_rows(alpha * x_ref[...] + acc, g_ref[0:1, :], b_ref[0:1, :])


def _outproj(x2, ys, w_out_b, g8, b8, alpha, tm):
    n, d = x2.shape
    assert sum(y.shape[1] for y in ys) == d
    const = lambda a: pl.BlockSpec(a.shape, lambda i: (0,) * a.ndim)
    return pl.pallas_call(
        functools.partial(_outproj_kernel, alpha=alpha),
        grid=(n // tm,),
        in_specs=[pl.BlockSpec((tm, d), lambda i: (i, 0))] + [pl.BlockSpec((tm, y.shape[1]), lambda i: (i, 0)) for y in ys]
        + [const(w_out_b), const(g8), const(b8)],
        out_specs=pl.BlockSpec((tm, d), lambda i: (i, 0)),
        out_shape=jax.ShapeDtypeStruct((n, d), f32),
        compiler_params=_cparams(("arbitrary",)),
    )(x2, *ys, w_out_b, g8, b8)


def _route(xb, wr_t, tm):
    logits = _dot_nt(wr_t, xb)
    e = jnp.exp(logits - jnp.max(logits, axis=0, keepdims=True))
    probs = e / jnp.sum(e, axis=0, keepdims=True)
    p = [probs[i:i + 1, :] for i in range(N_EXPERTS)]
    one = jnp.ones((1, tm), f32)
    zero = jnp.zeros((1, tm), f32)
    in_top = []
    gscore = []
    for g in range(N_GROUPS):
        members = range(g * EXP_PER_GROUP, (g + 1) * EXP_PER_GROUP)
        score = zero
        for i in members:
            rank = zero
            for j in members:
                if j < i:
                    rank = rank + jnp.where(p[j] >= p[i], one, zero)
                elif j > i:
                    rank = rank + jnp.where(p[j] > p[i], one, zero)
            keep = jnp.where(rank < TOP_K, one, zero)
            in_top.append(keep)
            score = score + keep * p[i]
        gscore.append(score)
    gate_rows = []
    wsum = zero
    chosen = []
    for g in range(N_GROUPS):
        win = one
        for g2 in range(N_GROUPS):
            if g2 < g:
                win = win * jnp.where(gscore[g] > gscore[g2], one, zero)
            elif g2 > g:
                win = win * jnp.where(gscore[g] >= gscore[g2], one, zero)
        for i in range(g * EXP_PER_GROUP, (g + 1) * EXP_PER_GROUP):
            c = in_top[i] * win
            chosen.append(c)
            wsum = wsum + c * p[i]
    for i in range(N_EXPERTS):
        gate_rows.append(jnp.where(chosen[i] > 0.5, p[i] / wsum, zero))
    gate_t = jnp.concatenate(gate_rows + [jnp.zeros((LANES - N_EXPERTS, tm), f32)], axis=0)
    return jnp.concatenate([jnp.transpose(gate_t[:, c * LANES:(c + 1) * LANES]) for c in range(tm // LANES)], axis=0)


def _moe_kernel(x_ref, wr_ref, wg_ref, wu_ref, wd_ref, g_ref, b_ref, o_ref, gate_sc, acc_sc, xb_sc, *, alpha, tm):
    grp = pl.program_id(1)
    n_e = wg_ref.shape[0]

    @pl.when(grp == 0)
    def _():
        xb = x_ref[...].astype(bf16)
        xb_sc[...] = xb
        gate_sc[...] = _route(xb, wr_ref[...], tm)

    xb = xb_sc[...]
    lane = lax.broadcasted_iota(jnp.int32, (tm, LANES), 1)
    gate = gate_sc[...]
    gates = [_dot(xb, wg_ref[j]) for j in range(n_e)]
    ups = [_dot(xb, wu_ref[j]) for j in range(n_e)]
    hs = []
    for j in range(n_e):
        gcol = jnp.sum(jnp.where(lane == grp * n_e + j, gate, 0.0), axis=-1, keepdims=True)
        hs.append((jax.nn.silu(gates[j]) * ups[j] * gcol).astype(bf16))
    y = _dot(jnp.concatenate(hs, axis=-1), wd_ref[...].reshape(n_e * wd_ref.shape[1], wd_ref.shape[2]))

    @pl.when(grp == 0)
    def _():
        acc_sc[...] = y

    @pl.when(grp > 0)
    def _():
        acc_sc[...] += y

    @pl.when(grp == pl.num_programs(1) - 1)
    def _():
        o_ref[...] = _layer_norm_rows(alpha * x_ref[...] + acc_sc[...], g_ref[0:1, :], b_ref[0:1, :])


def _moe(x2, wr_t, wg, wu, wd, g8, b8, alpha, tm):
    n, d = x2.shape
    de = wg.shape[-1]
    es = EXP_PER_GROUP
    const = lambda a: pl.BlockSpec(a.shape, lambda i, e: (0,) * a.ndim)
    return pl.pallas_call(
        functools.partial(_moe_kernel, alpha=alpha, tm=tm),
        grid=(n // tm, N_EXPERTS // es),
        in_specs=[
            pl.BlockSpec((tm, d), lambda i, e: (i, 0)), const(wr_t),
            pl.BlockSpec((es, d, de), lambda i, e: (e, 0, 0)), pl.BlockSpec((es, d, de), lambda i, e: (e, 0, 0)),
            pl.BlockSpec((es, de, d), lambda i, e: (e, 0, 0)), const(g8), const(b8),
        ],
        out_specs=pl.BlockSpec((tm, d), lambda i, e: (i, 0)),
        out_shape=jax.ShapeDtypeStruct((n, d), f32),
        scratch_shapes=[pltpu.VMEM((tm, LANES), f32), pltpu.VMEM((tm, d), f32), pltpu.VMEM((tm, d), bf16)],
        compiler_params=_cparams(("arbitrary", "arbitrary")),
    )(x2, wr_t, wg, wu, wd, g8, b8)


def _pad_rows(a, rows=8):
    return jnp.concatenate([a, jnp.zeros((rows - a.shape[0],) + a.shape[1:], a.dtype)], axis=0)


def _prep_layer(l, w_in, fox_b_f, gmlp_w_s, gmlp_b_s, nsa_cmp_pe, nsa_cmp_w1, nsa_cmp_w2, conv_w, w_out, ln_g, ln_b,
                w_exp_gate, w_exp_up, w_exp_down):
    w = w_in[l]
    d = w.shape[0]
    w_pack = jnp.concatenate([w[:, 0:1280], w[:, 1284:1924], w[:, 1936:2704], w[:, 1280:1284], w[:, 1924:1936],
                              jnp.zeros((d, LANES - N_FOX_F - N_GATES), w.dtype)], axis=1).astype(bf16)
    fb_row = _pad_rows(jnp.concatenate([fox_b_f[l], jnp.zeros((LANES - N_FOX_F,), f32)])[None, :])
    tril = jnp.asarray(np.tril(np.ones((CHUNK, CHUNK), np.float32)))
    wst = (gmlp_w_s[l] * tril).reshape(GROUP_HEADS * CHUNK, CHUNK).astype(bf16)
    bs_full = jnp.repeat(gmlp_b_s[l].T, HEAD_DIM, axis=1)
    cw8 = _pad_rows(conv_w[l])
    half = CMP_LEN // 2
    w1r = nsa_cmp_w1[l].reshape(2, CMP_LEN, HEAD_DIM, CMP_HID)
    per = nsa_cmp_pe[l]
    w2c = jnp.concatenate([nsa_cmp_w2[l, 0], nsa_cmp_w2[l, 1]], axis=1).astype(bf16)
    zed = jnp.zeros((CMP_LEN, HEAD_DIM, CMP_HID), f32)
    w1r = jnp.concatenate([jnp.concatenate([w1r[0], zed], axis=2), jnp.concatenate([zed, w1r[1]], axis=2)],
                          axis=1).astype(bf16)
    pe32 = jnp.concatenate([per[0], per[1]], axis=1)
    w1g = w1r.reshape(2, half * 2 * HEAD_DIM, 2 * CMP_HID)
    pe2 = _pad_rows(pe32.reshape(2, half * 2 * HEAD_DIM))
    w_t = jnp.concatenate([w_pack[:, BK:BK + 2 * GROUP_W], w_pack[:, CKV:CKV + 6 * HEAD_DIM]], axis=1).T
    return dict(
        w_pack=w_pack, w_t=w_t, fb_row=fb_row, wst=wst, bs_full=bs_full, cw8=cw8, w1g=w1g, pe2=pe2, w2c=w2c, w1r=w1r,
        pe32=pe32,
        w_out=w_out[l].astype(bf16), g1=_pad_rows(ln_g[l, 0:1]), b1=_pad_rows(ln_b[l, 0:1]),
        g2=_pad_rows(ln_g[l, 1:2]), b2=_pad_rows(ln_b[l, 1:2]),
        wg=w_exp_gate[l].astype(bf16), wu=w_exp_up[l].astype(bf16), wd=w_exp_down[l].astype(bf16),
    )


def _sel_overlap(t, n_sel):
    n_cmp = (t - CMP_LEN) // CMP_STRIDE + 1
    start = np.arange(n_cmp) * CMP_STRIDE
    sstart = np.arange(n_sel) * SEL_LEN
    ov = np.minimum(start[:, None] + CMP_LEN, sstart[None, :] + SEL_LEN) - np.maximum(start[:, None], sstart[None, :])
    return np.clip(ov, 0, None) / CMP_STRIDE


def _head_avg():
    a = np.kron(np.eye(GROUP_HEADS, dtype=np.float32), np.full((HEAD_DIM, HEAD_DIM), 1.0 / HEAD_DIM, np.float32))
    return jnp.asarray(a, bf16)


def _post_block(x2, ys, p, wr_t, alpha, tm, tm_moe=1024):
    x1 = _outproj(x2, ys, p['w_out'], p['g1'], p['b1'], alpha, tm)
    assert x1.shape[0] % tm_moe == 0
    return _moe(x1, wr_t, p['wg'], p['wu'], p['wd'], p['g2'], p['b2'], alpha, tm_moe)


def _prompt_layer(x, p, wr_t, alpha, tm=512, tk=512, bq=512, bq_fox=512):
    bsz, t, d = x.shape
    n = bsz * t
    n_sel = t // SEL_LEN
    n_grp = t // CMP_STRIDE
    assert t % tm == 0 and t % tk == 0 and n_sel <= LANES and t >= WINDOW + bq
    tri = jnp.asarray(np.triu(np.ones((tm, tm), np.float32)), bf16)
    expand_t = np.zeros((t, LANES), np.float32)
    expand_t[np.arange(t), np.arange(t) // SEL_LEN] = 1.0
    (ya, fq, fk, fvt, fkvt, misc, ccol, nq, nkvt, wint, kse, featt, kcvc, yd, cst) = _prompt_inproj(
        x, jnp.asarray(expand_t, bf16), p['w_pack'], p['w_t'], p['fb_row'], p['wst'], p['bs_full'], p['cw8'],
        _head_avg(), tri, tm)
    yb = _fox_prompt(fq, fk, fvt, ccol, tk, bq_fox)
    cmp, cmp_t = _compress(kcvc.reshape(bsz, n_grp, CMP_STRIDE * 2 * HEAD_DIM), p['pe2'], p['w1g'], p['w2c'])
    ov = _sel_overlap(t, n_sel)
    selw_t = np.zeros((LANES, n_grp), np.float32)
    selw_t[:n_sel, :ov.shape[0]] = ov.T
    yc = _nsa_prompt(nq, kse, featt, cmp, cmp_t, misc, jnp.asarray(selw_t, bf16), tk, bq)
    ys = [a.reshape(n, GROUP_W) for a in (ya, yb, yc, yd)]
    x_out = _post_block(x.reshape(n, d), ys, p, wr_t, alpha, tm).reshape(bsz, t, d)
    win_keep = min(WINDOW, t)
    states = (fkvt.reshape(bsz, 2, GROUP_HEADS, HEAD_DIM, t).transpose(0, 4, 1, 2, 3), misc[:, :, 0:N_FOX_F],
              nkvt.reshape(bsz, 4, HEAD_DIM, t).transpose(0, 3, 1, 2),
              wint[:, :, t - win_keep:].reshape(bsz, 2, HEAD_DIM, win_keep).transpose(0, 3, 1, 2),
              cst[:, 8 - (CONV_W - 1):])
    return x_out, states, (ya, yb, yc, yd)


def _sample_consts(p, n_pages, page, tn):
    utri = np.triu(np.ones((page, page), np.float32))
    ones = np.ones((page, page), np.float32)
    r = np.arange(GROUP_HEADS * n_pages)
    slt = np.zeros((GROUP_HEADS * n_pages, LANES), np.float32)
    slt[:, :r.size] = ((r[:, None] % GROUP_HEADS) == (r[None, :] % GROUP_HEADS)) & (r[None, :] // GROUP_HEADS < r[:, None] // GROUP_HEADS)
    tri8 = np.zeros((tn, LANES), np.float32)
    tri8[:, :tn] = np.tril(np.ones((tn, tn), np.float32))
    t_all = n_pages * page + tn
    n_sel = -(-t_all // SEL_LEN)
    ov = _sel_overlap(t_all, n_sel)
    n_grp = n_pages * page // CMP_STRIDE
    assert ov.shape[0] <= n_grp and n_sel <= LANES
    selw = np.zeros((n_grp, LANES), np.float32)
    selw[:ov.shape[0], :n_sel] = ov
    n_keys = (n_pages + 1) * page
    expand = np.zeros((LANES, n_keys), np.float32)
    expand[np.arange(n_keys) // SEL_LEN, np.arange(n_keys)] = 1.0
    as_b = lambda a: jnp.asarray(np.asarray(a, np.float32), bf16)
    return [p['fb_row'], p['wst'], p['bs_full'], p['cw8'], _head_avg(), as_b(utri), as_b(ones), as_b(slt),
            as_b(tri8), p['pe32'], p['w1g'], p['w2c'], as_b(selw), as_b(expand)]


def _sample_layer(x, views, page_table, layer, p, wr_t, alpha, tm=512):
    bsz, tn, d = x.shape
    n = bsz * tn
    fox_t, lf_t, nsa_t, win_t, conv4 = views
    n_pages = page_table.shape[1]
    page = fox_t.shape[3]
    assert tn == 8 and page == LANES and win_t.shape[3] == WINDOW and n_pages * page >= WINDOW
    assert GROUP_HEADS * n_pages <= LANES
    h = _sample_inproj(x.reshape(n, d), p['w_pack'], tm).reshape(bsz, tn, P_PACK)
    y, fkv, misc, nrow, wout, cout, gv = _sample_mixer(h, fox_t, lf_t, nsa_t, win_t, conv4, page_table, layer,
                                                       _sample_consts(p, n_pages, page, tn))
    x_out = _post_block(x.reshape(n, d), [y.reshape(n, d)], p, wr_t, alpha, tm).reshape(bsz, tn, d)
    win_state = wout.reshape(bsz, 2, HEAD_DIM, WINDOW).transpose(0, 3, 1, 2)
    states = (fkv.reshape(bsz, tn, 2, GROUP_HEADS, HEAD_DIM), misc[:, :, 0:N_FOX_F], nrow.reshape(bsz, tn, 4, HEAD_DIM),
              win_state, cout[:, tn - (CONV_W - 1):], gv.reshape(bsz, tn, GROUP_HEADS, HEAD_DIM))
    return x_out, states


def kernel(x_prompt, x_sample, cache_fox_kv, cache_fox_logf, cache_nsa_kv, state_nsa_win, state_conv, page_table,
           w_in, fox_b_f, gmlp_w_s, gmlp_b_s, nsa_cmp_pe, nsa_cmp_w1, nsa_cmp_w2, conv_w, w_out, ln_g, ln_b,
           w_router, w_exp_gate, w_exp_up, w_exp_down):
    depth, n_pool, page = cache_fox_kv.shape[0:3]
    dec_b = x_sample.shape[0]
    alpha = (2 * depth) ** 0.25
    wr_t = w_router.T.astype(bf16)
    views = (cache_fox_kv.transpose(0, 1, 3, 4, 5, 2).reshape(depth, n_pool, 2 * GROUP_W, page),
             cache_fox_logf.transpose(0, 1, 3, 2),
             cache_nsa_kv.transpose(0, 1, 3, 4, 2).reshape(depth, n_pool, 4 * HEAD_DIM, page),
             state_nsa_win.transpose(0, 1, 3, 4, 2).reshape(depth, dec_b, 2 * HEAD_DIM, state_nsa_win.shape[2]),
             state_conv)
    xp, xs = x_prompt, x_sample
    st_p = [[] for _ in range(5)]
    st_s = [[] for _ in range(6)]
    for l in range(depth):
        p = _prep_layer(l, w_in, fox_b_f, gmlp_w_s, gmlp_b_s, nsa_cmp_pe, nsa_cmp_w1, nsa_cmp_w2, conv_w, w_out,
                        ln_g, ln_b, w_exp_gate, w_exp_up, w_exp_down)
        xp, sp, _ = _prompt_layer(xp, p, wr_t, alpha)
        xs, ss = _sample_layer(xs, views, page_table, l, p, wr_t, alpha)
        for lst, a in zip(st_p, sp):
            lst.append(a)
        for lst, a in zip(st_s, ss):
            lst.append(a)
    fox_kv_p, fox_lf_p, nsa_kv_p, nsa_win_p, conv_p = [jnp.stack(a, axis=0) for a in st_p]
    fox_kv_s, fox_lf_s, nsa_kv_s, nsa_win_s, conv_s, gmlp_v_s = [jnp.stack(a, axis=0) for a in st_s]
    return (xp, xs, fox_kv_p, fox_kv_s, fox_lf_p, fox_lf_s, nsa_kv_p, nsa_kv_s, nsa_win_p, nsa_win_s, conv_p, conv_s,
            gmlp_v_s)
```

```python
import functools

import numpy as np
import jax
import jax.numpy as jnp
from jax import lax
from jax.experimental import pallas as pl
from jax.experimental.pallas import tpu as pltpu

f32 = jnp.float32
bf16 = jnp.bfloat16

HEAD_DIM = 64
GROUP_W = 256
GROUP_HEADS = 4
CHUNK = 128
CMP_LEN = 32
CMP_STRIDE = 16
CMP_HID = 128
SEL_LEN = 64
SEL_TOPK = 16
WINDOW = 512
CONV_W = 3
N_EXPERTS = 16
N_GROUPS = 4
EXP_PER_GROUP = 4
TOP_K = 2
LN_EPS = 1e-5
SCALE = HEAD_DIM ** -0.5
NEG = -1e30
BIG = 1e30
LANES = 128

AU, AV, BQ, BK, BV, CQ, CKV, DH, DB, DC, MISC = 0, 256, 512, 768, 1024, 1280, 1536, 1920, 2176, 2432, 2688
P_PACK = 2816
N_FOX_F = GROUP_HEADS
N_GATES = 3 * GROUP_HEADS

V7X_VMEM_BYTES = 64 * 1024 * 1024
VMEM_LIMIT = V7X_VMEM_BYTES * 7 // 8


def _cparams(sem):
    return pltpu.CompilerParams(dimension_semantics=sem, vmem_limit_bytes=VMEM_LIMIT)


def _dot(a, b):
    return jnp.dot(a, b, preferred_element_type=f32)


def _dot_nt(a, b):
    return lax.dot_general(a, b, (((1,), (1,)), ((), ())), preferred_element_type=f32)


def _split3(x):
    hi = x.astype(bf16)
    r = x - hi.astype(f32)
    mid = r.astype(bf16)
    lo = (r - mid.astype(f32)).astype(bf16)
    return hi, mid, lo


def _dot_exact_rhs(a, b3):
    return _dot(a, b3[0]) + _dot(a, b3[1]) + _dot(a, b3[2])


def _dot_exact_lhs(a3, b):
    return _dot(a3[0], b) + _dot(a3[1], b) + _dot(a3[2], b)


def _log_sigmoid(x):
    return jnp.minimum(x, 0.0) - jnp.log1p(jnp.exp(-jnp.abs(x)))


def _gelu(x):
    return jax.nn.gelu(x, approximate=True)


def _layer_norm_rows(x, g, b):
    mu = jnp.mean(x, axis=-1, keepdims=True)
    d = x - mu
    var = jnp.mean(d * d, axis=-1, keepdims=True)
    return d * lax.rsqrt(var + LN_EPS) * g + b


def _head_layer_norm(g, avg):
    h2 = g.astype(bf16)
    mu = _dot(h2, avg) + _dot((g - h2.astype(f32)).astype(bf16), avg)
    d = g - mu
    dd = d * d
    d2 = dd.astype(bf16)
    var = _dot(d2, avg) + _dot((dd - d2.astype(f32)).astype(bf16), avg)
    return d * lax.rsqrt(var + LN_EPS)


def _head_diag(z_all, rows):
    lane_head = lax.broadcasted_iota(jnp.int32, (rows, GROUP_W), 1) // HEAD_DIM
    out = jnp.where(lane_head == 0, z_all[0:rows], 0.0)
    for h in range(1, GROUP_HEADS):
        out = out + jnp.where(lane_head == h, z_all[h * rows:(h + 1) * rows], 0.0)
    return out


def _head_blockdiag_q(q, rows):
    lane_head = lax.broadcasted_iota(jnp.int32, (rows, GROUP_W), 1) // HEAD_DIM
    return jnp.concatenate([jnp.where(lane_head == h, q, jnp.zeros_like(q)) for h in range(GROUP_HEADS)], axis=0)


def _prompt_inproj_kernel(x_ref, exp_ref, w_ref, wt_ref, fb_ref, wst_ref, bs_ref, cw_ref, avg_ref, tri_ref, *rest,
                          tm, n_prev):
    prev = rest[:2] if n_prev else ()
    (ya_ref, fq_ref, fk_ref, fvt_ref, fkvt_ref, misc_ref, ccol_ref,
     nq_ref, nkvt_ref, wint_ref, kse_ref, featt_ref, kcvc_ref, yd_ref, cst_ref,
     carry_c, carry_z) = rest[len(prev):]
    if n_prev:
        fkvt_ref[0:n_prev] = prev[0][...]
        nkvt_ref[0:n_prev] = prev[1][...]
    ti = pl.program_id(1)

    @pl.when(ti == 0)
    def _():
        carry_c[...] = jnp.zeros_like(carry_c)
        carry_z[...] = jnp.zeros_like(carry_z)

    xb = x_ref[0].astype(bf16)

    u = _gelu(_dot(xb, w_ref[:, AU:AU + GROUP_W]))
    vn = _head_layer_norm(_gelu(_dot(xb, w_ref[:, AV:AV + GROUP_W])), avg_ref[...])
    wst = wst_ref[...]
    bsf = bs_ref[...]
    for c in range(tm // CHUNK):
        r0 = c * CHUNK
        z_all = _dot(wst, vn[r0:r0 + CHUNK].astype(bf16))
        ya_ref[0, r0:r0 + CHUNK, :] = (u[r0:r0 + CHUNK] * (_head_diag(z_all, CHUNK) + bsf)).astype(bf16)

    fq_ref[0] = (_dot(xb, w_ref[:, BQ:BQ + GROUP_W]) * SCALE).astype(bf16)
    res_t = _dot_nt(wt_ref[...], xb)
    fkvt_ref[n_prev, 0] = res_t[0:2 * GROUP_W]
    fvt_ref[0] = res_t[GROUP_W:2 * GROUP_W].astype(bf16)
    nkvt_ref[n_prev, 0] = res_t[2 * GROUP_W:3 * GROUP_W]
    wint_ref[0] = res_t[3 * GROUP_W:3 * GROUP_W + 2 * HEAD_DIM]
    featt_ref[0] = jnp.concatenate([res_t[3 * GROUP_W - HEAD_DIM:3 * GROUP_W],
                                    res_t[3 * GROUP_W + HEAD_DIM:3 * GROUP_W + 2 * HEAD_DIM]], axis=0).astype(bf16)
    pre =_dot(xb, w_ref[:, MISC:MISC + LANES]) + fb_ref[0:1, :]
    lane = lax.broadcasted_iota(jnp.int32, (tm, LANES), 1)
    lf = jnp.where(lane < N_FOX_F, _log_sigmoid(pre), 0.0)
    misc_ref[0] = jnp.where(lane < N_FOX_F, lf, jnp.where(lane < N_FOX_F + N_GATES, jax.nn.sigmoid(pre), 0.0))
    c_t = _dot_exact_lhs(_split3(jnp.transpose(lf)[0:8, :]), tri_ref[...]) + carry_c[:, 0:1]
    carry_c[...] = jnp.broadcast_to(c_t[:, tm - 1:tm], (8, LANES))
    ccol_ref[0] = jnp.transpose(jnp.concatenate([c_t, jnp.zeros((LANES - 8, tm), f32)], axis=0))
    c3 = [piece.astype(f32) for piece in _split3(c_t)]
    k_tok = jnp.transpose(res_t[0:GROUP_W])
    parts = []
    for g in range(GROUP_HEADS // 2):
        rows = [-c3[i][h:h + 1, :] for h in (2 * g, 2 * g + 1) for i in range(3)]
        aux_t = jnp.concatenate(rows + [jnp.ones((3, tm), f32), jnp.zeros((LANES - 9, tm), f32)], axis=0)
        parts += [k_tok[:, g * LANES:(g + 1) * LANES].astype(bf16), jnp.transpose(aux_t).astype(bf16)]
    fk_ref[0] = jnp.concatenate(parts, axis=-1)

    nq_ref[0] = (_dot(xb, w_ref[:, CQ:CQ + GROUP_W]) * SCALE).astype(bf16)
    nsa0 = 2 * GROUP_W
    kcvc_ref[0] = jnp.transpose(res_t[nsa0:nsa0 + 2 * HEAD_DIM])
    ks_kw = jnp.transpose(jnp.concatenate([res_t[nsa0 + 2 * HEAD_DIM:nsa0 + 3 * HEAD_DIM],
                                           res_t[nsa0 + 4 * HEAD_DIM:nsa0 + 5 * HEAD_DIM]], axis=0))
    kse_ref[0] = jnp.concatenate([ks_kw.astype(bf16), exp_ref[...]], axis=-1)

    z = _dot(xb, w_ref[:, DC:DC + GROUP_W]) * _dot(xb, w_ref[:, DH:DH + GROUP_W])
    row = lax.broadcasted_iota(jnp.int32, (tm, GROUP_W), 0)
    zp = carry_z[...]
    z1 = jnp.where(row == 0, zp[7:8, :], pltpu.roll(z, 1, 0))
    z2 = jnp.where(row == 0, zp[6:7, :], jnp.where(row == 1, zp[7:8, :], pltpu.roll(z, 2, 0)))
    cw = cw_ref[...]
    y_conv = cw[0:1, :] * z2 + cw[1:2, :] * z1 + cw[2:3, :] * z
    yd_ref[0] = (_dot(xb, w_ref[:, DB:DB + GROUP_W]) * y_conv).astype(bf16)
    carry_z[...] = z[tm - 8:tm, :]
    cst_ref[0] = z[tm - 8:tm, :]


def _prompt_inproj(x, expand_t, w_pack, w_t, fb_row, wst, bs_full, cw8, avg, tri, tm, prev):
    bsz, t, d = x.shape
    nt = t // tm
    n_prev = prev[0].shape[0] if prev else 0
    tok = lambda w, dt: (jax.ShapeDtypeStruct((bsz, t, w), dt), pl.BlockSpec((1, tm, w), lambda b, i: (b, i, 0)))
    feat = lambda r, dt: (jax.ShapeDtypeStruct((bsz, r, t), dt), pl.BlockSpec((1, r, tm), lambda b, i: (b, 0, i)))
    stack_spec = lambda n, r: pl.BlockSpec((n, 1, r, tm), lambda b, i: (0, b, 0, i))
    stacked = lambda r: (jax.ShapeDtypeStruct((n_prev + 1, bsz, r, t), f32), stack_spec(n_prev + 1, r))
    const = lambda a: pl.BlockSpec(a.shape, lambda b, i: (0,) * a.ndim)
    outs = [
        tok(GROUP_W, bf16),
        tok(GROUP_W, bf16),
        tok(2 * GROUP_W, bf16),
        feat(GROUP_W, bf16),
        stacked(2 * GROUP_W),
        tok(LANES, f32),
        tok(LANES, f32),
        tok(GROUP_W, bf16),
        stacked(4 * HEAD_DIM),
        feat(2 * HEAD_DIM, f32),
        tok(GROUP_W, bf16),
        feat(2 * HEAD_DIM, bf16),
        tok(2 * HEAD_DIM, f32),
        tok(GROUP_W, bf16),
        (jax.ShapeDtypeStruct((bsz, 8, GROUP_W), f32), pl.BlockSpec((1, 8, GROUP_W), lambda b, i: (b, 0, 0))),
    ]
    consts = [w_pack, w_t, fb_row, wst, bs_full, cw8, avg, tri]
    prev = list(prev) if prev else []
    return pl.pallas_call(
        functools.partial(_prompt_inproj_kernel, tm=tm, n_prev=n_prev),
        grid=(bsz, nt),
        in_specs=[pl.BlockSpec((1, tm, d), lambda b, i: (b, i, 0)), pl.BlockSpec((tm, LANES), lambda b, i: (i, 0))]
        + [const(a) for a in consts] + [stack_spec(n_prev, a.shape[2]) for a in prev],
        out_specs=[o[1] for o in outs],
        out_shape=[o[0] for o in outs],
        scratch_shapes=[pltpu.VMEM((8, LANES), f32), pltpu.VMEM((8, GROUP_W), f32)],
        compiler_params=_cparams(("arbitrary", "arbitrary")),
    )(x, expand_t, *consts, *prev)


def _flash_update_t(chains):
    parts = []
    for s, v_t, m_sc, l_sc, acc_sc in chains:
        m_old = m_sc[...]
        m_new = jnp.maximum(m_old, jnp.max(s, axis=0, keepdims=True))
        alpha = jnp.exp(m_old - m_new)
        p = jnp.exp(s - m_new)
        l_sc[...] = alpha * l_sc[...] + jnp.sum(p, axis=0, keepdims=True)
        m_sc[...] = m_new
        parts.append((alpha, p.astype(bf16)))
    for (s, v_t, m_sc, l_sc, acc_sc), (alpha, pb) in zip(chains, parts):
        acc_sc[...] = alpha * acc_sc[...] + _dot(v_t, pb)


def _fox_prompt_kernel(q_ref, k_ref, vt_ref, ccol_ref, o_ref, m_sc, l_sc, acc_sc, *, tk, bq):
    qi = pl.program_id(1)
    n_pair = GROUP_HEADS // 2
    pw = 2 * HEAD_DIM
    nq = LANES
    n_sub = bq // nq
    lane = lax.broadcasted_iota(jnp.int32, (nq, pw), 1)
    lane_hi = lane >= HEAD_DIM
    qbd, qpos = [], []
    for u in range(n_sub):
        q = q_ref[0, u * nq:(u + 1) * nq, :]
        cc = ccol_ref[0, u * nq:(u + 1) * nq, :]
        qp = qi * bq + u * nq + lax.broadcasted_iota(jnp.int32, (1, nq), 1)
        qpos.append(jnp.concatenate([qp, qp], axis=1))
        for g in range(n_pair):
            qg = q[:, g * pw:(g + 1) * pw]
            halves = []
            for i, h in enumerate((2 * g, 2 * g + 1)):
                hi, mid, lo = [piece.astype(f32) for piece in _split3(cc[:, h:h + 1])]
                aux = jnp.where((lane >= 3 * i) & (lane < 3 * i + 3), 1.0, 0.0)
                aux = jnp.where(lane == 6, hi, jnp.where(lane == 7, mid, jnp.where(lane == 8, lo, aux)))
                own = jnp.where(lane_hi, qg, jnp.zeros_like(qg)) if i else jnp.where(lane_hi, jnp.zeros_like(qg), qg)
                halves.append(jnp.concatenate([own, aux.astype(bf16)], axis=1))
            qbd.append(jnp.concatenate(halves, axis=0))
    m_sc[...] = jnp.full_like(m_sc, NEG)
    l_sc[...] = jnp.zeros_like(l_sc)
    acc_sc[...] = jnp.zeros_like(acc_sc)

    def tile(j, masked):
        k0 = pl.multiple_of(j * tk, tk)
        chains = []
        for g in range(n_pair):
            k = k_ref[0, pl.ds(k0, tk), 2 * g * pw:2 * (g + 1) * pw]
            v_t = vt_ref[0, g * pw:(g + 1) * pw, pl.ds(k0, tk)]
            for u in range(n_sub):
                c = u * n_pair + g
                s = _dot_nt(k, qbd[c])
                if masked:
                    kpos = k0 + lax.broadcasted_iota(jnp.int32, (tk, 1), 0)
                    s = jnp.where(kpos <= qpos[u], s, NEG)
                chains.append((s, v_t, m_sc.at[c], l_sc.at[c], acc_sc.at[c]))
        _flash_update_t(chains)

    n_full = (qi * bq) // tk

    def body(j, carry):
        tile(j, False)
        return carry

    lax.fori_loop(0, n_full, body, 0)
    tile(n_full, True)
    for u in range(n_sub):
        outs = []
        for g in range(n_pair):
            c = u * n_pair + g
            o = acc_sc[c] / l_sc[c]
            outs.append(o[0:HEAD_DIM, 0:nq])
            outs.append(o[HEAD_DIM:pw, nq:2 * nq])
        o_ref[0, u * nq:(u + 1) * nq, :] = jnp.transpose(jnp.concatenate(outs, axis=0)).astype(bf16)


def _fox_prompt(fq, fk, fvt, ccol, tk, bq):
    bsz, t, _ = fq.shape
    assert tk % bq == 0 and t % tk == 0 and bq % LANES == 0
    n_chain = (bq // LANES) * (GROUP_HEADS // 2)
    return pl.pallas_call(
        functools.partial(_fox_prompt_kernel, tk=tk, bq=bq),
        grid=(bsz, t // bq),
        in_specs=[
            pl.BlockSpec((1, bq, GROUP_W), lambda b, i: (b, i, 0)),
            pl.BlockSpec((1, t, 2 * GROUP_W), lambda b, i: (b, 0, 0)),
            pl.BlockSpec((1, GROUP_W, t), lambda b, i: (b, 0, 0)),
            pl.BlockSpec((1, bq, LANES), lambda b, i: (b, i, 0)),
        ],
        out_specs=pl.BlockSpec((1, bq, GROUP_W), lambda b, i: (b, i, 0)),
        out_shape=jax.ShapeDtypeStruct((bsz, t, GROUP_W), bf16),
        scratch_shapes=[pltpu.VMEM((n_chain, 1, 2 * LANES), f32), pltpu.VMEM((n_chain, 1, 2 * LANES), f32),
                        pltpu.VMEM((n_chain, 2 * HEAD_DIM, 2 * LANES), f32)],
        compiler_params=_cparams(("arbitrary", "arbitrary")),
    )(fq, fk, fvt, ccol)


def _compress_kernel(x_ref, pe_ref, w1_ref, w2_ref, o_ref, ot_ref):
    x = x_ref[0]
    n_grp = x.shape[0]
    top = _dot((x + pe_ref[0:1, :]).astype(bf16), w1_ref[0])
    bot = _dot((x + pe_ref[1:2, :]).astype(bf16), w1_ref[1])
    g = _gelu(top + pltpu.roll(bot, n_grp - 1, 0)).astype(bf16)
    w2 = w2_ref[...]
    kc = _dot(g[:, 0:CMP_HID], w2[:, 0:HEAD_DIM])
    vc = _dot(g[:, CMP_HID:2 * CMP_HID], w2[:, HEAD_DIM:2 * HEAD_DIM])
    cmp = jnp.concatenate([kc, vc], axis=-1)
    o_ref[0] = cmp.astype(bf16)
    ot_ref[0] = jnp.transpose(cmp).astype(bf16)


def _compress(x_groups, pe2, w1g, w2c):
    bsz, n_grp, w = x_groups.shape
    const = lambda a: pl.BlockSpec(a.shape, lambda b: (0,) * a.ndim)
    return pl.pallas_call(
        _compress_kernel,
        grid=(bsz,),
        in_specs=[pl.BlockSpec((1, n_grp, w), lambda b: (b, 0, 0)), const(pe2), const(w1g), const(w2c)],
        out_specs=[pl.BlockSpec((1, n_grp, 2 * HEAD_DIM), lambda b: (b, 0, 0)),
                   pl.BlockSpec((1, 2 * HEAD_DIM, n_grp), lambda b: (b, 0, 0))],
        out_shape=[jax.ShapeDtypeStruct((bsz, n_grp, 2 * HEAD_DIM), bf16),
                   jax.ShapeDtypeStruct((bsz, 2 * HEAD_DIM, n_grp), bf16)],
        compiler_params=_cparams(("arbitrary",)),
    )(x_groups, pe2, w1g, w2c)


def _stack_heads(q, rows):
    return jnp.concatenate([q[:, h * HEAD_DIM:(h + 1) * HEAD_DIM] for h in range(GROUP_HEADS)], axis=0)


def _unstack_heads(o, rows):
    return jnp.concatenate([o[h * rows:(h + 1) * rows] for h in range(GROUP_HEADS)], axis=-1)


def _gate_col(misc, branch):
    base = N_FOX_F + branch * GROUP_HEADS
    return jnp.concatenate([misc[:, base + h:base + h + 1] for h in range(GROUP_HEADS)], axis=0)


def _nsa_compressed(qs, cmp, qpos_s, selw, nq):
    n_c = cmp.shape[0]
    kc = cmp[:, 0:HEAD_DIM]
    vc = cmp[:, HEAD_DIM:2 * HEAD_DIM]
    s = _dot_nt(qs, kc)
    cend = lax.broadcasted_iota(jnp.int32, (1, n_c), 1) * CMP_STRIDE + (CMP_LEN - 1)
    vis = cend <= qpos_s
    s = jnp.where(vis, s, NEG)
    e = jnp.exp(s - jnp.max(s, axis=-1, keepdims=True))
    p = e / jnp.sum(e, axis=-1, keepdims=True)
    p = p * jnp.where(qpos_s >= CMP_LEN - 1, 1.0, 0.0)
    pb = p.astype(bf16)
    o_c = _dot(pb, vc)
    imp = _dot(pb[0:nq], selw)
    for h in range(1, GROUP_HEADS):
        imp = imp + _dot(pb[h * nq:(h + 1) * nq], selw)
    return o_c, imp


def _nsa_select(imp, qpos_row, n_sel_pad):
    nq = imp.shape[0]
    if nq < LANES:
        imp = jnp.concatenate([imp, jnp.zeros((LANES - nq, LANES), f32)], axis=0)
    sel_t = _nsa_select_t(jnp.transpose(imp)[0:n_sel_pad], qpos_row)
    if n_sel_pad < LANES:
        sel_t = jnp.concatenate([sel_t, jnp.zeros((LANES - n_sel_pad, LANES), f32)], axis=0)
    return jnp.transpose(sel_t)[0:nq]


def _nsa_select_t(imp_t, qpos_row):
    n_sel_pad = imp_t.shape[0]
    blk = lax.broadcasted_iota(jnp.int32, (n_sel_pad, LANES), 0)
    cur = qpos_row // SEL_LEN
    valid = blk * SEL_LEN <= qpos_row
    forced = (blk == 0) | (blk == cur) | (blk == cur - 1)
    val = jnp.where(valid, jnp.where(forced, BIG, imp_t), -BIG)
    n_g = n_sel_pad // 8
    grp = [val[8 * g:8 * g + 8] for g in range(n_g)]
    rank = [jnp.zeros((8, LANES), f32) for _ in range(n_g)]
    row8 = lax.broadcasted_iota(jnp.int32, (8, LANES), 0)
    for mp in range(n_sel_pad):
        vb = jnp.broadcast_to(val[mp:mp + 1, :], (8, LANES))
        for g in range(n_g):
            if 8 * g + 7 < mp:
                hit = jnp.where(vb > grp[g], 1.0, 0.0)
            elif 8 * g > mp:
                hit = jnp.where(vb >= grp[g], 1.0, 0.0)
            else:
                hit = jnp.where(row8 > mp - 8 * g, jnp.where(vb >= grp[g], 1.0, 0.0), jnp.where(vb > grp[g], 1.0, 0.0))
            rank[g] = rank[g] + hit
    rank = jnp.concatenate(rank, axis=0)
    return jnp.where(valid, jnp.where(rank < SEL_TOPK, 1.0, 0.0), 0.0)


def _nsa_prompt_kernel(q_ref, kse_ref, ft_ref, cmp_ref, cmpt_ref, misc_ref, selwt_ref, o_ref, m_sc, l_sc, acc_sc, *,
                       tk, t, n_sel_pad, bq):
    qi = pl.program_id(1)
    nq = LANES
    n_sub = bq // nq
    r_all = GROUP_HEADS * nq
    zq = jnp.zeros((HEAD_DIM, r_all), bf16)
    cmp = cmp_ref[0]
    n_c = cmp.shape[0]
    span = min(WINDOW + nq, t)
    selwt = selwt_ref[...]
    sub = [dict() for _ in range(n_sub)]

    def prepare(u):
        d = sub[u]
        q0 = qi * bq + u * nq
        q_t4 = jnp.transpose(q_ref[0, u * nq:(u + 1) * nq, :].astype(f32))
        q_t = jnp.concatenate([q_t4[h * HEAD_DIM:(h + 1) * HEAD_DIM] for h in range(GROUP_HEADS)], axis=1).astype(bf16)
        qpos1 = q0 + lax.broadcasted_iota(jnp.int32, (1, nq), 1)
        qpos = jnp.concatenate([qpos1] * GROUP_HEADS, axis=1)
        d['qpos'] = qpos
        w0 = pl.multiple_of(jnp.maximum(q0 - WINDOW, 0), LANES)
        q_win = jnp.concatenate([zq, q_t, jnp.zeros((LANES, r_all), bf16)], axis=0)
        s_w = _dot(kse_ref[0, pl.ds(w0, span), :], q_win)
        s_c = _dot(cmp, jnp.concatenate([q_t, zq], axis=0))
        yield
        wpos = w0 + lax.broadcasted_iota(jnp.int32, (span, 1), 0)
        s_w = jnp.where((wpos <= qpos) & (wpos > qpos - WINDOW), s_w, NEG)
        cend = lax.broadcasted_iota(jnp.int32, (n_c, 1), 0) * CMP_STRIDE + (CMP_LEN - 1)
        s_c = jnp.where(cend <= qpos, s_c, NEG)
        e_c = jnp.exp(s_c - jnp.max(s_c, axis=0, keepdims=True))
        p_c = e_c / jnp.sum(e_c, axis=0, keepdims=True)
        pb = (p_c * jnp.where(qpos >= CMP_LEN - 1, 1.0, 0.0)).astype(bf16)
        e_w = jnp.exp(s_w - jnp.max(s_w, axis=0, keepdims=True))
        p_w = (e_w / jnp.sum(e_w, axis=0, keepdims=True)).astype(bf16)
        yield
        d['o_c'] = _dot(cmpt_ref[0, HEAD_DIM:2 * HEAD_DIM, :], pb)
        d['o_w'] = _dot(ft_ref[0, HEAD_DIM:2 * HEAD_DIM, pl.ds(w0, span)], p_w)
        imp_t = _dot(selwt, pb[:, 0:nq])
        for h in range(1, GROUP_HEADS):
            imp_t = imp_t + _dot(selwt, pb[:, h * nq:(h + 1) * nq])
        yield
        sel_t = _nsa_select_t(imp_t[0:n_sel_pad], qpos1)
        unsel = jnp.where(sel_t > 0.5, 0.0, -(2.0 ** 100))
        if n_sel_pad < LANES:
            unsel = jnp.concatenate([unsel, jnp.zeros((LANES - n_sel_pad, nq), f32)], axis=0)
        unsel = jnp.concatenate([unsel.astype(bf16)] * GROUP_HEADS, axis=1)
        d['q_sel'] = jnp.concatenate([q_t, zq, unsel], axis=0)

    runs = [prepare(u) for u in range(n_sub)]
    done = object()
    while runs:
        runs = [run for run in runs if next(run, done) is not done]

    m_sc[...] = jnp.full_like(m_sc, NEG)
    l_sc[...] = jnp.zeros_like(l_sc)
    acc_sc[...] = jnp.zeros_like(acc_sc)

    n_half = 2
    rc = r_all // n_half
    chain_ids = [(u, c) for u in range(n_sub) for c in range(n_half)]

    def tile(j, diagonal):
        k0 = pl.multiple_of(j * tk, tk)
        kse = kse_ref[0, pl.ds(k0, tk), :]
        v_t = ft_ref[0, 0:HEAD_DIM, pl.ds(k0, tk)]
        chains = []
        for i, (u, c) in enumerate(chain_ids):
            sc = _dot(kse, sub[u]['q_sel'][:, c * rc:(c + 1) * rc])
            if diagonal:
                kpos = k0 + lax.broadcasted_iota(jnp.int32, (tk, 1), 0)
                sc = jnp.where(kpos <= sub[u]['qpos'][:, c * rc:(c + 1) * rc], sc, NEG)
            chains.append((sc, v_t, m_sc.at[i], l_sc.at[i], acc_sc.at[i]))
        _flash_update_t(chains)

    n_full = (qi * bq) // tk

    def body(j, carry):
        tile(j, False)
        return carry

    lax.fori_loop(0, n_full, body, 0)
    tile(n_full, True)

    for u in range(n_sub):
        o_s = jnp.concatenate([acc_sc[u * n_half + c] / l_sc[u * n_half + c] for c in range(n_half)], axis=1)
        misc_t = jnp.transpose(misc_ref[0, u * nq:(u + 1) * nq, :])

        def gate(branch):
            base = N_FOX_F + branch * GROUP_HEADS
            return jnp.concatenate([misc_t[base + h:base + h + 1, :] for h in range(GROUP_HEADS)], axis=1)

        o = gate(0) * sub[u]['o_c'] + gate(1) * o_s + gate(2) * sub[u]['o_w']
        o = jnp.concatenate([o[:, h * nq:(h + 1) * nq] for h in range(GROUP_HEADS)], axis=0)
        o_ref[0, u * nq:(u + 1) * nq, :] = jnp.transpose(o).astype(bf16)


def _nsa_prompt(nq_b, kse, feat_t, cmp, cmp_t, misc, selw_t, tk, bq):
    bsz, t, _ = nq_b.shape
    n_chain = 2 * (bq // LANES)
    rc = GROUP_HEADS * LANES // 2
    n_c = cmp.shape[1]
    n_sel_pad = -(-(t // SEL_LEN) // 8) * 8
    assert tk % bq == 0 and bq % LANES == 0
    return pl.pallas_call(
        functools.partial(_nsa_prompt_kernel, tk=tk, t=t, n_sel_pad=n_sel_pad, bq=bq),
        grid=(bsz, t // bq),
        in_specs=[
            pl.BlockSpec((1, bq, GROUP_W), lambda b, i: (b, i, 0)),
            pl.BlockSpec((1, t, GROUP_W), lambda b, i: (b, 0, 0)),
            pl.BlockSpec((1, 2 * HEAD_DIM, t), lambda b, i: (b, 0, 0)),
            pl.BlockSpec((1, n_c, 2 * HEAD_DIM), lambda b, i: (b, 0, 0)),
            pl.BlockSpec((1, 2 * HEAD_DIM, n_c), lambda b, i: (b, 0, 0)),
            pl.BlockSpec((1, bq, LANES), lambda b, i: (b, i, 0)),
            pl.BlockSpec(selw_t.shape, lambda b, i: (0, 0)),
        ],
        out_specs=pl.BlockSpec((1, bq, GROUP_W), lambda b, i: (b, i, 0)),
        out_shape=jax.ShapeDtypeStruct((bsz, t, GROUP_W), bf16),
        scratch_shapes=[pltpu.VMEM((n_chain, 1, rc), f32), pltpu.VMEM((n_chain, 1, rc), f32),
                        pltpu.VMEM((n_chain, HEAD_DIM, rc), f32)],
        compiler_params=_cparams(("arbitrary", "arbitrary")),
    )(nq_b, kse, feat_t, cmp, cmp_t, misc, selw_t)


def _matmul_kernel(x_ref, w_ref, o_ref):
    o_ref[...] = _dot(x_ref[...].astype(bf16), w_ref[...])


def _sample_inproj(x2, w_pack, tm):
    n, d = x2.shape
    pw = w_pack.shape[1]
    return pl.pallas_call(
        _matmul_kernel,
        grid=(n // tm,),
        in_specs=[pl.BlockSpec((tm, d), lambda i: (i, 0)), pl.BlockSpec((d, pw), lambda i: (0, 0))],
        out_specs=pl.BlockSpec((tm, pw), lambda i: (i, 0)),
        out_shape=jax.ShapeDtypeStruct((n, pw), f32),
        compiler_params=_cparams(("arbitrary",)),
    )(x2, w_pack)


def _pad_to(a, rows):
    return jnp.concatenate([a, jnp.zeros((rows - a.shape[0], a.shape[1]), a.dtype)], axis=0)


def _softmax_tiles(tiles):
    m = tiles[0].max(axis=-1, keepdims=True)
    for s in tiles[1:]:
        m = jnp.maximum(m, s.max(axis=-1, keepdims=True))
    ps = [jnp.exp(s - m) for s in tiles]
    tot = ps[0].sum(axis=-1, keepdims=True)
    for p in ps[1:]:
        tot = tot + p.sum(axis=-1, keepdims=True)
    return ps, tot


def _sample_mixer_kernel(pt_ref, h_ref, *refs, n_pages, page, past, n_sel_pad, n_seq, n_prev, prev_at):
    del pt_ref
    if n_prev:
        prev_ref = refs[prev_at]
        refs = refs[:prev_at] + refs[prev_at + 1:]
        refs[-4][0:n_prev] = prev_ref[...]
    runs = [_sample_mixer_seq(s, h_ref, refs, n_pages, page, past, n_sel_pad, n_seq, n_prev) for s in range(n_seq)]
    done = object()
    while runs:
        runs = [run for run in runs if next(run, done) is not done]


def _sample_mixer_seq(s, h_ref, refs, n_pages, page, past, n_sel_pad, n_seq, n_prev):
    per = n_seq * n_pages
    fox_pages = refs[s * n_pages:(s + 1) * n_pages]
    lf_pages = refs[per + s * n_pages:per + (s + 1) * n_pages]
    nsa_pages = refs[2 * per + s * n_pages:2 * per + (s + 1) * n_pages]
    (win_ref, conv_ref, fb_ref, wst_ref, bs_ref, cw_ref, avg_ref, utri_ref, ones_ref, slt_ref, tri8_ref,
     pe_ref, w1_ref, w2_ref, selw_ref, exp_ref,
     y_ref, fkv_ref, misc_ref, nrow_ref, wout_ref, cout_ref, gv_ref, rows_all) = refs[3 * per:]
    rows_sc = rows_all.at[s]
    tn = h_ref.shape[1]
    h = h_ref[s]
    qpos = past + lax.broadcasted_iota(jnp.int32, (tn, 1), 0)
    qpos_s = jnp.concatenate([qpos] * GROUP_HEADS, axis=0)
    new_pos = past + lax.broadcasted_iota(jnp.int32, (1, LANES), 1)
    new_ok = new_pos <= qpos_s

    ckv = h[:, CKV:CKV + 6 * HEAD_DIM]
    nrow_ref[s] = ckv[:, 0:4 * HEAD_DIM]
    for pg in range(n_pages):
        rows_sc[pg * page:(pg + 1) * page, :] = jnp.transpose(nsa_pages[pg][0, 0, 0:2 * HEAD_DIM, :])
    n_grp = n_pages * page // CMP_STRIDE
    half = CMP_LEN // 2
    tops = []
    bots = []
    for r in range(half):
        a_r = rows_sc[pl.ds(r, n_grp, stride=CMP_STRIDE), :]
        tops.append((a_r + pe_ref[r:r + 1, :]).astype(bf16))
        bots.append((a_r + pe_ref[half + r:half + r + 1, :]).astype(bf16))
    top = _dot(jnp.concatenate(tops, axis=1), w1_ref[0])
    bot = _dot(jnp.concatenate(bots, axis=1), w1_ref[1])
    g = _gelu(top + pltpu.roll(bot, n_grp - 1, 0)).astype(bf16)
    w2 = w2_ref[...]
    cmp = jnp.concatenate([_dot(g[:, 0:CMP_HID], w2[:, 0:HEAD_DIM]),
                           _dot(g[:, CMP_HID:2 * CMP_HID], w2[:, HEAD_DIM:2 * HEAD_DIM])], axis=-1).astype(bf16)
    yield
    qs = _stack_heads((h[:, CQ:CQ + GROUP_W] * SCALE).astype(bf16), tn)
    o_c, imp = _nsa_compressed(qs, cmp, qpos_s, selw_ref[...], tn)
    sel = _nsa_select(imp, new_pos, n_sel_pad).astype(bf16)

    pre = h[:, MISC:MISC + LANES] + fb_ref[0:1, :]
    lane = lax.broadcasted_iota(jnp.int32, (tn, LANES), 1)
    lf_new = jnp.where(lane < N_FOX_F, _log_sigmoid(pre), 0.0)
    misc = jnp.where(lane < N_FOX_F, lf_new, jnp.where(lane < N_FOX_F + N_GATES, jax.nn.sigmoid(pre), 0.0))
    misc_ref[s] = misc

    lf3 = _split3(jnp.concatenate([r[0, 0] for r in lf_pages], axis=0))
    tot = _dot_exact_lhs(lf3, ones_ref[...])
    off = _dot_exact_rhs(slt_ref[...], _split3(_pad_to(tot, LANES)))
    c_past = _dot_exact_lhs(lf3, utri_ref[...]) + off
    last = GROUP_HEADS * (n_pages - 1)
    total_past = off[last:last + GROUP_HEADS] + tot[last:last + GROUP_HEADS]
    c_new = jnp.transpose(_pad_to(total_past, LANES))[0:1, :] + _dot_exact_rhs(
        tri8_ref[...], _split3(_pad_to(lf_new, LANES)))
    cq = jnp.concatenate([c_new[:, hh:hh + 1] for hh in range(GROUP_HEADS)], axis=0)
    c_new_t = jnp.transpose(_pad_to(c_new, LANES))

    qbd =_head_blockdiag_q((h[:, BQ:BQ + GROUP_W] * SCALE).astype(bf16), tn)
    kv_new = h[:, BK:BK + 2 * GROUP_W]
    fkv_ref[s] = kv_new
    kvb_new = _pad_to(kv_new, LANES).astype(bf16)

    def head_rows(c4):
        return jnp.concatenate([jnp.broadcast_to(c4[hh:hh + 1, :], (tn, c4.shape[1])) for hh in range(GROUP_HEADS)],
                               axis=0)

    tiles = []
    for pg in range(n_pages):
        k_t = fox_pages[pg][0, 0, 0:GROUP_W, :].astype(bf16)
        tiles.append(_dot(qbd, k_t) + cq - head_rows(c_past[GROUP_HEADS * pg:GROUP_HEADS * (pg + 1)]))
    tiles.append(jnp.where(new_ok, _dot_nt(qbd, kvb_new[:, 0:GROUP_W]) + cq - head_rows(c_new_t[0:GROUP_HEADS]), NEG))
    yield
    ps, tot_p = _softmax_tiles(tiles)
    acc = _dot(ps[n_pages].astype(bf16), kvb_new[:, GROUP_W:2 * GROUP_W])
    for pg in range(n_pages):
        acc = acc + _dot_nt(ps[pg].astype(bf16), fox_pages[pg][0, 0, GROUP_W:2 * GROUP_W, :].astype(bf16))
    y_b = _head_diag(acc / tot_p, tn)

    u = _gelu(h[:, AU:AU + GROUP_W])
    vn = _head_layer_norm(_gelu(h[:, AV:AV + GROUP_W]), avg_ref[...])
    gv_ref[s] = vn
    w8 = jnp.concatenate([wst_ref[hh * CHUNK:hh * CHUNK + tn, :] for hh in range(GROUP_HEADS)], axis=0)
    z_all = _dot(w8, _pad_to(vn, CHUNK).astype(bf16))
    y_a = u * (_head_diag(z_all, tn) + bs_ref[0:tn, :])

    z = h[:, DC:DC + GROUP_W] * h[:, DH:DH + GROUP_W]
    zp = conv_ref[0, s]
    row = lax.broadcasted_iota(jnp.int32, (tn, GROUP_W), 0)
    z1 = jnp.where(row == 0, zp[1:2, :], pltpu.roll(z, 1, 0))
    z2 = jnp.where(row == 0, zp[0:1, :], jnp.where(row == 1, zp[1:2, :], pltpu.roll(z, 2, 0)))
    cw = cw_ref[...]
    y_d = h[:, DB:DB + GROUP_W] * (cw[0:1, :] * z2 + cw[1:2, :] * z1 + cw[2:3, :] * z)
    cout_ref[s] = z

    yield
    sel_new =_pad_to(ckv[:, 2 * HEAD_DIM:4 * HEAD_DIM], LANES).astype(bf16)
    tiles = []
    for pg in range(n_pages + 1):
        picked = _dot(sel, exp_ref[:, pg * page:(pg + 1) * page])
        if pg == n_pages:
            picked = jnp.where(new_pos <= qpos, picked, 0.0)
            sc = _dot_nt(qs, sel_new[:, 0:HEAD_DIM])
        else:
            sc = _dot(qs, nsa_pages[pg][0, 0, 2 * HEAD_DIM:3 * HEAD_DIM, :].astype(bf16))
        keep = jnp.concatenate([picked] * GROUP_HEADS, axis=0) > 0.5
        tiles.append(jnp.where(keep, sc, NEG))
    yield
    ps, tot_p = _softmax_tiles(tiles)
    acc = _dot(ps[n_pages].astype(bf16), sel_new[:, HEAD_DIM:2 * HEAD_DIM])
    for pg in range(n_pages):
        acc = acc + _dot_nt(ps[pg].astype(bf16), nsa_pages[pg][0, 0, 3 * HEAD_DIM:4 * HEAD_DIM, :].astype(bf16))
    o_s = acc / tot_p

    yield
    win_old = win_ref[0, s]
    win_new = ckv[:, 4 * HEAD_DIM:6 * HEAD_DIM]
    new_t = pltpu.roll(jnp.transpose(_pad_to(win_new, LANES)), LANES - tn, 1)
    new_wide = jnp.concatenate([jnp.zeros((2 * HEAD_DIM, WINDOW - LANES), f32), new_t], axis=1)
    w_idx = lax.broadcasted_iota(jnp.int32, (1, WINDOW), 1)
    wout_ref[n_prev, s] = jnp.where(w_idx >= WINDOW - tn, new_wide, pltpu.roll(win_old, WINDOW - tn, 1))
    wob = win_old.astype(bf16)
    wnb = _pad_to(win_new, LANES).astype(bf16)
    s_old = jnp.where(w_idx > qpos_s - past, _dot(qs, wob[0:HEAD_DIM, :]), NEG)
    s_new = jnp.where(new_ok, _dot_nt(qs, wnb[:, 0:HEAD_DIM]), NEG)
    ps, tot_p = _softmax_tiles([s_old, s_new])
    o_w = (_dot_nt(ps[0].astype(bf16), wob[HEAD_DIM:2 * HEAD_DIM, :])
           + _dot(ps[1].astype(bf16), wnb[:, HEAD_DIM:2 * HEAD_DIM])) / tot_p

    o = _gate_col(misc, 0) * o_c + _gate_col(misc, 1) * o_s + _gate_col(misc, 2) * o_w
    y_c = _unstack_heads(o, tn)
    y_ref[s] = jnp.concatenate([y_a, y_b, y_c, y_d], axis=-1)


def _sample_mixer(h3, fox_t, lf_t, nsa_t, win_t, conv4, page_table, layer, consts, prev_win=None, n_seq=2):
    n_all, tn, pw = h3.shape
    assert n_all % n_seq == 0
    bsz = n_all // n_seq
    n_pages = page_table.shape[1]
    page = fox_t.shape[3]
    past = n_pages * page
    n_sel_pad = -(-(-(-(past + tn) // SEL_LEN)) // 8) * 8

    def page_spec(arr, s, pg):
        blk = (1, 1) + arr.shape[2:]
        return pl.BlockSpec(blk, lambda b, pt: (layer, pt[n_seq * b + s, pg], 0, 0))

    in_specs = [pl.BlockSpec((n_seq, tn, pw), lambda b, pt: (b, 0, 0))]
    operands = [h3]
    for arr in (fox_t, lf_t, nsa_t):
        for s in range(n_seq):
            for pg in range(n_pages):
                in_specs.append(page_spec(arr, s, pg))
                operands.append(arr)
    in_specs.append(pl.BlockSpec((1, n_seq) + win_t.shape[2:], lambda b, pt: (layer, b, 0, 0)))
    operands.append(win_t)
    in_specs.append(pl.BlockSpec((1, n_seq) + conv4.shape[2:], lambda b, pt: (layer, b, 0, 0)))
    operands.append(conv4)
    for a in consts:
        in_specs.append(pl.BlockSpec(a.shape, lambda b, pt, nd=a.ndim: (0,) * nd))
        operands.append(a)
    out_dims = [(tn, 4 * GROUP_W), (tn, 2 * GROUP_W), (tn, LANES), (tn, 4 * HEAD_DIM), (2 * HEAD_DIM, WINDOW),
                (tn, GROUP_W), (tn, GROUP_W)]
    out_shape = [jax.ShapeDtypeStruct((n_all, r, w), f32) for r, w in out_dims]
    out_specs = [pl.BlockSpec((n_seq, r, w), lambda b, pt: (b, 0, 0)) for r, w in out_dims]
    n_prev = prev_win.shape[0] if prev_win is not None else 0
    prev_at = len(operands) - 1
    if n_prev:
        in_specs.append(pl.BlockSpec((n_prev, n_seq, 2 * HEAD_DIM, WINDOW), lambda b, pt: (0, b, 0, 0)))
        operands.append(prev_win)
    out_shape[4] = jax.ShapeDtypeStruct((n_prev + 1, n_all, 2 * HEAD_DIM, WINDOW), f32)
    out_specs[4] = pl.BlockSpec((n_prev + 1, n_seq, 2 * HEAD_DIM, WINDOW), lambda b, pt: (0, b, 0, 0))
    return pl.pallas_call(
        functools.partial(_sample_mixer_kernel, n_pages=n_pages, page=page, past=past, n_sel_pad=n_sel_pad,
                          n_seq=n_seq, n_prev=n_prev, prev_at=prev_at),
        grid_spec=pltpu.PrefetchScalarGridSpec(num_scalar_prefetch=1, grid=(bsz,), in_specs=in_specs,
                                               out_specs=out_specs,
                                               scratch_shapes=[pltpu.VMEM((n_seq, past, 2 * HEAD_DIM), f32)]),
        out_shape=out_shape,
        compiler_params=_cparams(("arbitrary",)),
    )(page_table, *operands)


def _outproj_kernel(x_ref, *refs, alpha):
    y_refs = refs[:-4]
    w_ref, g_ref, b_ref, o_ref = refs[-4:]
    acc = None
    c0 = 0
    for y_ref in y_refs:
        wy = y_ref.shape[1]
        part = _dot(y_ref[...].astype(bf16), w_ref[c0:c0 + wy, :])
        acc = part if acc is None else acc + part
        c0 += wy
    o_ref[...] = _layer_norm_rows(alpha * x_ref[...] + acc, g_ref[0:1, :], b_ref[0:1, :])


def _outproj(x2, ys, w_out_b, g8, b8, alpha, tm):
    n, d = x2.shape
    assert sum(y.shape[1] for y in ys) == d
    const = lambda a: pl.BlockSpec(a.shape, lambda i: (0,) * a.ndim)
    return pl.pallas_call(
        functools.partial(_outproj_kernel, alpha=alpha),
        grid=(n // tm,),
        in_specs=[pl.BlockSpec((tm, d), lambda i: (i, 0))] + [pl.BlockSpec((tm, y.shape[1]), lambda i: (i, 0)) for y in ys]
        + [const(w_out_b), const(g8), const(b8)],
        out_specs=pl.BlockSpec((tm, d), lambda i: (i, 0)),
        out_shape=jax.ShapeDtypeStruct((n, d), f32),
        compiler_params=_cparams(("arbitrary",)),
    )(x2, *ys, w_out_b, g8, b8)


def _route(xb, wr_t, tm):
    logits = _dot_nt(wr_t, xb)
    e = jnp.exp(logits - jnp.max(logits, axis=0, keepdims=True))
    probs = e / jnp.sum(e, axis=0, keepdims=True)
    p = [probs[i:i + 1, :] for i in range(N_EXPERTS)]
    one = jnp.ones((1, tm), f32)
    zero = jnp.zeros((1, tm), f32)
    in_top = []
    gscore = []
    for g in range(N_GROUPS):
        members = range(g * EXP_PER_GROUP, (g + 1) * EXP_PER_GROUP)
        score = zero
        for i in members:
            rank = zero
            for j in members:
                if j < i:
                    rank = rank + jnp.where(p[j] >= p[i], one, zero)
                elif j > i:
                    rank = rank + jnp.where(p[j] > p[i], one, zero)
            keep = jnp.where(rank < TOP_K, one, zero)
            in_top.append(keep)
            score = score + keep * p[i]
        gscore.append(score)
    gate_rows = []
    wsum = zero
    chosen = []
    for g in range(N_GROUPS):
        win = one
        for g2 in range(N_GROUPS):
            if g2 < g:
                win = win * jnp.where(gscore[g] > gscore[g2], one, zero)
            elif g2 > g:
                win = win * jnp.where(gscore[g] >= gscore[g2], one, zero)
        for i in range(g * EXP_PER_GROUP, (g + 1) * EXP_PER_GROUP):
            c = in_top[i] * win
            chosen.append(c)
            wsum = wsum + c * p[i]
    for i in range(N_EXPERTS):
        gate_rows.append(jnp.where(chosen[i] > 0.5, p[i] / wsum, zero))
    gate_t = jnp.concatenate(gate_rows + [jnp.zeros((LANES - N_EXPERTS, tm), f32)], axis=0)
    return jnp.concatenate([jnp.transpose(gate_t[:, c * LANES:(c + 1) * LANES]) for c in range(tm // LANES)], axis=0)


def _moe_kernel(x_ref, wr_ref, wg_ref, wu_ref, wd_ref, g_ref, b_ref, o_ref, gate_sc, acc_sc, xb_sc, *, alpha, tm):
    grp = pl.program_id(1)
    n_e = wg_ref.shape[0]

    @pl.when(grp == 0)
    def _():
        xb = x_ref[...].astype(bf16)
        xb_sc[...] = xb
        gate_sc[...] = _route(xb, wr_ref[...], tm)

    xb = xb_sc[...]
    lane = lax.broadcasted_iota(jnp.int32, (tm, LANES), 1)
    gate = gate_sc[...]
    gates = [_dot(xb, wg_ref[j]) for j in range(n_e)]
    ups = [_dot(xb, wu_ref[j]) for j in range(n_e)]
    hs = []
    for j in range(n_e):
        gcol = jnp.sum(jnp.where(lane == grp * n_e + j, gate, 0.0), axis=-1, keepdims=True)
        hs.append((jax.nn.silu(gates[j]) * ups[j] * gcol).astype(bf16))
    y = _dot(jnp.concatenate(hs, axis=-1), wd_ref[...].reshape(n_e * wd_ref.shape[1], wd_ref.shape[2]))

    @pl.when(grp == 0)
    def _():
        acc_sc[...] = y

    @pl.when(grp > 0)
    def _():
        acc_sc[...] += y

    @pl.when(grp == pl.num_programs(1) - 1)
    def _():
        o_ref[...] = _layer_norm_rows(alpha * x_ref[...] + acc_sc[...], g_ref[0:1, :], b_ref[0:1, :])


def _moe(x2, wr_t, wg, wu, wd, g8, b8, alpha, tm):
    n, d = x2.shape
    de = wg.shape[-1]
    es = EXP_PER_GROUP
    const = lambda a: pl.BlockSpec(a.shape, lambda i, e: (0,) * a.ndim)
    return pl.pallas_call(
        functools.partial(_moe_kernel, alpha=alpha, tm=tm),
        grid=(n // tm, N_EXPERTS // es),
        in_specs=[
            pl.BlockSpec((tm, d), lambda i, e: (i, 0)), const(wr_t),
            pl.BlockSpec((es, d, de), lambda i, e: (e, 0, 0)), pl.BlockSpec((es, d, de), lambda i, e: (e, 0, 0)),
            pl.BlockSpec((es, de, d), lambda i, e: (e, 0, 0)), const(g8), const(b8),
        ],
        out_specs=pl.BlockSpec((tm, d), lambda i, e: (i, 0)),
        out_shape=jax.ShapeDtypeStruct((n, d), f32),
        scratch_shapes=[pltpu.VMEM((tm, LANES), f32), pltpu.VMEM((tm, d), f32), pltpu.VMEM((tm, d), bf16)],
        compiler_params=_cparams(("arbitrary", "arbitrary")),
    )(x2, wr_t, wg, wu, wd, g8, b8)


def _pad_rows(a, rows=8):
    return jnp.concatenate([a, jnp.zeros((rows - a.shape[0],) + a.shape[1:], a.dtype)], axis=0)


def _prep_layer(l, w_in, fox_b_f, gmlp_w_s, gmlp_b_s, nsa_cmp_pe, nsa_cmp_w1, nsa_cmp_w2, conv_w, w_out, ln_g, ln_b,
                w_exp_gate, w_exp_up, w_exp_down):
    w = w_in[l]
    d = w.shape[0]
    w_pack = jnp.concatenate([w[:, 0:1280], w[:, 1284:1924], w[:, 1936:2704], w[:, 1280:1284], w[:, 1924:1936],
                              jnp.zeros((d, LANES - N_FOX_F - N_GATES), w.dtype)], axis=1).astype(bf16)
    fb_row = _pad_rows(jnp.concatenate([fox_b_f[l], jnp.zeros((LANES - N_FOX_F,), f32)])[None, :])
    tril = jnp.asarray(np.tril(np.ones((CHUNK, CHUNK), np.float32)))
    wst = (gmlp_w_s[l] * tril).reshape(GROUP_HEADS * CHUNK, CHUNK).astype(bf16)
    bs_full = jnp.repeat(gmlp_b_s[l].T, HEAD_DIM, axis=1)
    cw8 = _pad_rows(conv_w[l])
    half = CMP_LEN // 2
    w1r = nsa_cmp_w1[l].reshape(2, CMP_LEN, HEAD_DIM, CMP_HID)
    per = nsa_cmp_pe[l]
    w2c = jnp.concatenate([nsa_cmp_w2[l, 0], nsa_cmp_w2[l, 1]], axis=1).astype(bf16)
    zed = jnp.zeros((CMP_LEN, HEAD_DIM, CMP_HID), f32)
    w1r = jnp.concatenate([jnp.concatenate([w1r[0], zed], axis=2), jnp.concatenate([zed, w1r[1]], axis=2)],
                          axis=1).astype(bf16)
    pe32 = jnp.concatenate([per[0], per[1]], axis=1)
    w1g = w1r.reshape(2, half * 2 * HEAD_DIM, 2 * CMP_HID)
    pe2 = _pad_rows(pe32.reshape(2, half * 2 * HEAD_DIM))
    w_t = jnp.concatenate([w_pack[:, BK:BK + 2 * GROUP_W], w_pack[:, CKV:CKV + 6 * HEAD_DIM]], axis=1).T
    return dict(
        w_pack=w_pack, w_t=w_t, fb_row=fb_row, wst=wst, bs_full=bs_full, cw8=cw8, w1g=w1g, pe2=pe2, w2c=w2c, w1r=w1r,
        pe32=pe32,
        w_out=w_out[l].astype(bf16), g1=_pad_rows(ln_g[l, 0:1]), b1=_pad_rows(ln_b[l, 0:1]),
        g2=_pad_rows(ln_g[l, 1:2]), b2=_pad_rows(ln_b[l, 1:2]),
        wg=w_exp_gate[l].astype(bf16), wu=w_exp_up[l].astype(bf16), wd=w_exp_down[l].astype(bf16),
    )


def _sel_overlap(t, n_sel):
    n_cmp = (t - CMP_LEN) // CMP_STRIDE + 1
    start = np.arange(n_cmp) * CMP_STRIDE
    sstart = np.arange(n_sel) * SEL_LEN
    ov = np.minimum(start[:, None] + CMP_LEN, sstart[None, :] + SEL_LEN) - np.maximum(start[:, None], sstart[None, :])
    return np.clip(ov, 0, None) / CMP_STRIDE


def _head_avg():
    a = np.kron(np.eye(GROUP_HEADS, dtype=np.float32), np.full((HEAD_DIM, HEAD_DIM), 1.0 / HEAD_DIM, np.float32))
    return jnp.asarray(a, bf16)


def _post_block(x2, ys, p, wr_t, alpha, tm, tm_moe=1024):
    x1 = _outproj(x2, ys, p['w_out'], p['g1'], p['b1'], alpha, tm)
    assert x1.shape[0] % tm_moe == 0
    return _moe(x1, wr_t, p['wg'], p['wu'], p['wd'], p['g2'], p['b2'], alpha, tm_moe)


def _prompt_layer(x, p, wr_t, alpha, prev=None, tm=512, tk=512, bq=512, bq_fox=512):
    bsz, t, d = x.shape
    n = bsz * t
    n_sel = t // SEL_LEN
    n_grp = t // CMP_STRIDE
    assert t % tm == 0 and t % tk == 0 and n_sel <= LANES and t >= WINDOW + bq
    tri = jnp.asarray(np.triu(np.ones((tm, tm), np.float32)), bf16)
    expand_t = np.zeros((t, LANES), np.float32)
    expand_t[np.arange(t), np.arange(t) // SEL_LEN] = 1.0
    (ya, fq, fk, fvt, fkvt, misc, ccol, nq, nkvt, wint, kse, featt, kcvc, yd, cst) = _prompt_inproj(
        x, jnp.asarray(expand_t, bf16), p['w_pack'], p['w_t'], p['fb_row'], p['wst'], p['bs_full'], p['cw8'],
        _head_avg(), tri, tm, prev)
    yb = _fox_prompt(fq, fk, fvt, ccol, tk, bq_fox)
    cmp, cmp_t = _compress(kcvc.reshape(bsz, n_grp, CMP_STRIDE * 2 * HEAD_DIM), p['pe2'], p['w1g'], p['w2c'])
    ov = _sel_overlap(t, n_sel)
    selw_t = np.zeros((LANES, n_grp), np.float32)
    selw_t[:n_sel, :ov.shape[0]] = ov.T
    yc = _nsa_prompt(nq, kse, featt, cmp, cmp_t, misc, jnp.asarray(selw_t, bf16), tk, bq)
    ys = [a.reshape(n, GROUP_W) for a in (ya, yb, yc, yd)]
    x_out = _post_block(x.reshape(n, d), ys, p, wr_t, alpha, tm).reshape(bsz, t, d)
    win_keep = min(WINDOW, t)
    states = (misc[:, :, 0:N_FOX_F],
              wint[:, :, t - win_keep:].reshape(bsz, 2, HEAD_DIM, win_keep).transpose(0, 3, 1, 2),
              cst[:, 8 - (CONV_W - 1):])
    return x_out, states, (fkvt, nkvt), (ya, yb, yc, yd)


def _sample_consts(p, n_pages, page, tn):
    utri = np.triu(np.ones((page, page), np.float32))
    ones = np.ones((page, page), np.float32)
    r = np.arange(GROUP_HEADS * n_pages)
    slt = np.zeros((GROUP_HEADS * n_pages, LANES), np.float32)
    slt[:, :r.size] = ((r[:, None] % GROUP_HEADS) == (r[None, :] % GROUP_HEADS)) & (r[None, :] // GROUP_HEADS < r[:, None] // GROUP_HEADS)
    tri8 = np.zeros((tn, LANES), np.float32)
    tri8[:, :tn] = np.tril(np.ones((tn, tn), np.float32))
    t_all = n_pages * page + tn
    n_sel = -(-t_all // SEL_LEN)
    ov = _sel_overlap(t_all, n_sel)
    n_grp = n_pages * page // CMP_STRIDE
    assert ov.shape[0] <= n_grp and n_sel <= LANES
    selw = np.zeros((n_grp, LANES), np.float32)
    selw[:ov.shape[0], :n_sel] = ov
    n_keys = (n_pages + 1) * page
    expand = np.zeros((LANES, n_keys), np.float32)
    expand[np.arange(n_keys) // SEL_LEN, np.arange(n_keys)] = 1.0
    as_b = lambda a: jnp.asarray(np.asarray(a, np.float32), bf16)
    return [p['fb_row'], p['wst'], p['bs_full'], p['cw8'], _head_avg(), as_b(utri), as_b(ones), as_b(slt),
            as_b(tri8), p['pe32'], p['w1g'], p['w2c'], as_b(selw), as_b(expand)]


def _sample_layer(x, views, page_table, layer, p, wr_t, alpha, prev_win=None, tm=512):
    bsz, tn, d = x.shape
    n = bsz * tn
    fox_t, lf_t, nsa_t, win_t, conv4 = views
    n_pages = page_table.shape[1]
    page = fox_t.shape[3]
    assert tn == 8 and page == LANES and win_t.shape[3] == WINDOW and n_pages * page >= WINDOW
    assert GROUP_HEADS * n_pages <= LANES
    h = _sample_inproj(x.reshape(n, d), p['w_pack'], tm).reshape(bsz, tn, P_PACK)
    y, fkv, misc, nrow, wout, cout, gv = _sample_mixer(h, fox_t, lf_t, nsa_t, win_t, conv4, page_table, layer,
                                                       _sample_consts(p, n_pages, page, tn), prev_win)
    x_out = _post_block(x.reshape(n, d), [y.reshape(n, d)], p, wr_t, alpha, tm).reshape(bsz, tn, d)
    states = (fkv.reshape(bsz, tn, 2, GROUP_HEADS, HEAD_DIM), misc[:, :, 0:N_FOX_F], nrow.reshape(bsz, tn, 4, HEAD_DIM),
              cout[:, tn - (CONV_W - 1):], gv.reshape(bsz, tn, GROUP_HEADS, HEAD_DIM))
    return x_out, states, wout


def kernel(x_prompt, x_sample, cache_fox_kv, cache_fox_logf, cache_nsa_kv, state_nsa_win, state_conv, page_table,
           w_in, fox_b_f, gmlp_w_s, gmlp_b_s, nsa_cmp_pe, nsa_cmp_w1, nsa_cmp_w2, conv_w, w_out, ln_g, ln_b,
           w_router, w_exp_gate, w_exp_up, w_exp_down):
    depth, n_pool, page = cache_fox_kv.shape[0:3]
    dec_b = x_sample.shape[0]
    alpha = (2 * depth) ** 0.25
    wr_t = w_router.T.astype(bf16)
    views = (cache_fox_kv.transpose(0, 1, 3, 4, 5, 2).reshape(depth, n_pool, 2 * GROUP_W, page),
             cache_fox_logf.transpose(0, 1, 3, 2),
             cache_nsa_kv.transpose(0, 1, 3, 4, 2).reshape(depth, n_pool, 4 * HEAD_DIM, page),
             state_nsa_win.transpose(0, 1, 3, 4, 2).reshape(depth, dec_b, 2 * HEAD_DIM, state_nsa_win.shape[2]),
             state_conv)
    xp, xs = x_prompt, x_sample
    bsz, t = x_prompt.shape[0:2]
    st_p = [[] for _ in range(3)]
    st_s = [[] for _ in range(5)]
    stacks = None
    win_stack = None
    for l in range(depth):
        p = _prep_layer(l, w_in, fox_b_f, gmlp_w_s, gmlp_b_s, nsa_cmp_pe, nsa_cmp_w1, nsa_cmp_w2, conv_w, w_out,
                        ln_g, ln_b, w_exp_gate, w_exp_up, w_exp_down)
        xp, sp, stacks, _ = _prompt_layer(xp, p, wr_t, alpha, stacks)
        xs, ss, win_stack = _sample_layer(xs, views, page_table, l, p, wr_t, alpha, win_stack)
        for lst, a in zip(st_p, sp):
            lst.append(a)
        for lst, a in zip(st_s, ss):
            lst.append(a)
    fox_lf_p, nsa_win_p, conv_p = [jnp.stack(a, axis=0) for a in st_p]
    fox_kv_p = stacks[0].reshape(depth, bsz, 2, GROUP_HEADS, HEAD_DIM, t).transpose(0, 1, 5, 2, 3, 4)
    nsa_kv_p = stacks[1].reshape(depth, bsz, 4, HEAD_DIM, t).transpose(0, 1, 4, 2, 3)
    fox_kv_s, fox_lf_s, nsa_kv_s, conv_s, gmlp_v_s = [jnp.stack(a, axis=0) for a in st_s]
    nsa_win_s = win_stack.reshape(depth, dec_b, 2, HEAD_DIM, WINDOW).transpose(0, 1, 4, 2, 3)
    return (xp, xs, fox_kv_p, fox_kv_s, fox_lf_p, fox_lf_s, nsa_kv_p, nsa_kv_s, nsa_win_p, nsa_win_s, conv_p, conv_s,
            gmlp_v_s)
```

```python
import functools

import numpy as np
import jax
import jax.numpy as jnp
from jax import lax
from jax.experimental import pallas as pl
from jax.experimental.pallas import tpu as pltpu

f32 = jnp.float32
bf16 = jnp.bfloat16

HEAD_DIM = 64
GROUP_W = 256
GROUP_HEADS = 4
CHUNK = 128
CMP_LEN = 32
CMP_STRIDE = 16
CMP_HID = 128
SEL_LEN = 64
SEL_TOPK = 16
WINDOW = 512
CONV_W = 3
N_EXPERTS = 16
N_GROUPS = 4
EXP_PER_GROUP = 4
TOP_K = 2
LN_EPS = 1e-5
SCALE = HEAD_DIM ** -0.5
NEG = -1e30
BIG = 1e30
LANES = 128

AU, AV, BQ, BK, BV, CQ, CKV, DH, DB, DC, MISC = 0, 256, 512, 768, 1024, 1280, 1536, 1920, 2176, 2432, 2688
P_PACK = 2816
N_FOX_F = GROUP_HEADS
N_GATES = 3 * GROUP_HEADS

V7X_VMEM_BYTES = 64 * 1024 * 1024
VMEM_LIMIT = V7X_VMEM_BYTES * 7 // 8


def _cparams(sem):
    return pltpu.CompilerParams(dimension_semantics=sem, vmem_limit_bytes=VMEM_LIMIT)


def _dot(a, b):
    return jnp.dot(a, b, preferred_element_type=f32)


def _dot_nt(a, b):
    return lax.dot_general(a, b, (((1,), (1,)), ((), ())), preferred_element_type=f32)


def _split3(x):
    hi = x.astype(bf16)
    r = x - hi.astype(f32)
    mid = r.astype(bf16)
    lo = (r - mid.astype(f32)).astype(bf16)
    return hi, mid, lo


def _dot_exact_rhs(a, b3):
    return _dot(a, b3[0]) + _dot(a, b3[1]) + _dot(a, b3[2])


def _dot_exact_lhs(a3, b):
    return _dot(a3[0], b) + _dot(a3[1], b) + _dot(a3[2], b)


def _log_sigmoid(x):
    return jnp.minimum(x, 0.0) - jnp.log1p(jnp.exp(-jnp.abs(x)))


def _gelu(x):
    return jax.nn.gelu(x, approximate=True)


def _layer_norm_rows(x, g, b):
    mu = jnp.mean(x, axis=-1, keepdims=True)
    d = x - mu
    var = jnp.mean(d * d, axis=-1, keepdims=True)
    return d * lax.rsqrt(var + LN_EPS) * g + b


def _head_layer_norm(g, avg):
    h2 = g.astype(bf16)
    mu = _dot(h2, avg) + _dot((g - h2.astype(f32)).astype(bf16), avg)
    d = g - mu
    dd = d * d
    d2 = dd.astype(bf16)
    var = _dot(d2, avg) + _dot((dd - d2.astype(f32)).astype(bf16), avg)
    return d * lax.rsqrt(var + LN_EPS)


def _head_diag(z_all, rows):
    lane_head = lax.broadcasted_iota(jnp.int32, (rows, GROUP_W), 1) // HEAD_DIM
    out = jnp.where(lane_head == 0, z_all[0:rows], 0.0)
    for h in range(1, GROUP_HEADS):
        out = out + jnp.where(lane_head == h, z_all[h * rows:(h + 1) * rows], 0.0)
    return out


def _head_blockdiag_q(q, rows):
    lane_head = lax.broadcasted_iota(jnp.int32, (rows, GROUP_W), 1) // HEAD_DIM
    return jnp.concatenate([jnp.where(lane_head == h, q, jnp.zeros_like(q)) for h in range(GROUP_HEADS)], axis=0)


def _prompt_inproj_kernel(x_ref, exp_ref, w_ref, wt_ref, fb_ref, wst_ref, bs_ref, cw_ref, avg_ref, tri_ref, *rest,
                          tm, n_prev):
    prev = rest[:2] if n_prev else ()
    (ya_ref, fq_ref, fk_ref, fvt_ref, fkvt_ref, misc_ref, ccol_ref,
     nq_ref, nkvt_ref, wint_ref, kse_ref, featt_ref, kcvc_ref, yd_ref, cst_ref,
     carry_c, carry_z) = rest[len(prev):]
    if n_prev:
        fkvt_ref[0:n_prev] = prev[0][...]
        nkvt_ref[0:n_prev] = prev[1][...]
    ti = pl.program_id(1)

    @pl.when(ti == 0)
    def _():
        carry_c[...] = jnp.zeros_like(carry_c)
        carry_z[...] = jnp.zeros_like(carry_z)

    xb = x_ref[0].astype(bf16)

    u = _gelu(_dot(xb, w_ref[:, AU:AU + GROUP_W]))
    vn = _head_layer_norm(_gelu(_dot(xb, w_ref[:, AV:AV + GROUP_W])), avg_ref[...])
    wst = wst_ref[...]
    bsf = bs_ref[...]
    for c in range(tm // CHUNK):
        r0 = c * CHUNK
        z_all = _dot(wst, vn[r0:r0 + CHUNK].astype(bf16))
        ya_ref[0, r0:r0 + CHUNK, :] = (u[r0:r0 + CHUNK] * (_head_diag(z_all, CHUNK) + bsf)).astype(bf16)

    fq_ref[0] = (_dot(xb, w_ref[:, BQ:BQ + GROUP_W]) * SCALE).astype(bf16)
    res_t = _dot_nt(wt_ref[...], xb)
    fkvt_ref[n_prev, 0] = res_t[0:2 * GROUP_W]
    fvt_ref[0] = res_t[GROUP_W:2 * GROUP_W].astype(bf16)
    nkvt_ref[n_prev, 0] = res_t[2 * GROUP_W:3 * GROUP_W]
    wint_ref[0] = res_t[3 * GROUP_W:3 * GROUP_W + 2 * HEAD_DIM]
    featt_ref[0] = jnp.concatenate([res_t[3 * GROUP_W - HEAD_DIM:3 * GROUP_W],
                                    res_t[3 * GROUP_W + HEAD_DIM:3 * GROUP_W + 2 * HEAD_DIM]], axis=0).astype(bf16)
    pre =_dot(xb, w_ref[:, MISC:MISC + LANES]) + fb_ref[0:1, :]
    lane = lax.broadcasted_iota(jnp.int32, (tm, LANES), 1)
    lf = jnp.where(lane < N_FOX_F, _log_sigmoid(pre), 0.0)
    misc_ref[0] = jnp.where(lane < N_FOX_F, lf, jnp.where(lane < N_FOX_F + N_GATES, jax.nn.sigmoid(pre), 0.0))
    c_t = _dot_exact_lhs(_split3(jnp.transpose(lf)[0:8, :]), tri_ref[...]) + carry_c[:, 0:1]
    carry_c[...] = jnp.broadcast_to(c_t[:, tm - 1:tm], (8, LANES))
    ccol_ref[0] = jnp.transpose(jnp.concatenate([c_t, jnp.zeros((LANES - 8, tm), f32)], axis=0))
    c3 = [piece.astype(f32) for piece in _split3(c_t)]
    k_tok = jnp.transpose(res_t[0:GROUP_W])
    parts = []
    for g in range(GROUP_HEADS // 2):
        rows = [-c3[i][h:h + 1, :] for h in (2 * g, 2 * g + 1) for i in range(3)]
        aux_t = jnp.concatenate(rows + [jnp.ones((3, tm), f32), jnp.zeros((LANES - 9, tm), f32)], axis=0)
        parts += [k_tok[:, g * LANES:(g + 1) * LANES].astype(bf16), jnp.transpose(aux_t).astype(bf16)]
    fk_ref[0] = jnp.concatenate(parts, axis=-1)

    nq_ref[0] = (_dot(xb, w_ref[:, CQ:CQ + GROUP_W]) * SCALE).astype(bf16)
    nsa0 = 2 * GROUP_W
    kcvc_ref[0] = jnp.transpose(res_t[nsa0:nsa0 + 2 * HEAD_DIM])
    ks_kw = jnp.transpose(jnp.concatenate([res_t[nsa0 + 2 * HEAD_DIM:nsa0 + 3 * HEAD_DIM],
                                           res_t[nsa0 + 4 * HEAD_DIM:nsa0 + 5 * HEAD_DIM]], axis=0))
    kse_ref[0] = jnp.concatenate([ks_kw.astype(bf16), exp_ref[...]], axis=-1)

    z = _dot(xb, w_ref[:, DC:DC + GROUP_W]) * _dot(xb, w_ref[:, DH:DH + GROUP_W])
    row = lax.broadcasted_iota(jnp.int32, (tm, GROUP_W), 0)
    zp = carry_z[...]
    z1 = jnp.where(row == 0, zp[7:8, :], pltpu.roll(z, 1, 0))
    z2 = jnp.where(row == 0, zp[6:7, :], jnp.where(row == 1, zp[7:8, :], pltpu.roll(z, 2, 0)))
    cw = cw_ref[...]
    y_conv = cw[0:1, :] * z2 + cw[1:2, :] * z1 + cw[2:3, :] * z
    yd_ref[0] = (_dot(xb, w_ref[:, DB:DB + GROUP_W]) * y_conv).astype(bf16)
    carry_z[...] = z[tm - 8:tm, :]
    cst_ref[0] = z[tm - 8:tm, :]


def _prompt_inproj(x, expand_t, w_pack, w_t, fb_row, wst, bs_full, cw8, avg, tri, tm, prev):
    bsz, t, d = x.shape
    nt = t // tm
    n_prev = prev[0].shape[0] if prev else 0
    tok = lambda w, dt: (jax.ShapeDtypeStruct((bsz, t, w), dt), pl.BlockSpec((1, tm, w), lambda b, i: (b, i, 0)))
    feat = lambda r, dt: (jax.ShapeDtypeStruct((bsz, r, t), dt), pl.BlockSpec((1, r, tm), lambda b, i: (b, 0, i)))
    stack_spec = lambda n, r: pl.BlockSpec((n, 1, r, tm), lambda b, i: (0, b, 0, i))
    stacked = lambda r: (jax.ShapeDtypeStruct((n_prev + 1, bsz, r, t), f32), stack_spec(n_prev + 1, r))
    const = lambda a: pl.BlockSpec(a.shape, lambda b, i: (0,) * a.ndim)
    outs = [
        tok(GROUP_W, bf16),
        tok(GROUP_W, bf16),
        tok(2 * GROUP_W, bf16),
        feat(GROUP_W, bf16),
        stacked(2 * GROUP_W),
        tok(LANES, f32),
        tok(LANES, f32),
        tok(GROUP_W, bf16),
        stacked(4 * HEAD_DIM),
        feat(2 * HEAD_DIM, f32),
        tok(GROUP_W, bf16),
        feat(2 * HEAD_DIM, bf16),
        tok(2 * HEAD_DIM, f32),
        tok(GROUP_W, bf16),
        (jax.ShapeDtypeStruct((bsz, 8, GROUP_W), f32), pl.BlockSpec((1, 8, GROUP_W), lambda b, i: (b, 0, 0))),
    ]
    consts = [w_pack, w_t, fb_row, wst, bs_full, cw8, avg, tri]
    prev = list(prev) if prev else []
    return pl.pallas_call(
        functools.partial(_prompt_inproj_kernel, tm=tm, n_prev=n_prev),
        grid=(bsz, nt),
        in_specs=[pl.BlockSpec((1, tm, d), lambda b, i: (b, i, 0)), pl.BlockSpec((tm, LANES), lambda b, i: (i, 0))]
        + [const(a) for a in consts] + [stack_spec(n_prev, a.shape[2]) for a in prev],
        out_specs=[o[1] for o in outs],
        out_shape=[o[0] for o in outs],
        scratch_shapes=[pltpu.VMEM((8, LANES), f32), pltpu.VMEM((8, GROUP_W), f32)],
        compiler_params=_cparams(("arbitrary", "arbitrary")),
    )(x, expand_t, *consts, *prev)


def _flash_update_t(chains):
    parts = []
    for s, v_t, m_sc, l_sc, acc_sc in chains:
        m_old = m_sc[...]
        m_new = jnp.maximum(m_old, jnp.max(s, axis=0, keepdims=True))
        alpha = jnp.exp(m_old - m_new)
        p = jnp.exp(s - m_new)
        l_sc[...] = alpha * l_sc[...] + jnp.sum(p, axis=0, keepdims=True)
        m_sc[...] = m_new
        parts.append((alpha, p.astype(bf16)))
    for (s, v_t, m_sc, l_sc, acc_sc), (alpha, pb) in zip(chains, parts):
        acc_sc[...] = alpha * acc_sc[...] + _dot(v_t, pb)


def _fox_prompt_kernel(q_ref, k_ref, vt_ref, ccol_ref, o_ref, m_sc, l_sc, acc_sc, *, tk, bq):
    qi = pl.program_id(1)
    n_pair = GROUP_HEADS // 2
    pw = 2 * HEAD_DIM
    nq = LANES
    n_sub = bq // nq
    lane = lax.broadcasted_iota(jnp.int32, (nq, pw), 1)
    lane_hi = lane >= HEAD_DIM
    qbd, qpos = [], []
    for u in range(n_sub):
        q = q_ref[0, u * nq:(u + 1) * nq, :]
        cc = ccol_ref[0, u * nq:(u + 1) * nq, :]
        qp = qi * bq + u * nq + lax.broadcasted_iota(jnp.int32, (1, nq), 1)
        qpos.append(jnp.concatenate([qp, qp], axis=1))
        for g in range(n_pair):
            qg = q[:, g * pw:(g + 1) * pw]
            halves = []
            for i, h in enumerate((2 * g, 2 * g + 1)):
                hi, mid, lo = [piece.astype(f32) for piece in _split3(cc[:, h:h + 1])]
                aux = jnp.where((lane >= 3 * i) & (lane < 3 * i + 3), 1.0, 0.0)
                aux = jnp.where(lane == 6, hi, jnp.where(lane == 7, mid, jnp.where(lane == 8, lo, aux)))
                own = jnp.where(lane_hi, qg, jnp.zeros_like(qg)) if i else jnp.where(lane_hi, jnp.zeros_like(qg), qg)
                halves.append(jnp.concatenate([own, aux.astype(bf16)], axis=1))
            qbd.append(jnp.concatenate(halves, axis=0))
    m_sc[...] = jnp.full_like(m_sc, NEG)
    l_sc[...] = jnp.zeros_like(l_sc)
    acc_sc[...] = jnp.zeros_like(acc_sc)

    def tile(j, masked):
        k0 = pl.multiple_of(j * tk, tk)
        chains = []
        for g in range(n_pair):
            k = k_ref[0, pl.ds(k0, tk), 2 * g * pw:2 * (g + 1) * pw]
            v_t = vt_ref[0, g * pw:(g + 1) * pw, pl.ds(k0, tk)]
            for u in range(n_sub):
                c = u * n_pair + g
                s = _dot_nt(k, qbd[c])
                if masked:
                    kpos = k0 + lax.broadcasted_iota(jnp.int32, (tk, 1), 0)
                    s = jnp.where(kpos <= qpos[u], s, NEG)
                chains.append((s, v_t, m_sc.at[c], l_sc.at[c], acc_sc.at[c]))
        _flash_update_t(chains)

    n_full = (qi * bq) // tk

    def body(j, carry):
        tile(j, False)
        return carry

    lax.fori_loop(0, n_full, body, 0)
    tile(n_full, True)
    for u in range(n_sub):
        outs = []
        for g in range(n_pair):
            c = u * n_pair + g
            o = acc_sc[c] / l_sc[c]
            outs.append(o[0:HEAD_DIM, 0:nq])
            outs.append(o[HEAD_DIM:pw, nq:2 * nq])
        o_ref[0, u * nq:(u + 1) * nq, :] = jnp.transpose(jnp.concatenate(outs, axis=0)).astype(bf16)


def _fox_prompt(fq, fk, fvt, ccol, tk, bq):
    bsz, t, _ = fq.shape
    assert tk % bq == 0 and t % tk == 0 and bq % LANES == 0
    n_chain = (bq // LANES) * (GROUP_HEADS // 2)
    return pl.pallas_call(
        functools.partial(_fox_prompt_kernel, tk=tk, bq=bq),
        grid=(bsz, t // bq),
        in_specs=[
            pl.BlockSpec((1, bq, GROUP_W), lambda b, i: (b, i, 0)),
            pl.BlockSpec((1, t, 2 * GROUP_W), lambda b, i: (b, 0, 0)),
            pl.BlockSpec((1, GROUP_W, t), lambda b, i: (b, 0, 0)),
            pl.BlockSpec((1, bq, LANES), lambda b, i: (b, i, 0)),
        ],
        out_specs=pl.BlockSpec((1, bq, GROUP_W), lambda b, i: (b, i, 0)),
        out_shape=jax.ShapeDtypeStruct((bsz, t, GROUP_W), bf16),
        scratch_shapes=[pltpu.VMEM((n_chain, 1, 2 * LANES), f32), pltpu.VMEM((n_chain, 1, 2 * LANES), f32),
                        pltpu.VMEM((n_chain, 2 * HEAD_DIM, 2 * LANES), f32)],
        compiler_params=_cparams(("arbitrary", "arbitrary")),
    )(fq, fk, fvt, ccol)


def _compress_kernel(x_ref, pe_ref, w1_ref, w2_ref, o_ref, ot_ref):
    x = x_ref[0]
    n_grp = x.shape[0]
    top = _dot((x + pe_ref[0:1, :]).astype(bf16), w1_ref[0])
    bot = _dot((x + pe_ref[1:2, :]).astype(bf16), w1_ref[1])
    g = _gelu(top + pltpu.roll(bot, n_grp - 1, 0)).astype(bf16)
    w2 = w2_ref[...]
    kc = _dot(g[:, 0:CMP_HID], w2[:, 0:HEAD_DIM])
    vc = _dot(g[:, CMP_HID:2 * CMP_HID], w2[:, HEAD_DIM:2 * HEAD_DIM])
    cmp = jnp.concatenate([kc, vc], axis=-1)
    o_ref[0] = cmp.astype(bf16)
    ot_ref[0] = jnp.transpose(cmp).astype(bf16)


def _compress(x_groups, pe2, w1g, w2c):
    bsz, n_grp, w = x_groups.shape
    const = lambda a: pl.BlockSpec(a.shape, lambda b: (0,) * a.ndim)
    return pl.pallas_call(
        _compress_kernel,
        grid=(bsz,),
        in_specs=[pl.BlockSpec((1, n_grp, w), lambda b: (b, 0, 0)), const(pe2), const(w1g), const(w2c)],
        out_specs=[pl.BlockSpec((1, n_grp, 2 * HEAD_DIM), lambda b: (b, 0, 0)),
                   pl.BlockSpec((1, 2 * HEAD_DIM, n_grp), lambda b: (b, 0, 0))],
        out_shape=[jax.ShapeDtypeStruct((bsz, n_grp, 2 * HEAD_DIM), bf16),
                   jax.ShapeDtypeStruct((bsz, 2 * HEAD_DIM, n_grp), bf16)],
        compiler_params=_cparams(("arbitrary",)),
    )(x_groups, pe2, w1g, w2c)


def _stack_heads(q, rows):
    return jnp.concatenate([q[:, h * HEAD_DIM:(h + 1) * HEAD_DIM] for h in range(GROUP_HEADS)], axis=0)


def _unstack_heads(o, rows):
    return jnp.concatenate([o[h * rows:(h + 1) * rows] for h in range(GROUP_HEADS)], axis=-1)


def _gate_col(misc, branch):
    base = N_FOX_F + branch * GROUP_HEADS
    return jnp.concatenate([misc[:, base + h:base + h + 1] for h in range(GROUP_HEADS)], axis=0)


def _nsa_compressed(qs, cmp, qpos_s, selw, nq):
    n_c = cmp.shape[0]
    kc = cmp[:, 0:HEAD_DIM]
    vc = cmp[:, HEAD_DIM:2 * HEAD_DIM]
    s = _dot_nt(qs, kc)
    cend = lax.broadcasted_iota(jnp.int32, (1, n_c), 1) * CMP_STRIDE + (CMP_LEN - 1)
    vis = cend <= qpos_s
    s = jnp.where(vis, s, NEG)
    e = jnp.exp(s - jnp.max(s, axis=-1, keepdims=True))
    p = e / jnp.sum(e, axis=-1, keepdims=True)
    p = p * jnp.where(qpos_s >= CMP_LEN - 1, 1.0, 0.0)
    pb = p.astype(bf16)
    o_c = _dot(pb, vc)
    imp = _dot(pb[0:nq], selw)
    for h in range(1, GROUP_HEADS):
        imp = imp + _dot(pb[h * nq:(h + 1) * nq], selw)
    return o_c, imp


def _nsa_select(imp, qpos_row, n_sel_pad):
    nq = imp.shape[0]
    if nq < LANES:
        imp = jnp.concatenate([imp, jnp.zeros((LANES - nq, LANES), f32)], axis=0)
    sel_t = _nsa_select_t(jnp.transpose(imp)[0:n_sel_pad], qpos_row)
    if n_sel_pad < LANES:
        sel_t = jnp.concatenate([sel_t, jnp.zeros((LANES - n_sel_pad, LANES), f32)], axis=0)
    return jnp.transpose(sel_t)[0:nq]


def _nsa_select_t(imp_t, qpos_row):
    n_sel_pad = imp_t.shape[0]
    blk = lax.broadcasted_iota(jnp.int32, (n_sel_pad, LANES), 0)
    cur = qpos_row // SEL_LEN
    valid = blk * SEL_LEN <= qpos_row
    forced = (blk == 0) | (blk == cur) | (blk == cur - 1)
    val = jnp.where(valid, jnp.where(forced, BIG, imp_t), -BIG)
    n_g = n_sel_pad // 8
    grp = [val[8 * g:8 * g + 8] for g in range(n_g)]
    rank = [jnp.zeros((8, LANES), f32) for _ in range(n_g)]
    row8 = lax.broadcasted_iota(jnp.int32, (8, LANES), 0)
    for mp in range(n_sel_pad):
        vb = jnp.broadcast_to(val[mp:mp + 1, :], (8, LANES))
        for g in range(n_g):
            if 8 * g + 7 < mp:
                hit = jnp.where(vb > grp[g], 1.0, 0.0)
            elif 8 * g > mp:
                hit = jnp.where(vb >= grp[g], 1.0, 0.0)
            else:
                hit = jnp.where(row8 > mp - 8 * g, jnp.where(vb >= grp[g], 1.0, 0.0), jnp.where(vb > grp[g], 1.0, 0.0))
            rank[g] = rank[g] + hit
    rank = jnp.concatenate(rank, axis=0)
    return jnp.where(valid, jnp.where(rank < SEL_TOPK, 1.0, 0.0), 0.0)


def _nsa_prompt_kernel(q_ref, kse_ref, ft_ref, cmp_ref, cmpt_ref, misc_ref, selwt_ref, o_ref, m_sc, l_sc, acc_sc, *,
                       tk, t, n_sel_pad, bq):
    qi = pl.program_id(1)
    nq = LANES
    n_sub = bq // nq
    r_all = GROUP_HEADS * nq
    zq = jnp.zeros((HEAD_DIM, r_all), bf16)
    cmp = cmp_ref[0]
    n_c = cmp.shape[0]
    span = min(WINDOW + nq, t)
    selwt = selwt_ref[...]
    sub = [dict() for _ in range(n_sub)]

    def prepare(u):
        d = sub[u]
        q0 = qi * bq + u * nq
        q_t4 = jnp.transpose(q_ref[0, u * nq:(u + 1) * nq, :].astype(f32))
        q_t = jnp.concatenate([q_t4[h * HEAD_DIM:(h + 1) * HEAD_DIM] for h in range(GROUP_HEADS)], axis=1).astype(bf16)
        qpos1 = q0 + lax.broadcasted_iota(jnp.int32, (1, nq), 1)
        qpos = jnp.concatenate([qpos1] * GROUP_HEADS, axis=1)
        d['qpos'] = qpos
        w0 = pl.multiple_of(jnp.maximum(q0 - WINDOW, 0), LANES)
        q_win = jnp.concatenate([zq, q_t, jnp.zeros((LANES, r_all), bf16)], axis=0)
        s_w = _dot(kse_ref[0, pl.ds(w0, span), :], q_win)
        s_c = _dot(cmp, jnp.concatenate([q_t, zq], axis=0))
        yield
        wpos = w0 + lax.broadcasted_iota(jnp.int32, (span, 1), 0)
        s_w = jnp.where((wpos <= qpos) & (wpos > qpos - WINDOW), s_w, NEG)
        cend = lax.broadcasted_iota(jnp.int32, (n_c, 1), 0) * CMP_STRIDE + (CMP_LEN - 1)
        s_c = jnp.where(cend <= qpos, s_c, NEG)
        e_c = jnp.exp(s_c - jnp.max(s_c, axis=0, keepdims=True))
        p_c = e_c / jnp.sum(e_c, axis=0, keepdims=True)
        pb = (p_c * jnp.where(qpos >= CMP_LEN - 1, 1.0, 0.0)).astype(bf16)
        e_w = jnp.exp(s_w - jnp.max(s_w, axis=0, keepdims=True))
        p_w = (e_w / jnp.sum(e_w, axis=0, keepdims=True)).astype(bf16)
        yield
        d['o_c'] = _dot(cmpt_ref[0, HEAD_DIM:2 * HEAD_DIM, :], pb)
        d['o_w'] = _dot(ft_ref[0, HEAD_DIM:2 * HEAD_DIM, pl.ds(w0, span)], p_w)
        imp_t = _dot(selwt, pb[:, 0:nq])
        for h in range(1, GROUP_HEADS):
            imp_t = imp_t + _dot(selwt, pb[:, h * nq:(h + 1) * nq])
        yield
        sel_t = _nsa_select_t(imp_t[0:n_sel_pad], qpos1)
        unsel = jnp.where(sel_t > 0.5, 0.0, -(2.0 ** 100))
        if n_sel_pad < LANES:
            unsel = jnp.concatenate([unsel, jnp.zeros((LANES - n_sel_pad, nq), f32)], axis=0)
        unsel = jnp.concatenate([unsel.astype(bf16)] * GROUP_HEADS, axis=1)
        d['q_sel'] = jnp.concatenate([q_t, zq, unsel], axis=0)

    runs = [prepare(u) for u in range(n_sub)]
    done = object()
    while runs:
        runs = [run for run in runs if next(run, done) is not done]

    m_sc[...] = jnp.full_like(m_sc, NEG)
    l_sc[...] = jnp.zeros_like(l_sc)
    acc_sc[...] = jnp.zeros_like(acc_sc)

    n_half = 2
    rc = r_all // n_half
    chain_ids = [(u, c) for u in range(n_sub) for c in range(n_half)]

    def tile(j, diagonal):
        k0 = pl.multiple_of(j * tk, tk)
        kse = kse_ref[0, pl.ds(k0, tk), :]
        v_t = ft_ref[0, 0:HEAD_DIM, pl.ds(k0, tk)]
        chains = []
        for i, (u, c) in enumerate(chain_ids):
            sc = _dot(kse, sub[u]['q_sel'][:, c * rc:(c + 1) * rc])
            if diagonal:
                kpos = k0 + lax.broadcasted_iota(jnp.int32, (tk, 1), 0)
                sc = jnp.where(kpos <= sub[u]['qpos'][:, c * rc:(c + 1) * rc], sc, NEG)
            chains.append((sc, v_t, m_sc.at[i], l_sc.at[i], acc_sc.at[i]))
        _flash_update_t(chains)

    n_full = (qi * bq) // tk

    def body(j, carry):
        tile(j, False)
        return carry

    lax.fori_loop(0, n_full, body, 0)
    tile(n_full, True)

    for u in range(n_sub):
        o_s = jnp.concatenate([acc_sc[u * n_half + c] / l_sc[u * n_half + c] for c in range(n_half)], axis=1)
        misc_t = jnp.transpose(misc_ref[0, u * nq:(u + 1) * nq, :])

        def gate(branch):
            base = N_FOX_F + branch * GROUP_HEADS
            return jnp.concatenate([misc_t[base + h:base + h + 1, :] for h in range(GROUP_HEADS)], axis=1)

        o = gate(0) * sub[u]['o_c'] + gate(1) * o_s + gate(2) * sub[u]['o_w']
        o = jnp.concatenate([o[:, h * nq:(h + 1) * nq] for h in range(GROUP_HEADS)], axis=0)
        o_ref[0, u * nq:(u + 1) * nq, :] = jnp.transpose(o).astype(bf16)


def _nsa_prompt(nq_b, kse, feat_t, cmp, cmp_t, misc, selw_t, tk, bq):
    bsz, t, _ = nq_b.shape
    n_chain = 2 * (bq // LANES)
    rc = GROUP_HEADS * LANES // 2
    n_c = cmp.shape[1]
    n_sel_pad = -(-(t // SEL_LEN) // 8) * 8
    assert tk % bq == 0 and bq % LANES == 0
    return pl.pallas_call(
        functools.partial(_nsa_prompt_kernel, tk=tk, t=t, n_sel_pad=n_sel_pad, bq=bq),
        grid=(bsz, t // bq),
        in_specs=[
            pl.BlockSpec((1, bq, GROUP_W), lambda b, i: (b, i, 0)),
            pl.BlockSpec((1, t, GROUP_W), lambda b, i: (b, 0, 0)),
            pl.BlockSpec((1, 2 * HEAD_DIM, t), lambda b, i: (b, 0, 0)),
            pl.BlockSpec((1, n_c, 2 * HEAD_DIM), lambda b, i: (b, 0, 0)),
            pl.BlockSpec((1, 2 * HEAD_DIM, n_c), lambda b, i: (b, 0, 0)),
            pl.BlockSpec((1, bq, LANES), lambda b, i: (b, i, 0)),
            pl.BlockSpec(selw_t.shape, lambda b, i: (0, 0)),
        ],
        out_specs=pl.BlockSpec((1, bq, GROUP_W), lambda b, i: (b, i, 0)),
        out_shape=jax.ShapeDtypeStruct((bsz, t, GROUP_W), bf16),
        scratch_shapes=[pltpu.VMEM((n_chain, 1, rc), f32), pltpu.VMEM((n_chain, 1, rc), f32),
                        pltpu.VMEM((n_chain, HEAD_DIM, rc), f32)],
        compiler_params=_cparams(("arbitrary", "arbitrary")),
    )(nq_b, kse, feat_t, cmp, cmp_t, misc, selw_t)


def _matmul_kernel(x_ref, w_ref, o_ref):
    o_ref[...] = _dot(x_ref[...].astype(bf16), w_ref[...])


def _sample_inproj(x2, w_pack, tm):
    n, d = x2.shape
    pw = w_pack.shape[1]
    return pl.pallas_call(
        _matmul_kernel,
        grid=(n // tm,),
        in_specs=[pl.BlockSpec((tm, d), lambda i: (i, 0)), pl.BlockSpec((d, pw), lambda i: (0, 0))],
        out_specs=pl.BlockSpec((tm, pw), lambda i: (i, 0)),
        out_shape=jax.ShapeDtypeStruct((n, pw), f32),
        compiler_params=_cparams(("arbitrary",)),
    )(x2, w_pack)


def _pad_to(a, rows):
    return jnp.concatenate([a, jnp.zeros((rows - a.shape[0], a.shape[1]), a.dtype)], axis=0)


def _softmax_tiles(tiles):
    m = tiles[0].max(axis=-1, keepdims=True)
    for s in tiles[1:]:
        m = jnp.maximum(m, s.max(axis=-1, keepdims=True))
    ps = [jnp.exp(s - m) for s in tiles]
    tot = ps[0].sum(axis=-1, keepdims=True)
    for p in ps[1:]:
        tot = tot + p.sum(axis=-1, keepdims=True)
    return ps, tot


def _sample_mixer_kernel(pt_ref, h_ref, *refs, n_pages, page, past, n_sel_pad, n_seq, n_prev, prev_at):
    del pt_ref
    if n_prev:
        prev_ref = refs[prev_at]
        refs = refs[:prev_at] + refs[prev_at + 1:]
        refs[-4][0:n_prev] = prev_ref[...]
    runs = [_sample_mixer_seq(s, h_ref, refs, n_pages, page, past, n_sel_pad, n_seq, n_prev) for s in range(n_seq)]
    done = object()
    while runs:
        runs = [run for run in runs if next(run, done) is not done]


def _sample_mixer_seq(s, h_ref, refs, n_pages, page, past, n_sel_pad, n_seq, n_prev):
    per = n_seq * n_pages
    fox_pages = refs[s * n_pages:(s + 1) * n_pages]
    lf_pages = refs[per + s * n_pages:per + (s + 1) * n_pages]
    nsa_pages = refs[2 * per + s * n_pages:2 * per + (s + 1) * n_pages]
    (win_ref, conv_ref, fb_ref, wst_ref, bs_ref, cw_ref, avg_ref, utri_ref, ones_ref, slt_ref, tri8_ref,
     pe_ref, w1_ref, w2_ref, selw_ref, exp_ref,
     y_ref, fkv_ref, misc_ref, nrow_ref, wout_ref, cout_ref, gv_ref, rows_all) = refs[3 * per:]
    rows_sc = rows_all.at[s]
    tn = h_ref.shape[1]
    h = h_ref[s]
    qpos = past + lax.broadcasted_iota(jnp.int32, (tn, 1), 0)
    qpos_s = jnp.concatenate([qpos] * GROUP_HEADS, axis=0)
    new_pos = past + lax.broadcasted_iota(jnp.int32, (1, LANES), 1)
    new_ok = new_pos <= qpos_s

    ckv = h[:, CKV:CKV + 6 * HEAD_DIM]
    nrow_ref[s] = ckv[:, 0:4 * HEAD_DIM]
    for pg in range(n_pages):
        rows_sc[pg * page:(pg + 1) * page, :] = jnp.transpose(nsa_pages[pg][0, 0, 0:2 * HEAD_DIM, :])
    n_grp = n_pages * page // CMP_STRIDE
    half = CMP_LEN // 2
    tops = []
    bots = []
    for r in range(half):
        a_r = rows_sc[pl.ds(r, n_grp, stride=CMP_STRIDE), :]
        tops.append((a_r + pe_ref[r:r + 1, :]).astype(bf16))
        bots.append((a_r + pe_ref[half + r:half + r + 1, :]).astype(bf16))
    top = _dot(jnp.concatenate(tops, axis=1), w1_ref[0])
    bot = _dot(jnp.concatenate(bots, axis=1), w1_ref[1])
    g = _gelu(top + pltpu.roll(bot, n_grp - 1, 0)).astype(bf16)
    w2 = w2_ref[...]
    cmp = jnp.concatenate([_dot(g[:, 0:CMP_HID], w2[:, 0:HEAD_DIM]),
                           _dot(g[:, CMP_HID:2 * CMP_HID], w2[:, HEAD_DIM:2 * HEAD_DIM])], axis=-1).astype(bf16)
    yield
    qs = _stack_heads((h[:, CQ:CQ + GROUP_W] * SCALE).astype(bf16), tn)
    o_c, imp = _nsa_compressed(qs, cmp, qpos_s, selw_ref[...], tn)
    sel = _nsa_select(imp, new_pos, n_sel_pad).astype(bf16)

    pre = h[:, MISC:MISC + LANES] + fb_ref[0:1, :]
    lane = lax.broadcasted_iota(jnp.int32, (tn, LANES), 1)
    lf_new = jnp.where(lane < N_FOX_F, _log_sigmoid(pre), 0.0)
    misc = jnp.where(lane < N_FOX_F, lf_new, jnp.where(lane < N_FOX_F + N_GATES, jax.nn.sigmoid(pre), 0.0))
    misc_ref[s] = misc

    lf3 = _split3(jnp.concatenate([r[0, 0] for r in lf_pages], axis=0))
    tot = _dot_exact_lhs(lf3, ones_ref[...])
    off = _dot_exact_rhs(slt_ref[...], _split3(_pad_to(tot, LANES)))
    c_past = _dot_exact_lhs(lf3, utri_ref[...]) + off
    last = GROUP_HEADS * (n_pages - 1)
    total_past = off[last:last + GROUP_HEADS] + tot[last:last + GROUP_HEADS]
    c_new = jnp.transpose(_pad_to(total_past, LANES))[0:1, :] + _dot_exact_rhs(
        tri8_ref[...], _split3(_pad_to(lf_new, LANES)))
    cq = jnp.concatenate([c_new[:, hh:hh + 1] for hh in range(GROUP_HEADS)], axis=0)
    c_new_t = jnp.transpose(_pad_to(c_new, LANES))

    qbd =_head_blockdiag_q((h[:, BQ:BQ + GROUP_W] * SCALE).astype(bf16), tn)
    kv_new = h[:, BK:BK + 2 * GROUP_W]
    fkv_ref[s] = kv_new
    kvb_new = _pad_to(kv_new, LANES).astype(bf16)

    def head_rows(c4):
        return jnp.concatenate([jnp.broadcast_to(c4[hh:hh + 1, :], (tn, c4.shape[1])) for hh in range(GROUP_HEADS)],
                               axis=0)

    tiles = []
    for pg in range(n_pages):
        k_t = fox_pages[pg][0, 0, 0:GROUP_W, :].astype(bf16)
        tiles.append(_dot(qbd, k_t) + cq - head_rows(c_past[GROUP_HEADS * pg:GROUP_HEADS * (pg + 1)]))
    tiles.append(jnp.where(new_ok, _dot_nt(qbd, kvb_new[:, 0:GROUP_W]) + cq - head_rows(c_new_t[0:GROUP_HEADS]), NEG))
    yield
    ps, tot_p = _softmax_tiles(tiles)
    acc = _dot(ps[n_pages].astype(bf16), kvb_new[:, GROUP_W:2 * GROUP_W])
    for pg in range(n_pages):
        acc = acc + _dot_nt(ps[pg].astype(bf16), fox_pages[pg][0, 0, GROUP_W:2 * GROUP_W, :].astype(bf16))
    y_b = _head_diag(acc / tot_p, tn)

    u = _gelu(h[:, AU:AU + GROUP_W])
    vn = _head_layer_norm(_gelu(h[:, AV:AV + GROUP_W]), avg_ref[...])
    gv_ref[s] = vn
    w8 = jnp.concatenate([wst_ref[hh * CHUNK:hh * CHUNK + tn, :] for hh in range(GROUP_HEADS)], axis=0)
    z_all = _dot(w8, _pad_to(vn, CHUNK).astype(bf16))
    y_a = u * (_head_diag(z_all, tn) + bs_ref[0:tn, :])

    z = h[:, DC:DC + GROUP_W] * h[:, DH:DH + GROUP_W]
    zp = conv_ref[0, s]
    row = lax.broadcasted_iota(jnp.int32, (tn, GROUP_W), 0)
    z1 = jnp.where(row == 0, zp[1:2, :], pltpu.roll(z, 1, 0))
    z2 = jnp.where(row == 0, zp[0:1, :], jnp.where(row == 1, zp[1:2, :], pltpu.roll(z, 2, 0)))
    cw = cw_ref[...]
    y_d = h[:, DB:DB + GROUP_W] * (cw[0:1, :] * z2 + cw[1:2, :] * z1 + cw[2:3, :] * z)
    cout_ref[s] = z

    yield
    sel_new =_pad_to(ckv[:, 2 * HEAD_DIM:4 * HEAD_DIM], LANES).astype(bf16)
    tiles = []
    for pg in range(n_pages + 1):
        picked = _dot(sel, exp_ref[:, pg * page:(pg + 1) * page])
        if pg == n_pages:
            picked = jnp.where(new_pos <= qpos, picked, 0.0)
            sc = _dot_nt(qs, sel_new[:, 0:HEAD_DIM])
        else:
            sc = _dot(qs, nsa_pages[pg][0, 0, 2 * HEAD_DIM:3 * HEAD_DIM, :].astype(bf16))
        keep = jnp.concatenate([picked] * GROUP_HEADS, axis=0) > 0.5
        tiles.append(jnp.where(keep, sc, NEG))
    yield
    ps, tot_p = _softmax_tiles(tiles)
    acc = _dot(ps[n_pages].astype(bf16), sel_new[:, HEAD_DIM:2 * HEAD_DIM])
    for pg in range(n_pages):
        acc = acc + _dot_nt(ps[pg].astype(bf16), nsa_pages[pg][0, 0, 3 * HEAD_DIM:4 * HEAD_DIM, :].astype(bf16))
    o_s = acc / tot_p

    yield
    win_old = win_ref[0, s]
    win_new = ckv[:, 4 * HEAD_DIM:6 * HEAD_DIM]
    new_t = pltpu.roll(jnp.transpose(_pad_to(win_new, LANES)), LANES - tn, 1)
    new_wide = jnp.concatenate([jnp.zeros((2 * HEAD_DIM, WINDOW - LANES), f32), new_t], axis=1)
    w_idx = lax.broadcasted_iota(jnp.int32, (1, WINDOW), 1)
    wout_ref[n_prev, s] = jnp.where(w_idx >= WINDOW - tn, new_wide, pltpu.roll(win_old, WINDOW - tn, 1))
    wob = win_old.astype(bf16)
    wnb = _pad_to(win_new, LANES).astype(bf16)
    s_old = jnp.where(w_idx > qpos_s - past, _dot(qs, wob[0:HEAD_DIM, :]), NEG)
    s_new = jnp.where(new_ok, _dot_nt(qs, wnb[:, 0:HEAD_DIM]), NEG)
    ps, tot_p = _softmax_tiles([s_old, s_new])
    o_w = (_dot_nt(ps[0].astype(bf16), wob[HEAD_DIM:2 * HEAD_DIM, :])
           + _dot(ps[1].astype(bf16), wnb[:, HEAD_DIM:2 * HEAD_DIM])) / tot_p

    o = _gate_col(misc, 0) * o_c + _gate_col(misc, 1) * o_s + _gate_col(misc, 2) * o_w
    y_c = _unstack_heads(o, tn)
    y_ref[s] = jnp.concatenate([y_a, y_b, y_c, y_d], axis=-1)


def _sample_mixer(h3, fox_t, lf_t, nsa_t, win_t, conv4, page_table, layer, consts, prev_win=None, n_seq=2):
    n_all, tn, pw = h3.shape
    assert n_all % n_seq == 0
    bsz = n_all // n_seq
    n_pages = page_table.shape[1]
    page = fox_t.shape[3]
    past = n_pages * page
    n_sel_pad = -(-(-(-(past + tn) // SEL_LEN)) // 8) * 8

    def page_spec(arr, s, pg):
        blk = (1, 1) + arr.shape[2:]
        return pl.BlockSpec(blk, lambda b, pt: (layer, pt[n_seq * b + s, pg], 0, 0))

    in_specs = [pl.BlockSpec((n_seq, tn, pw), lambda b, pt: (b, 0, 0))]
    operands = [h3]
    for arr in (fox_t, lf_t, nsa_t):
        for s in range(n_seq):
            for pg in range(n_pages):
                in_specs.append(page_spec(arr, s, pg))
                operands.append(arr)
    in_specs.append(pl.BlockSpec((1, n_seq) + win_t.shape[2:], lambda b, pt: (layer, b, 0, 0)))
    operands.append(win_t)
    in_specs.append(pl.BlockSpec((1, n_seq) + conv4.shape[2:], lambda b, pt: (layer, b, 0, 0)))
    operands.append(conv4)
    for a in consts:
        in_specs.append(pl.BlockSpec(a.shape, lambda b, pt, nd=a.ndim: (0,) * nd))
        operands.append(a)
    out_dims = [(tn, 4 * GROUP_W), (tn, 2 * GROUP_W), (tn, LANES), (tn, 4 * HEAD_DIM), (2 * HEAD_DIM, WINDOW),
                (tn, GROUP_W), (tn, GROUP_W)]
    out_shape = [jax.ShapeDtypeStruct((n_all, r, w), f32) for r, w in out_dims]
    out_specs = [pl.BlockSpec((n_seq, r, w), lambda b, pt: (b, 0, 0)) for r, w in out_dims]
    n_prev = prev_win.shape[0] if prev_win is not None else 0
    prev_at = len(operands) - 1
    if n_prev:
        in_specs.append(pl.BlockSpec((n_prev, n_seq, 2 * HEAD_DIM, WINDOW), lambda b, pt: (0, b, 0, 0)))
        operands.append(prev_win)
    out_shape[4] = jax.ShapeDtypeStruct((n_prev + 1, n_all, 2 * HEAD_DIM, WINDOW), f32)
    out_specs[4] = pl.BlockSpec((n_prev + 1, n_seq, 2 * HEAD_DIM, WINDOW), lambda b, pt: (0, b, 0, 0))
    return pl.pallas_call(
        functools.partial(_sample_mixer_kernel, n_pages=n_pages, page=page, past=past, n_sel_pad=n_sel_pad,
                          n_seq=n_seq, n_prev=n_prev, prev_at=prev_at),
        grid_spec=pltpu.PrefetchScalarGridSpec(num_scalar_prefetch=1, grid=(bsz,), in_specs=in_specs,
                                               out_specs=out_specs,
                                               scratch_shapes=[pltpu.VMEM((n_seq, past, 2 * HEAD_DIM), f32)]),
        out_shape=out_shape,
        compiler_params=_cparams(("arbitrary",)),
    )(page_table, *operands)


def _outproj_kernel(x_ref, *refs, alpha):
    y_refs = refs[:-4]
    w_ref, g_ref, b_ref, o_ref = refs[-4:]
    acc = None
    c0 = 0
    for y_ref in y_refs:
        wy = y_ref.shape[1]
        part = _dot(y_ref[...].astype(bf16), w_ref[c0:c0 + wy, :])
        acc = part if acc is None else acc + part
        c0 += wy
    o_ref[...] = _layer_norm_rows(alpha * x_ref[...] + acc, g_ref[0:1, :], b_ref[0:1, :])


def _outproj(x2, ys, w_out_b, g8, b8, alpha, tm):
    n, d = x2.shape
    assert sum(y.shape[1] for y in ys) == d
    const = lambda a: pl.BlockSpec(a.shape, lambda i: (0,) * a.ndim)
    return pl.pallas_call(
        functools.partial(_outproj_kernel, alpha=alpha),
        grid=(n // tm,),
        in_specs=[pl.BlockSpec((tm, d), lambda i: (i, 0))] + [pl.BlockSpec((tm, y.shape[1]), lambda i: (i, 0)) for y in ys]
        + [const(w_out_b), const(g8), const(b8)],
        out_specs=pl.BlockSpec((tm, d), lambda i: (i, 0)),
        out_shape=jax.ShapeDtypeStruct((n, d), f32),
        compiler_params=_cparams(("arbitrary",)),
    )(x2, *ys, w_out_b, g8, b8)


def _route(xb, wr_t, tm):
    logits = _dot_nt(wr_t, xb)
    e = jnp.exp(logits - jnp.max(logits, axis=0, keepdims=True))
    probs = e / jnp.sum(e, axis=0, keepdims=True)
    p = [probs[i:i + 1, :] for i in range(N_EXPERTS)]
    one = jnp.ones((1, tm), f32)
    zero = jnp.zeros((1, tm), f32)
    in_top = []
    gscore = []
    for g in range(N_GROUPS):
        members = range(g * EXP_PER_GROUP, (g + 1) * EXP_PER_GROUP)
        score = zero
        for i in members:
            rank = zero
            for j in members:
                if j < i:
                    rank = rank + jnp.where(p[j] >= p[i], one, zero)
                elif j > i:
                    rank = rank + jnp.where(p[j] > p[i], one, zero)
            keep = jnp.where(rank < TOP_K, one, zero)
            in_top.append(keep)
            score = score + keep * p[i]
        gscore.append(score)
    gate_rows = []
    wsum = zero
    chosen = []
    for g in range(N_GROUPS):
        win = one
        for g2 in range(N_GROUPS):
            if g2 < g:
                win = win * jnp.where(gscore[g] > gscore[g2], one, zero)
            elif g2 > g:
                win = win * jnp.where(gscore[g] >= gscore[g2], one, zero)
        for i in range(g * EXP_PER_GROUP, (g + 1) * EXP_PER_GROUP):
            c = in_top[i] * win
            chosen.append(c)
            wsum = wsum + c * p[i]
    for i in range(N_EXPERTS):
        gate_rows.append(jnp.where(chosen[i] > 0.5, p[i] / wsum, zero))
    gate_t = jnp.concatenate(gate_rows + [jnp.zeros((LANES - N_EXPERTS, tm), f32)], axis=0)
    return jnp.concatenate([jnp.transpose(gate_t[:, c * LANES:(c + 1) * LANES]) for c in range(tm // LANES)], axis=0)


def _moe_kernel(x_ref, wr_ref, wg_ref, wu_ref, wd_ref, g_ref, b_ref, o_ref, gate_sc, acc_sc, xb_sc, *, alpha, tm):
    grp = pl.program_id(1)
    n_e = wg_ref.shape[0]

    @pl.when(grp == 0)
    def _():
        xb = x_ref[...].astype(bf16)
        xb_sc[...] = xb
        gate_sc[...] = _route(xb, wr_ref[...], tm)

    xb = xb_sc[...]
    lane = lax.broadcasted_iota(jnp.int32, (tm, LANES), 1)
    gate = gate_sc[...]
    gates = [_dot(xb, wg_ref[j]) for j in range(n_e)]
    ups = [_dot(xb, wu_ref[j]) for j in range(n_e)]
    hs = []
    for j in range(n_e):
        gcol = jnp.sum(jnp.where(lane == grp * n_e + j, gate, 0.0), axis=-1, keepdims=True)
        hs.append((jax.nn.silu(gates[j]) * ups[j] * gcol).astype(bf16))
    y = _dot(jnp.concatenate(hs, axis=-1), wd_ref[...].reshape(n_e * wd_ref.shape[1], wd_ref.shape[2]))

    @pl.when(grp == 0)
    def _():
        acc_sc[...] = y

    @pl.when(grp > 0)
    def _():
        acc_sc[...] += y

    @pl.when(grp == pl.num_programs(1) - 1)
    def _():
        o_ref[...] = _layer_norm_rows(alpha * x_ref[...] + acc_sc[...], g_ref[0:1, :], b_ref[0:1, :])


def _moe(x2, wr_t, wg, wu, wd, g8, b8, alpha, tm):
    n, d = x2.shape
    de = wg.shape[-1]
    es = EXP_PER_GROUP
    const = lambda a: pl.BlockSpec(a.shape, lambda i, e: (0,) * a.ndim)
    return pl.pallas_call(
        functools.partial(_moe_kernel, alpha=alpha, tm=tm),
        grid=(n // tm, N_EXPERTS // es),
        in_specs=[
            pl.BlockSpec((tm, d), lambda i, e: (i, 0)), const(wr_t),
            pl.BlockSpec((es, d, de), lambda i, e: (e, 0, 0)), pl.BlockSpec((es, d, de), lambda i, e: (e, 0, 0)),
            pl.BlockSpec((es, de, d), lambda i, e: (e, 0, 0)), const(g8), const(b8),
        ],
        out_specs=pl.BlockSpec((tm, d), lambda i, e: (i, 0)),
        out_shape=jax.ShapeDtypeStruct((n, d), f32),
        scratch_shapes=[pltpu.VMEM((tm, LANES), f32), pltpu.VMEM((tm, d), f32), pltpu.VMEM((tm, d), bf16)],
        compiler_params=_cparams(("arbitrary", "arbitrary")),
    )(x2, wr_t, wg, wu, wd, g8, b8)


def _pad_rows(a, rows=8):
    return jnp.concatenate([a, jnp.zeros((rows - a.shape[0],) + a.shape[1:], a.dtype)], axis=0)


def _prep_layer(l, w_in, fox_b_f, gmlp_w_s, gmlp_b_s, nsa_cmp_pe, nsa_cmp_w1, nsa_cmp_w2, conv_w, w_out, ln_g, ln_b,
                w_exp_gate, w_exp_up, w_exp_down):
    w = w_in[l]
    d = w.shape[0]
    w_pack = jnp.concatenate([w[:, 0:1280], w[:, 1284:1924], w[:, 1936:2704], w[:, 1280:1284], w[:, 1924:1936],
                              jnp.zeros((d, LANES - N_FOX_F - N_GATES), w.dtype)], axis=1).astype(bf16)
    fb_row = _pad_rows(jnp.concatenate([fox_b_f[l], jnp.zeros((LANES - N_FOX_F,), f32)])[None, :])
    tril = jnp.asarray(np.tril(np.ones((CHUNK, CHUNK), np.float32)))
    wst = (gmlp_w_s[l] * tril).reshape(GROUP_HEADS * CHUNK, CHUNK).astype(bf16)
    bs_full = jnp.repeat(gmlp_b_s[l].T, HEAD_DIM, axis=1)
    cw8 = _pad_rows(conv_w[l])
    half = CMP_LEN // 2
    w1r = nsa_cmp_w1[l].reshape(2, CMP_LEN, HEAD_DIM, CMP_HID)
    per = nsa_cmp_pe[l]
    w2c = jnp.concatenate([nsa_cmp_w2[l, 0], nsa_cmp_w2[l, 1]], axis=1).astype(bf16)
    zed = jnp.zeros((CMP_LEN, HEAD_DIM, CMP_HID), f32)
    w1r = jnp.concatenate([jnp.concatenate([w1r[0], zed], axis=2), jnp.concatenate([zed, w1r[1]], axis=2)],
                          axis=1).astype(bf16)
    pe32 = jnp.concatenate([per[0], per[1]], axis=1)
    w1g = w1r.reshape(2, half * 2 * HEAD_DIM, 2 * CMP_HID)
    pe2 = _pad_rows(pe32.reshape(2, half * 2 * HEAD_DIM))
    w_t = jnp.concatenate([w_pack[:, BK:BK + 2 * GROUP_W], w_pack[:, CKV:CKV + 6 * HEAD_DIM]], axis=1).T
    return dict(
        w_pack=w_pack, w_t=w_t, fb_row=fb_row, wst=wst, bs_full=bs_full, cw8=cw8, w1g=w1g, pe2=pe2, w2c=w2c, w1r=w1r,
        pe32=pe32,
        w_out=w_out[l].astype(bf16), g1=_pad_rows(ln_g[l, 0:1]), b1=_pad_rows(ln_b[l, 0:1]),
        g2=_pad_rows(ln_g[l, 1:2]), b2=_pad_rows(ln_b[l, 1:2]),
        wg=w_exp_gate[l].astype(bf16), wu=w_exp_up[l].astype(bf16), wd=w_exp_down[l].astype(bf16),
    )


def _sel_overlap(t, n_sel):
    n_cmp = (t - CMP_LEN) // CMP_STRIDE + 1
    start = np.arange(n_cmp) * CMP_STRIDE
    sstart = np.arange(n_sel) * SEL_LEN
    ov = np.minimum(start[:, None] + CMP_LEN, sstart[None, :] + SEL_LEN) - np.maximum(start[:, None], sstart[None, :])
    return np.clip(ov, 0, None) / CMP_STRIDE


def _head_avg():
    a = np.kron(np.eye(GROUP_HEADS, dtype=np.float32), np.full((HEAD_DIM, HEAD_DIM), 1.0 / HEAD_DIM, np.float32))
    return jnp.asarray(a, bf16)


def _post_block(x2, ys, p, wr_t, alpha, tm, tm_moe=1024):
    x1 = _outproj(x2, ys, p['w_out'], p['g1'], p['b1'], alpha, tm)
    assert x1.shape[0] % tm_moe == 0
    return _moe(x1, wr_t, p['wg'], p['wu'], p['wd'], p['g2'], p['b2'], alpha, tm_moe)


def _prompt_layer(x, p, wr_t, alpha, prev=None, tm=512, tk=512, bq=512, tk_fox=1024, bq_fox=1024):
    bsz, t, d = x.shape
    n = bsz * t
    n_sel = t // SEL_LEN
    n_grp = t // CMP_STRIDE
    assert t % tm == 0 and t % tk == 0 and n_sel <= LANES and t >= WINDOW + bq
    tri = jnp.asarray(np.triu(np.ones((tm, tm), np.float32)), bf16)
    expand_t = np.zeros((t, LANES), np.float32)
    expand_t[np.arange(t), np.arange(t) // SEL_LEN] = 1.0
    (ya, fq, fk, fvt, fkvt, misc, ccol, nq, nkvt, wint, kse, featt, kcvc, yd, cst) = _prompt_inproj(
        x, jnp.asarray(expand_t, bf16), p['w_pack'], p['w_t'], p['fb_row'], p['wst'], p['bs_full'], p['cw8'],
        _head_avg(), tri, tm, prev)
    yb = _fox_prompt(fq, fk, fvt, ccol, tk_fox, bq_fox)
    cmp, cmp_t = _compress(kcvc.reshape(bsz, n_grp, CMP_STRIDE * 2 * HEAD_DIM), p['pe2'], p['w1g'], p['w2c'])
    ov = _sel_overlap(t, n_sel)
    selw_t = np.zeros((LANES, n_grp), np.float32)
    selw_t[:n_sel, :ov.shape[0]] = ov.T
    yc = _nsa_prompt(nq, kse, featt, cmp, cmp_t, misc, jnp.asarray(selw_t, bf16), tk, bq)
    ys = [a.reshape(n, GROUP_W) for a in (ya, yb, yc, yd)]
    x_out = _post_block(x.reshape(n, d), ys, p, wr_t, alpha, tm).reshape(bsz, t, d)
    win_keep = min(WINDOW, t)
    states = (misc[:, :, 0:N_FOX_F],
              wint[:, :, t - win_keep:].reshape(bsz, 2, HEAD_DIM, win_keep).transpose(0, 3, 1, 2),
              cst[:, 8 - (CONV_W - 1):])
    return x_out, states, (fkvt, nkvt), (ya, yb, yc, yd)


def _sample_consts(p, n_pages, page, tn):
    utri = np.triu(np.ones((page, page), np.float32))
    ones = np.ones((page, page), np.float32)
    r = np.arange(GROUP_HEADS * n_pages)
    slt = np.zeros((GROUP_HEADS * n_pages, LANES), np.float32)
    slt[:, :r.size] = ((r[:, None] % GROUP_HEADS) == (r[None, :] % GROUP_HEADS)) & (r[None, :] // GROUP_HEADS < r[:, None] // GROUP_HEADS)
    tri8 = np.zeros((tn, LANES), np.float32)
    tri8[:, :tn] = np.tril(np.ones((tn, tn), np.float32))
    t_all = n_pages * page + tn
    n_sel = -(-t_all // SEL_LEN)
    ov = _sel_overlap(t_all, n_sel)
    n_grp = n_pages * page // CMP_STRIDE
    assert ov.shape[0] <= n_grp and n_sel <= LANES
    selw = np.zeros((n_grp, LANES), np.float32)
    selw[:ov.shape[0], :n_sel] = ov
    n_keys = (n_pages + 1) * page
    expand = np.zeros((LANES, n_keys), np.float32)
    expand[np.arange(n_keys) // SEL_LEN, np.arange(n_keys)] = 1.0
    as_b = lambda a: jnp.asarray(np.asarray(a, np.float32), bf16)
    return [p['fb_row'], p['wst'], p['bs_full'], p['cw8'], _head_avg(), as_b(utri), as_b(ones), as_b(slt),
            as_b(tri8), p['pe32'], p['w1g'], p['w2c'], as_b(selw), as_b(expand)]


def _sample_layer(x, views, page_table, layer, p, wr_t, alpha, prev_win=None, tm=512):
    bsz, tn, d = x.shape
    n = bsz * tn
    fox_t, lf_t, nsa_t, win_t, conv4 = views
    n_pages = page_table.shape[1]
    page = fox_t.shape[3]
    assert tn == 8 and page == LANES and win_t.shape[3] == WINDOW and n_pages * page >= WINDOW
    assert GROUP_HEADS * n_pages <= LANES
    h = _sample_inproj(x.reshape(n, d), p['w_pack'], tm).reshape(bsz, tn, P_PACK)
    y, fkv, misc, nrow, wout, cout, gv = _sample_mixer(h, fox_t, lf_t, nsa_t, win_t, conv4, page_table, layer,
                                                       _sample_consts(p, n_pages, page, tn), prev_win)
    x_out = _post_block(x.reshape(n, d), [y.reshape(n, d)], p, wr_t, alpha, tm).reshape(bsz, tn, d)
    states = (fkv.reshape(bsz, tn, 2, GROUP_HEADS, HEAD_DIM), misc[:, :, 0:N_FOX_F], nrow.reshape(bsz, tn, 4, HEAD_DIM),
              cout[:, tn - (CONV_W - 1):], gv.reshape(bsz, tn, GROUP_HEADS, HEAD_DIM))
    return x_out, states, wout


def kernel(x_prompt, x_sample, cache_fox_kv, cache_fox_logf, cache_nsa_kv, state_nsa_win, state_conv, page_table,
           w_in, fox_b_f, gmlp_w_s, gmlp_b_s, nsa_cmp_pe, nsa_cmp_w1, nsa_cmp_w2, conv_w, w_out, ln_g, ln_b,
           w_router, w_exp_gate, w_exp_up, w_exp_down):
    depth, n_pool, page = cache_fox_kv.shape[0:3]
    dec_b = x_sample.shape[0]
    alpha = (2 * depth) ** 0.25
    wr_t = w_router.T.astype(bf16)
    views = (cache_fox_kv.transpose(0, 1, 3, 4, 5, 2).reshape(depth, n_pool, 2 * GROUP_W, page),
             cache_fox_logf.transpose(0, 1, 3, 2),
             cache_nsa_kv.transpose(0, 1, 3, 4, 2).reshape(depth, n_pool, 4 * HEAD_DIM, page),
             state_nsa_win.transpose(0, 1, 3, 4, 2).reshape(depth, dec_b, 2 * HEAD_DIM, state_nsa_win.shape[2]),
             state_conv)
    xp, xs = x_prompt, x_sample
    bsz, t = x_prompt.shape[0:2]
    st_p = [[] for _ in range(3)]
    st_s = [[] for _ in range(5)]
    stacks = None
    win_stack = None
    for l in range(depth):
        p = _prep_layer(l, w_in, fox_b_f, gmlp_w_s, gmlp_b_s, nsa_cmp_pe, nsa_cmp_w1, nsa_cmp_w2, conv_w, w_out,
                        ln_g, ln_b, w_exp_gate, w_exp_up, w_exp_down)
        xp, sp, stacks, _ = _prompt_layer(xp, p, wr_t, alpha, stacks)
        xs, ss, win_stack = _sample_layer(xs, views, page_table, l, p, wr_t, alpha, win_stack)
        for lst, a in zip(st_p, sp):
            lst.append(a)
        for lst, a in zip(st_s, ss):
            lst.append(a)
    fox_lf_p, nsa_win_p, conv_p = [jnp.stack(a, axis=0) for a in st_p]
    fox_kv_p = stacks[0].reshape(depth, bsz, 2, GROUP_HEADS, HEAD_DIM, t).transpose(0, 1, 5, 2, 3, 4)
    nsa_kv_p = stacks[1].reshape(depth, bsz, 4, HEAD_DIM, t).transpose(0, 1, 4, 2, 3)
    fox_kv_s, fox_lf_s, nsa_kv_s, conv_s, gmlp_v_s = [jnp.stack(a, axis=0) for a in st_s]
    nsa_win_s = win_stack.reshape(depth, dec_b, 2, HEAD_DIM, WINDOW).transpose(0, 1, 4, 2, 3)
    return (xp, xs, fox_kv_p, fox_kv_s, fox_lf_p, fox_lf_s, nsa_kv_p, nsa_kv_s, nsa_win_p, nsa_win_s, conv_p, conv_s,
            gmlp_v_s)
```
